```python
import math
import jax
import jax.numpy as jnp
from jax import lax
import numpy as np

D_MODEL = 1024
BATCH = 8
SEQ = 2048
DEPTH = 2
DEC_BATCH = 128
DEC_SEQ = 1
PAST_LEN = 16384
PAGE_SIZE = 128

N_HEADS = 4
HEAD_DIM = 64
BRANCH_W = N_HEADS * HEAD_DIM
N_BRANCH = 4
NSA_CMP_BLOCK = 32
NSA_SEL_BLOCK = 64
NSA_TOP_N = 16
NSA_WINDOW = 512
FORCE_SCORE = 1e9
IDX_HEADS = 4
IDX_DIM = 64
DSA_TOPK_MAX = 256
HGRN_CHUNK = 16
Q_LORA = 256
KV_LORA = 128
QK_NOPE = 64
QK_ROPE = 32
V_DIM = 64
MLA_HEAD_QK = QK_NOPE + QK_ROPE
ROPE_THETA = 10000.0
REL_BUCKETS = 32
REL_MAX_DIST = 512
N_BIAS_HEADS = 2 * N_HEADS
D_FF = 2816
CONV_W = 3
EPS = 1e-6
Q_BLOCK = 128
IN_WIDTHS = (BRANCH_W, 6 * HEAD_DIM, 3 * N_HEADS,
             BRANCH_W, 2 * HEAD_DIM, IDX_HEADS * IDX_DIM, IDX_DIM, IDX_HEADS,
             4 * BRANCH_W,
             Q_LORA, KV_LORA, QK_ROPE,
             N_BRANCH * D_MODEL)
D_IN = sum(IN_WIDTHS)

kernel_name = 'hybrid_nsa_dsa_hgrn2_mla_step'


def rms_norm(x, g):
    xf = x.astype(jnp.float32)
    xf = xf * lax.rsqrt(jnp.mean(xf * xf, axis=-1, keepdims=True) + EPS)
    return (xf * g.astype(jnp.float32)).astype(x.dtype)


def masked_softmax(logits, mask):
    logits = jnp.where(mask, logits.astype(jnp.float32), -jnp.inf)
    m = jnp.max(logits, axis=-1, keepdims=True)
    m = jnp.where(jnp.isfinite(m), m, 0.0)
    p = jnp.exp(logits - m)
    return p / jnp.maximum(jnp.sum(p, axis=-1, keepdims=True), 1e-30)


def rel_bucket(dist):
    n = jnp.maximum(dist, 0)
    exact = REL_BUCKETS // 2
    nf = jnp.maximum(n, 1).astype(jnp.float32)
    log_b = exact + (jnp.log(nf / exact) / math.log(REL_MAX_DIST / exact) * (REL_BUCKETS - exact)).astype(jnp.int32)
    return jnp.where(n < exact, n, jnp.minimum(log_b, REL_BUCKETS - 1))


def rel_bias(table, dist):
    return jnp.moveaxis(table[rel_bucket(dist)], -1, -3)


def rope_angles(pos):
    inv = ROPE_THETA ** (-jnp.arange(0, QK_ROPE, 2, dtype=jnp.float32) / QK_ROPE)
    return pos.astype(jnp.float32)[:, None] * inv


def apply_rope(x, ang):
    x1, x2 = jnp.split(x, 2, axis=-1)
    c, s = jnp.cos(ang).astype(x.dtype), jnp.sin(ang).astype(x.dtype)
    return jnp.concatenate([x1 * c - x2 * s, x1 * s + x2 * c], axis=-1)


def map_query_blocks(fn, q_pos, *qs):
    T = q_pos.shape[0]
    if T <= Q_BLOCK or T % Q_BLOCK:
        return fn(q_pos, *qs)
    nb = T // Q_BLOCK

    def split(a):
        return jnp.moveaxis(a.reshape((a.shape[0], nb, Q_BLOCK) + a.shape[2:]), 1, 0)

    out = lax.map(lambda args: fn(*args), (q_pos.reshape(nb, Q_BLOCK),) + tuple(split(a) for a in qs))
    out = jnp.moveaxis(out, 0, 1)
    return out.reshape((out.shape[0], T) + out.shape[3:])


def split_in(z):
    return jnp.split(z, [int(v) for v in np.cumsum(IN_WIDTHS)[:-1]], axis=-1)


def ada_params(c, w, b):
    mod = jax.nn.silu(c) @ w + b
    return jnp.split(mod[:, None, :], 6, axis=-1)


def modulate(x, g, shift, scale):
    return rms_norm(x, g) * (1.0 + scale) + shift


def nsa_core(q_pos, q, gates, cmp_kv, sel_gather, n_sel, win_kv, win_pos, table):
    B, Tq, H, Dh = q.shape
    scale = Dh ** -0.5
    t = q_pos[:, None]
    nc = cmp_kv.shape[1]
    c_end = jnp.arange(nc) * NSA_CMP_BLOCK + (NSA_CMP_BLOCK - 1)
    lg = jnp.einsum('bthd,bcd->bhtc', q, cmp_kv[:, :, 0]) * scale + rel_bias(table, t - c_end[None, :])
    p_cmp = masked_softmax(lg, c_end[None, :] <= t)
    o_cmp = jnp.einsum('bhtc,bcd->bthd', p_cmp.astype(q.dtype), cmp_kv[:, :, 1])
    per_sel = NSA_SEL_BLOCK // NSA_CMP_BLOCK
    imp = jnp.pad(jnp.sum(p_cmp, axis=1), ((0, 0), (0, 0), (0, n_sel * per_sel - nc)))
    imp = imp.reshape(B, Tq, n_sel, per_sel).sum(-1)
    blk = jnp.arange(n_sel)[None, :]
    cur = (q_pos // NSA_SEL_BLOCK)[:, None]
    imp = jnp.where((blk == cur) | (blk == 0), FORCE_SCORE, imp)
    imp = jnp.where(blk <= cur, imp, -jnp.inf)
    _, idx = lax.top_k(imp, min(NSA_TOP_N, n_sel))
    k_sel, v_sel = sel_gather(idx)
    s_pos = (idx[..., None] * NSA_SEL_BLOCK + jnp.arange(NSA_SEL_BLOCK)).reshape(B, Tq, -1)
    lg = jnp.einsum('bthd,btsd->bhts', q, k_sel.reshape(B, Tq, -1, Dh)) * scale + rel_bias(table, q_pos[None, :, None] - s_pos)
    p = masked_softmax(lg, (s_pos <= q_pos[None, :, None])[:, None])
    o_sel = jnp.einsum('bhts,btsd->bthd', p.astype(q.dtype), v_sel.reshape(B, Tq, -1, Dh))
    d = t - win_pos[None, :]
    lg = jnp.einsum('bthd,bsd->bhts', q, win_kv[:, :, 0]) * scale + rel_bias(table, d)
    p = masked_softmax(lg, (d >= 0) & (d <= NSA_WINDOW) & (win_pos[None, :] >= 0))
    o_win = jnp.einsum('bhts,bsd->bthd', p.astype(q.dtype), win_kv[:, :, 1])
    o = gates[..., 0:1] * o_cmp + gates[..., 1:2] * o_sel + gates[..., 2:3] * o_win
    return o.reshape(B, Tq, H * Dh)


def nsa_prompt(q, gates, kv, table):
    B, T = q.shape[:2]
    nc = T // NSA_CMP_BLOCK
    cmp_kv = kv[:, :nc * NSA_CMP_BLOCK, 0:2].reshape(B, nc, NSA_CMP_BLOCK, 2, HEAD_DIM).mean(axis=2)
    n_sel = -(-T // NSA_SEL_BLOCK)
    sel_blocks = jnp.pad(kv[:, :, 2:4], ((0, 0), (0, n_sel * NSA_SEL_BLOCK - T), (0, 0), (0, 0)))
    sel_blocks = sel_blocks.reshape(B, n_sel, NSA_SEL_BLOCK, 2, HEAD_DIM)
    win_pad = jnp.pad(kv[:, :, 4:6], ((0, 0), (NSA_WINDOW, 0), (0, 0), (0, 0)))
    bi = jnp.arange(B)[:, None, None]

    def sel_gather(idx):
        g = sel_blocks[bi, idx]
        return g[..., 0, :], g[..., 1, :]

    def block(q_pos, q_b, g_b):
        Tq = q_pos.shape[0]
        win = lax.dynamic_slice_in_dim(win_pad, q_pos[0], Tq + NSA_WINDOW, axis=1)
        win_pos = q_pos[0] - NSA_WINDOW + jnp.arange(Tq + NSA_WINDOW)
        return nsa_core(q_pos, q_b, g_b, cmp_kv, sel_gather, n_sel, win, win_pos, table)

    o = map_query_blocks(block, jnp.arange(T), q, gates)
    return o, kv[:, T - min(NSA_WINDOW, T):, 4:6]


def nsa_sample(q, gates, kv, cache, l, page_table, win_buf, table):
    B, T = q.shape[:2]
    P = page_table.shape[1] * PAGE_SIZE
    past_cmp = cache[l, page_table, :, 0:2].reshape(B, P // NSA_CMP_BLOCK, NSA_CMP_BLOCK, 2, HEAD_DIM).mean(axis=2)
    nc_new = T // NSA_CMP_BLOCK
    new_cmp = kv[:, :nc_new * NSA_CMP_BLOCK, 0:2].reshape(B, nc_new, NSA_CMP_BLOCK, 2, HEAD_DIM).mean(axis=2)
    cmp_kv = jnp.concatenate([past_cmp.astype(kv.dtype), new_cmp], axis=1)
    n_sel = -(-(P + T) // NSA_SEL_BLOCK)
    n_sel_past = P // NSA_SEL_BLOCK
    n_sel_new = n_sel - n_sel_past
    new_sel = jnp.pad(kv[:, :, 2:4], ((0, 0), (0, n_sel_new * NSA_SEL_BLOCK - T), (0, 0), (0, 0)))
    new_sel = new_sel.reshape(B, n_sel_new, NSA_SEL_BLOCK, 2, HEAD_DIM)
    bi = jnp.arange(B)[:, None, None]

    def sel_gather(idx):
        start = jnp.minimum(idx, n_sel_past - 1) * NSA_SEL_BLOCK
        page = page_table[bi, start // PAGE_SIZE]
        rows = (start % PAGE_SIZE)[..., None] + jnp.arange(NSA_SEL_BLOCK)
        from_pool = cache[l, page[..., None], rows, 2:4].astype(kv.dtype)
        from_new = new_sel[bi, jnp.clip(idx - n_sel_past, 0, n_sel_new - 1)]
        g = jnp.where((idx >= n_sel_past)[..., None, None, None], from_new, from_pool)
        return g[..., 0, :], g[..., 1, :]

    WB = win_buf.shape[1]
    win = jnp.concatenate([win_buf.astype(kv.dtype), kv[:, :, 4:6]], axis=1)
    win_pos = P - WB + jnp.arange(WB + T)
    o = map_query_blocks(lambda qp, qb, gb: nsa_core(qp, qb, gb, cmp_kv, sel_gather, n_sel, win, win_pos, table),
                         P + jnp.arange(T), q, gates)
    return o, win[:, WB + T - min(NSA_WINDOW, WB + T):]


def index_scores(qi, wi, ki):
    s = jax.nn.relu(jnp.einsum('bthd,bsd->bths', qi, ki).astype(jnp.float32))
    return jnp.einsum('bths,bth->bts', s, wi.astype(jnp.float32)) * (IDX_DIM ** -0.5 * IDX_HEADS ** -0.5)


def dsa_core(q_pos, q, iscore, topk, gather_kv, table):
    B, Tq, H, Dh = q.shape
    k_pos = jnp.arange(iscore.shape[-1])
    t = q_pos[None, :, None]
    iscore = jnp.where(k_pos[None, None, :] <= t, iscore, -jnp.inf)
    _, sel = lax.top_k(iscore, topk)
    kv = gather_kv(sel)
    lg = jnp.einsum('bthd,btsd->bhts', q, kv[..., 0, :]) * Dh ** -0.5 + rel_bias(table, t - sel)
    p = masked_softmax(lg, (sel <= t)[:, None])
    return jnp.einsum('bhts,btsd->bthd', p.astype(q.dtype), kv[..., 1, :]).reshape(B, Tq, H * Dh)


def dsa_prompt(q, qi, wi, ki, kv, table):
    B, T = q.shape[:2]
    topk = min(DSA_TOPK_MAX, T // 4)
    bi = jnp.arange(B)[:, None, None]

    def block(q_pos, q_b, qi_b, wi_b):
        return dsa_core(q_pos, q_b, index_scores(qi_b, wi_b, ki), topk, lambda sel: kv[bi, sel], table)

    return map_query_blocks(block, jnp.arange(T), q, qi, wi)


def dsa_sample(q, qi, wi, ki, kv, cache_kv, cache_idx, l, page_table, table):
    B, T = q.shape[:2]
    P = page_table.shape[1] * PAGE_SIZE
    ki_past = cache_idx[l, page_table].reshape(B, P, IDX_DIM).astype(ki.dtype)
    topk = min(DSA_TOPK_MAX, (P + T) // 4)
    bi = jnp.arange(B)[:, None, None]

    def gather(sel):
        sp = jnp.minimum(sel, P - 1)
        from_pool = cache_kv[l, page_table[bi, sp // PAGE_SIZE], sp % PAGE_SIZE].astype(kv.dtype)
        from_new = kv[bi, jnp.clip(sel - P, 0, T - 1)]
        return jnp.where((sel >= P)[..., None, None], from_new, from_pool)

    def block(q_pos, q_b, qi_b, wi_b):
        isc = jnp.concatenate([index_scores(qi_b, wi_b, ki_past), index_scores(qi_b, wi_b, ki)], axis=-1)
        return dsa_core(q_pos, q_b, isc, topk, gather, table)

    return map_query_blocks(block, P + jnp.arange(T), q, qi, wi)


def hgrn2(q, f_logit, i, lower, s0):
    B, T, H, Dk = q.shape
    f32 = jnp.float32
    lb = lower.reshape(H, Dk)
    f = lb + (1.0 - lb) * jax.nn.sigmoid(f_logit.astype(f32))
    log_f = jnp.log(jnp.maximum(f, 1e-20))
    k = 1.0 - f
    qf = jax.nn.silu(q.astype(f32))
    v = i.astype(f32)
    C = HGRN_CHUNK
    n_chunks = -(-T // C)
    pad = n_chunks * C - T

    def chunks(a):
        a = jnp.pad(a, ((0, 0), (0, pad), (0, 0), (0, 0)))
        return jnp.moveaxis(a.reshape(B, n_chunks, C, H, a.shape[-1]), 1, 0)

    causal = jnp.tril(jnp.ones((C, C), dtype=bool))[None, :, :, None, None]

    def step(S, xs):
        qc, lfc, kc, vc = xs
        b = jnp.cumsum(lfc, axis=1)
        dec = jnp.exp(jnp.where(causal, b[:, :, None] - b[:, None, :], -jnp.inf))
        A = jnp.einsum('bthd,btshd,bshd->bhts', qc, dec, kc)
        o = jnp.einsum('bhts,bshv->bthv', A, vc) + jnp.einsum('bthd,bhdv->bthv', qc * jnp.exp(b), S)
        b_last = b[:, -1]
        S = jnp.exp(b_last)[..., None] * S + jnp.einsum('bshd,bshv->bhdv', kc * jnp.exp(b_last[:, None] - b), vc)
        return S, o

    S, o = lax.scan(step, s0.astype(f32), (chunks(qf), chunks(log_f), chunks(k), chunks(v)))
    o = jnp.moveaxis(o, 0, 1).reshape(B, n_chunks * C, H, -1)[:, :T]
    return o, S


def mla_project(cq, ckv_raw, kr_raw, pos, g_qn, g_kvn, w_uq, w_uk):
    B, T = cq.shape[:2]
    q = (rms_norm(cq, g_qn) @ w_uq).reshape(B, T, N_HEADS, MLA_HEAD_QK)
    ang = rope_angles(pos)
    q_rope = apply_rope(q[..., QK_NOPE:], ang[None, :, None, :])
    q_lat = jnp.einsum('bthn,chn->bthc', q[..., :QK_NOPE], w_uk)
    ckv = rms_norm(ckv_raw, g_kvn)
    kr = apply_rope(kr_raw, ang[None])
    return q_lat, q_rope, ckv, kr


def mla_logits(q_lat, q_rope, ckv, kr):
    lg = jnp.einsum('bthc,bsc->bhts', q_lat, ckv) + jnp.einsum('bthr,bsr->bhts', q_rope, kr)
    return lg.astype(jnp.float32) * MLA_HEAD_QK ** -0.5


def mla_prompt(q_lat, q_rope, ckv, kr):
    T = q_lat.shape[1]
    k_pos = jnp.arange(T)

    def block(q_pos, ql, qr):
        p = masked_softmax(mla_logits(ql, qr, ckv, kr), k_pos[None, :] <= q_pos[:, None])
        return jnp.einsum('bhts,bsc->bthc', p.astype(ckv.dtype), ckv)

    return map_query_blocks(block, k_pos, q_lat, q_rope)


def mla_sample(q_lat, q_rope, ckv, kr, cache, l, page_table):
    B, T = q_lat.shape[:2]
    P = page_table.shape[1] * PAGE_SIZE
    ckv_p = cache[l, page_table, :, :KV_LORA].reshape(B, P, KV_LORA).astype(ckv.dtype)
    kr_p = cache[l, page_table, :, KV_LORA:].reshape(B, P, QK_ROPE).astype(kr.dtype)
    k_pos = jnp.arange(P + T)

    def block(q_pos, ql, qr):
        lg = jnp.concatenate([mla_logits(ql, qr, ckv_p, kr_p), mla_logits(ql, qr, ckv, kr)], axis=-1)
        p = masked_softmax(lg, k_pos[None, :] <= q_pos[:, None]).astype(ckv.dtype)
        return jnp.einsum('bhts,bsc->bthc', p[..., :P], ckv_p) + jnp.einsum('bhts,bsc->bthc', p[..., P:], ckv)

    return map_query_blocks(block, P + jnp.arange(T), q_lat, q_rope)


def conv_ffn(h, prev, w_up, conv_w, conv_b, w_down):
    T = h.shape[1]
    u = h @ w_up
    ext = jnp.concatenate([prev.astype(u.dtype), u], axis=1)
    conv = conv_b + sum(ext[:, k:k + T] * conv_w[k] for k in range(CONV_W))
    gate, val = jnp.split(conv, 2, axis=-1)
    y = (jax.nn.gelu(gate, approximate=True) * val) @ w_down
    return y, ext[:, T:]


def trunk_layer(x, c, lw, rel_table, lower_l, past):
    B, T, _ = x.shape
    shift1, scale1, gate1, shift2, scale2, gate2 = ada_params(c, lw['w_ada'], lw['b_ada'])
    h = modulate(x, lw['g_pre_mix'], shift1, scale1)
    (nsa_q, nsa_kv, nsa_g, dsa_q, dsa_kv, idx_q, idx_k, idx_w,
     hg, mla_cq, mla_ckv, mla_kr, merge_g) = split_in(h @ lw['w_in'])
    nsa_q = nsa_q.reshape(B, T, N_HEADS, HEAD_DIM)
    nsa_kv = nsa_kv.reshape(B, T, 6, HEAD_DIM)
    nsa_g = jax.nn.sigmoid(nsa_g.reshape(B, T, N_HEADS, 3))
    dsa_q = dsa_q.reshape(B, T, N_HEADS, HEAD_DIM)
    dsa_kv = dsa_kv.reshape(B, T, 2, HEAD_DIM)
    idx_q = idx_q.reshape(B, T, IDX_HEADS, IDX_DIM)
    hg_q, hg_f, hg_i, hg_g = [a.reshape(B, T, N_HEADS, HEAD_DIM) for a in jnp.split(hg, 4, axis=-1)]
    P = 0 if past is None else past['page_table'].shape[1] * PAGE_SIZE
    pos = P + jnp.arange(T)
    q_lat, q_rope, ckv, kr = mla_project(mla_cq, mla_ckv, mla_kr, pos, lw['mla_q_norm'], lw['mla_kv_norm'],
                                         lw['w_uq'], lw['w_uk'])
    nsa_tab, dsa_tab = rel_table[:, :N_HEADS], rel_table[:, N_HEADS:]
    if past is None:
        o_nsa, new_win = nsa_prompt(nsa_q, nsa_g, nsa_kv, nsa_tab)
        o_dsa = dsa_prompt(dsa_q, idx_q, idx_w, idx_k, dsa_kv, dsa_tab)
        o_lat = mla_prompt(q_lat, q_rope, ckv, kr)
        s0 = jnp.zeros((B, N_HEADS, HEAD_DIM, HEAD_DIM), jnp.float32)
        conv_prev = jnp.zeros((B, CONV_W - 1, 2 * D_FF), x.dtype)
    else:
        l, pt = past['layer'], past['page_table']
        o_nsa, new_win = nsa_sample(nsa_q, nsa_g, nsa_kv, past['cache_nsa_kv'], l, pt, past['win'], nsa_tab)
        o_dsa = dsa_sample(dsa_q, idx_q, idx_w, idx_k, dsa_kv, past['cache_dsa_kv'], past['cache_dsa_idx'], l, pt, dsa_tab)
        o_lat = mla_sample(q_lat, q_rope, ckv, kr, past['cache_mla'], l, pt)
        s0 = past['hgrn']
        conv_prev = past['conv']
    o_hg, s_new = hgrn2(hg_q, hg_f, hg_i, lower_l, s0)
    o_hg = (rms_norm(o_hg.astype(x.dtype), lw['hg_norm']) * jax.nn.silu(hg_g)).reshape(B, T, BRANCH_W)
    o_mla = jnp.einsum('bthc,chv->bthv', o_lat, lw['w_uv']).reshape(B, T, BRANCH_W)
    branches = jnp.stack([o_nsa, o_dsa, o_hg, o_mla], axis=2)
    proj = jnp.einsum('btnw,nwd->btnd', branches, lw['w_branch'])
    gates = jax.nn.sigmoid(merge_g.reshape(B, T, N_BRANCH, D_MODEL))
    mixed = jnp.sum(gates * proj, axis=2) @ lw['w_out']
    x = x + gate1 * rms_norm(mixed, lw['g_post_mix'])
    h2 = modulate(x, lw['g_pre_ffn'], shift2, scale2)
    y, conv_state = conv_ffn(h2, conv_prev, lw['w_up'], lw['conv_w'], lw['conv_b'], lw['w_down'])
    x = x + gate2 * rms_norm(y, lw['g_post_ffn'])
    new_state = (nsa_kv[:, :, 0:4], dsa_kv, idx_k, jnp.concatenate([ckv, kr], axis=-1), new_win, s_new, conv_state)
    return x, new_state


def setup_inputs(seed: int = 0) -> dict:
    key = jax.random.key(seed)
    ks = jax.random.split(key, 40)
    f32 = jnp.float32

    def nrm(k, shape, scale=1.0):
        return jax.random.normal(k, shape, f32) * scale

    def gain(k, shape):
        return 1.0 + 0.02 * jax.random.normal(k, shape, f32)

    D = D_MODEL
    n_pages = PAST_LEN // PAGE_SIZE
    n_used = DEC_BATCH * n_pages
    n_pool = n_used + n_used // 4
    page_table = jax.random.permutation(ks[0], n_pool)[:n_used].reshape(DEC_BATCH, n_pages).astype(jnp.int32)
    win_len = min(NSA_WINDOW, PAST_LEN)
    return {
        'x_prompt': nrm(ks[1], (BATCH, SEQ, D)),
        'x_sample': nrm(ks[2], (DEC_BATCH, DEC_SEQ, D)),
        'cache_nsa_kv': nrm(ks[3], (DEPTH, n_pool, PAGE_SIZE, 4, HEAD_DIM)),
        'cache_dsa_kv': nrm(ks[4], (DEPTH, n_pool, PAGE_SIZE, 2, HEAD_DIM)),
        'cache_dsa_idx': nrm(ks[5], (DEPTH, n_pool, PAGE_SIZE, IDX_DIM)),
        'cache_mla': nrm(ks[6], (DEPTH, n_pool, PAGE_SIZE, KV_LORA + QK_ROPE)),
        'state_nsa_win': nrm(ks[7], (DEPTH, DEC_BATCH, win_len, 2, HEAD_DIM)),
        'state_hgrn': nrm(ks[8], (DEPTH, DEC_BATCH, N_HEADS, HEAD_DIM, HEAD_DIM), 0.3),
        'state_ffn_conv': nrm(ks[9], (DEPTH, DEC_BATCH, CONV_W - 1, 2 * D_FF)),
        'page_table': page_table,
        'c_prompt': nrm(ks[10], (BATCH, D)),
        'c_sample': nrm(ks[11], (DEC_BATCH, D)),
        'rel_table': nrm(ks[12], (REL_BUCKETS, N_BIAS_HEADS), 0.5),
        'w_ada': nrm(ks[13], (DEPTH, D, 6 * D), 0.5 * D ** -0.5),
        'b_ada': nrm(ks[14], (DEPTH, 6 * D), 0.02),
        'g_pre_mix': gain(ks[15], (DEPTH, D)),
        'g_post_mix': gain(ks[16], (DEPTH, D)),
        'g_pre_ffn': gain(ks[17], (DEPTH, D)),
        'g_post_ffn': gain(ks[18], (DEPTH, D)),
        'w_in': nrm(ks[19], (DEPTH, D, D_IN), D ** -0.5),
        'hg_lb': nrm(ks[20], (DEPTH, BRANCH_W), 0.5),
        'hg_norm': gain(ks[21], (DEPTH, HEAD_DIM)),
        'mla_q_norm': gain(ks[22], (DEPTH, Q_LORA)),
        'mla_kv_norm': gain(ks[23], (DEPTH, KV_LORA)),
        'w_uq': nrm(ks[24], (DEPTH, Q_LORA, N_HEADS * MLA_HEAD_QK), Q_LORA ** -0.5),
        'w_uk': nrm(ks[25], (DEPTH, KV_LORA, N_HEADS, QK_NOPE), KV_LORA ** -0.5),
        'w_uv': nrm(ks[26], (DEPTH, KV_LORA, N_HEADS, V_DIM), KV_LORA ** -0.5),
        'w_branch': nrm(ks[27], (DEPTH, N_BRANCH, BRANCH_W, D), BRANCH_W ** -0.5),
        'w_out': nrm(ks[28], (DEPTH, D, D), D ** -0.5),
        'w_up': nrm(ks[29], (DEPTH, D, 2 * D_FF), D ** -0.5),
        'conv_w': nrm(ks[30], (DEPTH, CONV_W, 2 * D_FF), CONV_W ** -0.5),
        'conv_b': nrm(ks[31], (DEPTH, 2 * D_FF), 0.02),
        'w_down': nrm(ks[32], (DEPTH, D_FF, D), D_FF ** -0.5),
    }


def reference(x_prompt, x_sample, cache_nsa_kv, cache_dsa_kv, cache_dsa_idx, cache_mla,
              state_nsa_win, state_hgrn, state_ffn_conv, page_table, c_prompt, c_sample,
              rel_table, w_ada, b_ada, g_pre_mix, g_post_mix, g_pre_ffn, g_post_ffn, w_in,
              hg_lb, hg_norm, mla_q_norm, mla_kv_norm, w_uq, w_uk, w_uv, w_branch, w_out,
              w_up, conv_w, conv_b, w_down):
    gam = jax.nn.softmax(hg_lb.astype(jnp.float32), axis=0)
    cum = jnp.cumsum(gam, axis=0)
    lower = cum - cum[0]
    xp, xs = x_prompt, x_sample
    states_p, states_s = [], []
    for l in range(DEPTH):
        lw = {'w_ada': w_ada[l], 'b_ada': b_ada[l], 'g_pre_mix': g_pre_mix[l], 'g_post_mix': g_post_mix[l],
              'g_pre_ffn': g_pre_ffn[l], 'g_post_ffn': g_post_ffn[l], 'w_in': w_in[l], 'hg_norm': hg_norm[l],
              'mla_q_norm': mla_q_norm[l], 'mla_kv_norm': mla_kv_norm[l], 'w_uq': w_uq[l], 'w_uk': w_uk[l],
              'w_uv': w_uv[l], 'w_branch': w_branch[l], 'w_out': w_out[l], 'w_up': w_up[l],
              'conv_w': conv_w[l], 'conv_b': conv_b[l], 'w_down': w_down[l]}
        xp, st_p = trunk_layer(xp, c_prompt, lw, rel_table, lower[l], None)
        past = {'layer': l, 'page_table': page_table, 'cache_nsa_kv': cache_nsa_kv, 'cache_dsa_kv': cache_dsa_kv,
                'cache_dsa_idx': cache_dsa_idx, 'cache_mla': cache_mla, 'win': state_nsa_win[l],
                'hgrn': state_hgrn[l], 'conv': state_ffn_conv[l]}
        xs, st_s = trunk_layer(xs, c_sample, lw, rel_table, lower[l], past)
        states_p.append(st_p)
        states_s.append(st_s)
    sp = [jnp.stack(z) for z in zip(*states_p)]
    ss = [jnp.stack(z) for z in zip(*states_s)]
    return (xp, xs, sp[0], ss[0], sp[1], ss[1], sp[2], ss[2], sp[3], ss[3], sp[4], ss[4], sp[5], ss[5], sp[6], ss[6])
```

```python
import functools
import math

import jax
import jax.numpy as jnp
from jax import lax
from jax.experimental import pallas as pl
from jax.experimental.pallas import tpu as pltpu

F32, BF16, I32 = jnp.float32, jnp.bfloat16, jnp.int32

N_HEADS = 4
HEAD_DIM = 64
BRANCH_W = N_HEADS * HEAD_DIM
N_BRANCH = 4
NSA_CMP_BLOCK = 32
NSA_SEL_BLOCK = 64
NSA_TOP_N = 16
NSA_WINDOW = 512
FORCE_SCORE = 1e9
IDX_HEADS = 4
IDX_DIM = 64
DSA_TOPK_MAX = 256
HGRN_CHUNK = 16
Q_LORA = 256
KV_LORA = 128
QK_NOPE = 64
QK_ROPE = 32
V_DIM = 64
MLA_HEAD_QK = QK_NOPE + QK_ROPE
ROPE_THETA = 10000.0
REL_BUCKETS = 32
REL_MAX_DIST = 512
CONV_W = 3
EPS = 1e-6
PAGE_SIZE = 128

LANES = 128
VMEM_LIMIT = 56 * 1024 * 1024

OUT_WIDTHS = (256, 256, 128, 256, 128, 256, 1024, 256, 128, 128, 4096)
OUT_NAMES = ('nsa_q', 'nsa_kv4', 'nsa_win', 'dsa_q', 'dsa_kv', 'idx_q', 'hg', 'mla_cq', 'mla_ckv', 'small', 'merge_g')
SM_IDXK, SM_KR, SM_NSAG, SM_IDXW = 0, 64, 96, 108
MLA_QW = 256
INT_MIN = -2 ** 31


def _cp(sem, vmem=VMEM_LIMIT):
    return pltpu.CompilerParams(dimension_semantics=sem, vmem_limit_bytes=vmem)


def _dot(a, b):
    return jnp.dot(a.astype(BF16), b.astype(BF16), preferred_element_type=F32)


def _dot_nt(a, b):
    return lax.dot_general(a.astype(BF16), b.astype(BF16), (((1,), (1,)), ((), ())), preferred_element_type=F32)


def _dot_tn(a, b):
    return lax.dot_general(a.astype(BF16), b.astype(BF16), (((0,), (0,)), ((), ())), preferred_element_type=F32)


def _dot_hilo(a, g):
    hi = a.astype(BF16)
    lo = (a - hi.astype(F32)).astype(BF16)
    return jnp.dot(hi, g, preferred_element_type=F32) + jnp.dot(lo, g, preferred_element_type=F32)


def _rms(x, g):
    return x * lax.rsqrt(jnp.mean(x * x, axis=-1, keepdims=True) + EPS) * g


def _sigmoid(x):
    return 1.0 / (1.0 + jnp.exp(-x))


def _silu(x):
    return x * _sigmoid(x)


def _iota(shape, dim):
    return lax.broadcasted_iota(I32, shape, dim)


def _head_block_ones(n):
    return (_iota((n, n), 0) // HEAD_DIM == _iota((n, n), 1) // HEAD_DIM)


def _rel_bias(d, tab_ref, heads):
    exact = REL_BUCKETS // 2
    n = jnp.maximum(d, 0)
    nf = jnp.maximum(n, 1).astype(F32)
    log_b = exact + (jnp.log(nf / exact) / math.log(REL_MAX_DIST / exact) * (REL_BUCKETS - exact)).astype(I32)
    bucket = jnp.where(n < exact, n, jnp.minimum(log_b, REL_BUCKETS - 1))
    outs = []
    for h in heads:
        o = jnp.full(d.shape, tab_ref[0, h], F32)
        for k in range(1, REL_BUCKETS):
            o = jnp.where(bucket == k, tab_ref[k, h], o)
        outs.append(o)
    return outs


def _ordered_key(x):
    x = jnp.where(x == 0.0, 0.0, x)
    bits = pltpu.bitcast(x, I32)
    return jnp.where(bits < 0, bits ^ 0x7FFFFFFF, bits)


def _kth_largest_key(count_ge, k, shape):
    t = jnp.full(shape, INT_MIN, I32)
    zero = jnp.zeros(shape, I32)
    t = jnp.where(count_ge(zero) >= k, zero, t)
    for bit in range(30, -1, -1):
        cand = t + (1 << bit)
        t = jnp.where(count_ge(cand) >= k, cand, t)
    return t


def _softmax_step(lg, mask, m_ref, l_ref):
    lg = jnp.where(mask, lg, -jnp.inf)
    m_old = m_ref[...]
    m_new = jnp.maximum(m_old, jnp.max(lg, axis=-1, keepdims=True))
    m_safe = jnp.where(m_new == -jnp.inf, 0.0, m_new)
    p = jnp.exp(lg - m_safe)
    alpha = jnp.exp(m_old - m_safe)
    l_ref[...] = alpha * l_ref[...] + jnp.sum(p, axis=-1, keepdims=True)
    m_ref[...] = m_new
    return p, alpha


def _ada_kernel(c_ref, w_ref, b_ref, o_ref):
    o_ref[0] = _dot(_silu(c_ref[...]), w_ref[0]) + b_ref[0]


def _ada(c, w_ada, b_ada):
    depth, d, n = w_ada.shape
    bc = c.shape[0]
    tn = 1536 if n % 1536 == 0 else n
    return pl.pallas_call(
        _ada_kernel,
        out_shape=jax.ShapeDtypeStruct((depth, bc, n), F32),
        grid=(depth, n // tn),
        in_specs=[pl.BlockSpec((bc, d), lambda l, j: (0, 0)),
                  pl.BlockSpec((1, d, tn), lambda l, j: (l, 0, j)),
                  pl.BlockSpec((1, 1, tn), lambda l, j: (l, 0, j))],
        out_specs=pl.BlockSpec((1, bc, tn), lambda l, j: (l, 0, j)),
        compiler_params=_cp(("arbitrary", "arbitrary")),
        name="ada",
    )(c, w_ada, b_ada.reshape(depth, 1, n))


def _bias_kernel(tab_ref, cend_ref, tile_ref, cmp_ref, *, tq, tk, seq):
    i = _iota((tq, tk), 0)
    j = _iota((tq, tk), 1)
    for k in range(4):
        bs = _rel_bias(k * tk + i - j, tab_ref, range(2 * N_HEADS))
        for h in range(2 * N_HEADS):
            tile_ref[k, h] = bs[h]
    t = _iota((seq, cend_ref.shape[1]), 0)
    bs = _rel_bias(t - cend_ref[...], tab_ref, range(N_HEADS))
    for h in range(N_HEADS):
        cmp_ref[h] = bs[h]


def _bias_tiles(rel_table, cend, tq, tk, seq):
    nc = cend.shape[1]
    return pl.pallas_call(
        functools.partial(_bias_kernel, tq=tq, tk=tk, seq=seq),
        out_shape=(jax.ShapeDtypeStruct((4, 2 * N_HEADS, tq, tk), F32),
                   jax.ShapeDtypeStruct((N_HEADS, seq, nc), F32)),
        in_specs=[pl.BlockSpec(memory_space=pltpu.SMEM),
                  pl.BlockSpec(memory_space=pltpu.VMEM)],
        out_specs=(pl.BlockSpec(memory_space=pltpu.VMEM), pl.BlockSpec(memory_space=pltpu.VMEM)),
        compiler_params=pltpu.CompilerParams(vmem_limit_bytes=VMEM_LIMIT),
        name="rel_bias_tiles",
    )(rel_table, cend)


def _inproj_kernel(x_ref, g_ref, sh_ref, sc_ref, w_ref, *o_refs):
    h = _rms(x_ref[...], g_ref[...]) * (1.0 + sc_ref[0]) + sh_ref[0]
    hb = h.astype(BF16)
    off = 0
    for o_ref, w in zip(o_refs, OUT_WIDTHS):
        o_ref[...] = jnp.dot(hb, w_ref[:, off:off + w], preferred_element_type=F32)
        off += w


def _mod_spec(arr, tm, tps):
    d = arr.shape[-1]
    if arr.shape[1] == 1:
        return pl.BlockSpec((1, 1, d), lambda i, *_: (i // tps, 0, 0))
    return pl.BlockSpec((1, tm, d), lambda i, *_: (0, i, 0))


def _inproj(x, g, shift, scale, w_p, tm, tps):
    m, d = x.shape
    n = w_p.shape[1]
    return pl.pallas_call(
        _inproj_kernel,
        out_shape=tuple(jax.ShapeDtypeStruct((m, w), F32) for w in OUT_WIDTHS),
        grid=(m // tm,),
        in_specs=[pl.BlockSpec((tm, d), lambda i: (i, 0)),
                  pl.BlockSpec((1, d), lambda i: (0, 0)),
                  _mod_spec(shift, tm, tps), _mod_spec(scale, tm, tps),
                  pl.BlockSpec((d, n), lambda i: (0, 0), pipeline_mode=pl.Buffered(1))],
        out_specs=tuple(pl.BlockSpec((tm, w), lambda i: (i, 0)) for w in OUT_WIDTHS),
        compiler_params=_cp(("arbitrary",)),
        name="inproj",
    )(x, g, shift, scale, w_p)


def _mla_proj_kernel(cq_ref, ckv_ref, sm_ref, inv_ref, gq_ref, gkv_ref, wn_ref, wr1_ref, wr2_ref, wuk_ref,
                     qc_ref, kc_ref, st_ref, *, tm, seq, pos0):
    i = pl.program_id(0)
    qn = _rms(cq_ref[...], gq_ref[...]).astype(BF16)
    q_nope = jnp.dot(qn, wn_ref[...], preferred_element_type=F32)
    r1 = jnp.dot(qn, wr1_ref[...], preferred_element_type=F32)
    r2 = jnp.dot(qn, wr2_ref[...], preferred_element_type=F32)
    pos = ((i * tm + _iota((tm, 1), 0)) % seq + pos0).astype(F32)
    ang = pos * inv_ref[...]
    c, s = jnp.cos(ang), jnp.sin(ang)
    rot1 = r1 * c - r2 * s
    rot2 = r1 * s + r2 * c
    hr = QK_ROPE // 2
    zpad = jnp.zeros((tm, MLA_QW - KV_LORA - QK_ROPE), F32)
    for h in range(N_HEADS):
        q_lat = _dot(q_nope[:, h * QK_NOPE:(h + 1) * QK_NOPE], wuk_ref[h])
        qh = jnp.concatenate([q_lat, rot1[:, h * hr:(h + 1) * hr], rot2[:, h * hr:(h + 1) * hr], zpad], axis=-1)
        qc_ref[:, h * MLA_QW:(h + 1) * MLA_QW] = qh.astype(BF16)
    ckv = _rms(ckv_ref[...], gkv_ref[...])
    sm = sm_ref[...]
    x1 = sm[:, SM_KR:SM_KR + hr]
    x2 = sm[:, SM_KR + hr:SM_KR + 2 * hr]
    ang1 = pos * inv_ref[:, 0:hr]
    c1, s1 = jnp.cos(ang1), jnp.sin(ang1)
    kr = jnp.concatenate([x1 * c1 - x2 * s1, x1 * s1 + x2 * c1], axis=-1)
    st = jnp.concatenate([ckv, kr], axis=-1)
    st_ref[...] = st
    kc_ref[...] = jnp.concatenate([st, zpad], axis=-1).astype(BF16)


def _mla_proj(cq, ckv, small, inv4, gq, gkv, wn, wr1, wr2, wukT, tm, seq, pos0):
    m = cq.shape[0]
    row = lambda w: pl.BlockSpec((tm, w), lambda i: (i, 0))
    full = lambda a: pl.BlockSpec(a.shape, lambda i: (0,) * a.ndim)
    return pl.pallas_call(
        functools.partial(_mla_proj_kernel, tm=tm, seq=seq, pos0=pos0),
        out_shape=(jax.ShapeDtypeStruct((m, N_HEADS * MLA_QW), BF16),
                   jax.ShapeDtypeStruct((m, MLA_QW), BF16),
                   jax.ShapeDtypeStruct((m, KV_LORA + QK_ROPE), F32)),
        grid=(m // tm,),
        in_specs=[row(Q_LORA), row(KV_LORA), row(LANES), full(inv4), full(gq), full(gkv),
                  full(wn), full(wr1), full(wr2), full(wukT)],
        out_specs=(row(N_HEADS * MLA_QW), row(MLA_QW), row(KV_LORA + QK_ROPE)),
        compiler_params=_cp(("arbitrary",)),
        name="mla_proj",
    )(cq, ckv, small, inv4, gq, gkv, wn, wr1, wr2, wukT)


def _stack_heads(q, dst_ref, tq, w):
    for h in range(N_HEADS):
        dst_ref[h * tq:(h + 1) * tq, :] = q[:, h * w:(h + 1) * w].astype(BF16)


def _flash_chunk(lg, mask, v, m_ref, l_ref, acc_ref, tq, tk):
    p, alpha = _softmax_step(lg.reshape(N_HEADS, tq, tk), mask[None], m_ref, l_ref)
    acc_ref[...] = alpha.reshape(N_HEADS * tq, 1) * acc_ref[...] + _dot(p.reshape(N_HEADS * tq, tk), v)


def _flash_init(m_ref, l_ref, acc_ref):
    m_ref[...] = jnp.full(m_ref.shape, -jnp.inf, F32)
    l_ref[...] = jnp.zeros(l_ref.shape, F32)
    acc_ref[...] = jnp.zeros(acc_ref.shape, F32)


def _flash_out(l_ref, acc_ref, tq):
    return acc_ref[...] / jnp.maximum(l_ref[...], 1e-30).reshape(N_HEADS * tq, 1)


def _nsa_prompt_kernel(q_ref, sm_ref, kv_ref, win_ref, bias_ref, bcmp_ref, o_ref,
                       cmp_s, q4_s, m_s, l_s, acc_s, *, seq, tq, n_top):
    tk = tq
    qb = pl.program_id(1)
    t0 = qb * tq
    nsel = seq // NSA_SEL_BLOCK
    nc = 2 * nsel
    scale = HEAD_DIM ** -0.5

    @pl.when(qb == 0)
    def _():
        for j in range(nsel):
            r0 = j * NSA_SEL_BLOCK
            cmp_s[j:j + 1, :] = jnp.sum(kv_ref[r0:r0 + NSA_CMP_BLOCK, 0:2 * HEAD_DIM], axis=0, keepdims=True) * (1.0 / NSA_CMP_BLOCK)
            cmp_s[nsel + j:nsel + j + 1, :] = jnp.sum(kv_ref[r0 + NSA_CMP_BLOCK:r0 + 2 * NSA_CMP_BLOCK, 0:2 * HEAD_DIM],
                                                      axis=0, keepdims=True) * (1.0 / NSA_CMP_BLOCK)

    _stack_heads(q_ref[...], q4_s, tq, HEAD_DIM)
    q4 = q4_s[...]
    t = t0 + _iota((tq, 1), 0)

    jj = _iota((1, nc), 1)
    c_end = jnp.where(jj < nsel, NSA_SEL_BLOCK * jj + (NSA_CMP_BLOCK - 1), NSA_SEL_BLOCK * (jj - nsel) + (NSA_SEL_BLOCK - 1))
    lg = (_dot_nt(q4, cmp_s[:, 0:HEAD_DIM]) * scale).reshape(N_HEADS, tq, nc) + bcmp_ref[...]
    lg = jnp.where((c_end <= t)[None], lg, -jnp.inf)
    mx = jnp.max(lg, axis=-1, keepdims=True)
    mx = jnp.where(mx == -jnp.inf, 0.0, mx)
    p = jnp.exp(lg - mx)
    p = p / jnp.maximum(jnp.sum(p, axis=-1, keepdims=True), 1e-30)
    o_cmp = _dot(p.reshape(N_HEADS * tq, nc), cmp_s[:, HEAD_DIM:2 * HEAD_DIM])

    ps = p[0] + p[1] + p[2] + p[3]
    imp = ps[:, 0:nsel] + ps[:, nsel:nc]
    blk = _iota((1, nsel), 1)
    cur = t // NSA_SEL_BLOCK
    imp = jnp.where((blk == cur) | (blk == 0), FORCE_SCORE, imp)
    imp = jnp.where(blk <= cur, imp, -jnp.inf)
    selm = jnp.zeros((tq, nsel), F32)
    for _ in range(n_top):
        top = jnp.max(imp, axis=-1, keepdims=True)
        first = jnp.min(jnp.where(imp == top, blk, nsel), axis=-1, keepdims=True)
        pick = blk == first
        selm = jnp.where(pick, 1.0, selm)
        imp = jnp.where(pick, -jnp.inf, imp)
    selm = selm.astype(BF16)

    _flash_init(m_s, l_s, acc_s)

    def sel_body(c, carry):
        s0 = pl.multiple_of(c * tk, tk)
        k = kv_ref[pl.ds(s0, tk), 2 * HEAD_DIM:3 * HEAD_DIM]
        v = kv_ref[pl.ds(s0, tk), 3 * HEAD_DIM:4 * HEAD_DIM]
        lg = (_dot_nt(q4, k) * scale).reshape(N_HEADS, tq, tk) + bias_ref[jnp.minimum(qb - c, 3)]
        s_pos = s0 + _iota((1, tk), 1)
        expand = (_iota((nsel, tk), 0) == (s0 + _iota((nsel, tk), 1)) // NSA_SEL_BLOCK).astype(BF16)
        chosen = jnp.dot(selm, expand, preferred_element_type=F32) > 0.5
        _flash_chunk(lg.reshape(N_HEADS * tq, tk), chosen & (s_pos <= t), v, m_s, l_s, acc_s, tq, tk)
        return carry

    lax.fori_loop(0, qb + 1, sel_body, 0)
    o_sel = _flash_out(l_s, acc_s, tq)

    _flash_init(m_s, l_s, acc_s)

    def win_body(c, carry):
        s0 = pl.multiple_of(c * tk, tk)
        k = win_ref[pl.ds(s0, tk), 0:HEAD_DIM]
        v = win_ref[pl.ds(s0, tk), HEAD_DIM:2 * HEAD_DIM]
        lg = (_dot_nt(q4, k) * scale).reshape(N_HEADS, tq, tk) + bias_ref[jnp.minimum(qb - c, 3)]
        d = t - (s0 + _iota((1, tk), 1))
        _flash_chunk(lg.reshape(N_HEADS * tq, tk), (d >= 0) & (d <= NSA_WINDOW), v, m_s, l_s, acc_s, tq, tk)
        return carry

    lax.fori_loop(jnp.maximum(qb - (NSA_WINDOW + tk - 1) // tk, 0), qb + 1, win_body, 0)
    o_win = _flash_out(l_s, acc_s, tq)

    g = _sigmoid(sm_ref[:, SM_NSAG:SM_NSAG + 3 * N_HEADS])
    for h in range(N_HEADS):
        rows = slice(h * tq, (h + 1) * tq)
        o_ref[:, h * HEAD_DIM:(h + 1) * HEAD_DIM] = (g[:, 3 * h:3 * h + 1] * o_cmp[rows]
                                                      + g[:, 3 * h + 1:3 * h + 2] * o_sel[rows]
                                                      + g[:, 3 * h + 2:3 * h + 3] * o_win[rows])


def _nsa_prompt(nsa_q, small, kv4, win, bias_t, bias_cmp, batch, seq, tq):
    nq = seq // tq
    nsel = seq // NSA_SEL_BLOCK
    rowq = lambda w: pl.BlockSpec((tq, w), lambda b, i: (b * nq + i, 0))
    rows = lambda w: pl.BlockSpec((seq, w), lambda b, i: (b, 0))
    return pl.pallas_call(
        functools.partial(_nsa_prompt_kernel, seq=seq, tq=tq, n_top=min(NSA_TOP_N, nsel)),
        out_shape=jax.ShapeDtypeStruct((batch * seq, BRANCH_W), F32),
        grid=(batch, nq),
        in_specs=[rowq(BRANCH_W), rowq(LANES), rows(4 * HEAD_DIM), rows(2 * HEAD_DIM),
                  pl.BlockSpec((4, N_HEADS, tq, tq), lambda b, i: (0, 0, 0, 0)),
                  pl.BlockSpec((N_HEADS, tq, 2 * nsel), lambda b, i: (0, i, 0))],
        out_specs=rowq(BRANCH_W),
        scratch_shapes=[pltpu.VMEM((2 * nsel, 2 * HEAD_DIM), F32),
                        pltpu.VMEM((N_HEADS * tq, HEAD_DIM), BF16),
                        pltpu.VMEM((N_HEADS, tq, 1), F32), pltpu.VMEM((N_HEADS, tq, 1), F32),
                        pltpu.VMEM((N_HEADS * tq, HEAD_DIM), F32)],
        compiler_params=_cp(("arbitrary", "arbitrary")),
        name="nsa_prompt",
    )(nsa_q, small, kv4, win, bias_t, bias_cmp)


def _dsa_prompt_kernel(q_ref, iq_ref, sm_ref, smf_ref, kv_ref, bias_ref, o_ref,
                       q4_s, qi4_s, key_s, m_s, l_s, acc_s, *, tq, topk):
    tk = tq
    qb = pl.program_id(1)
    t0 = qb * tq
    scale = HEAD_DIM ** -0.5
    _stack_heads(q_ref[...], q4_s, tq, HEAD_DIM)
    _stack_heads(iq_ref[...], qi4_s, tq, IDX_DIM)
    q4, qi4 = q4_s[...], qi4_s[...]
    t = t0 + _iota((tq, 1), 0)
    wi = sm_ref[:, SM_IDXW:SM_IDXW + IDX_HEADS]
    cst = IDX_DIM ** -0.5 * IDX_HEADS ** -0.5

    def score_body(c, carry):
        s0 = pl.multiple_of(c * tk, tk)
        ki = smf_ref[pl.ds(s0, tk), SM_IDXK:SM_IDXK + IDX_DIM]
        s = jnp.maximum(_dot_nt(qi4, ki), 0.0).reshape(IDX_HEADS, tq, tk)
        sc = (s[0] * wi[:, 0:1] + s[1] * wi[:, 1:2] + s[2] * wi[:, 2:3] + s[3] * wi[:, 3:4]) * cst
        sc = jnp.where(s0 + _iota((1, tk), 1) <= t, sc, -jnp.inf)
        key_s[c] = _ordered_key(sc)
        return carry

    lax.fori_loop(0, qb + 1, score_body, 0)

    def count(pred):
        def body(c, acc):
            hit = pred(key_s[c]).astype(I32)
            part = hit[:, 0:LANES]
            for u in range(1, tk // LANES):
                part = part + hit[:, u * LANES:(u + 1) * LANES]
            return acc + part
        acc = lax.fori_loop(0, qb + 1, body, jnp.zeros((tq, LANES), I32))
        return jnp.sum(acc, axis=-1, keepdims=True)

    thr = _kth_largest_key(lambda cand: count(lambda key: key >= cand), topk, (tq, 1))
    need = (topk - count(lambda key: key > thr)).astype(F32)

    strict_upper = (_iota((tk, tk), 0) < _iota((tk, tk), 1)).astype(BF16)
    _flash_init(m_s, l_s, acc_s)

    def att_body(c, run):
        s0 = pl.multiple_of(c * tk, tk)
        key = key_s[c]
        eq = key == thr
        eqb = jnp.where(eq, 1.0, 0.0).astype(BF16)
        before = jnp.dot(eqb, strict_upper, preferred_element_type=F32) + run
        chosen = (key > thr) | (eq & (before < need))
        k = kv_ref[pl.ds(s0, tk), 0:HEAD_DIM]
        v = kv_ref[pl.ds(s0, tk), HEAD_DIM:2 * HEAD_DIM]
        lg = (_dot_nt(q4, k) * scale).reshape(N_HEADS, tq, tk) + bias_ref[jnp.minimum(qb - c, 3)]
        mask = chosen & (s0 + _iota((1, tk), 1) <= t)
        _flash_chunk(lg.reshape(N_HEADS * tq, tk), mask, v, m_s, l_s, acc_s, tq, tk)
        return run + jnp.sum(eqb.astype(F32), axis=-1, keepdims=True)

    lax.fori_loop(0, qb + 1, att_body, jnp.zeros((tq, 1), F32))
    o = _flash_out(l_s, acc_s, tq)
    for h in range(N_HEADS):
        o_ref[:, h * HEAD_DIM:(h + 1) * HEAD_DIM] = o[h * tq:(h + 1) * tq]


def _dsa_prompt(dsa_q, idx_q, small, dsa_kv, bias_t, batch, seq, tq):
    nq = seq // tq
    rowq = lambda w: pl.BlockSpec((tq, w), lambda b, i: (b * nq + i, 0))
    rows = lambda w: pl.BlockSpec((seq, w), lambda b, i: (b, 0))
    return pl.pallas_call(
        functools.partial(_dsa_prompt_kernel, tq=tq, topk=min(DSA_TOPK_MAX, seq // 4)),
        out_shape=jax.ShapeDtypeStruct((batch * seq, BRANCH_W), F32),
        grid=(batch, nq),
        in_specs=[rowq(BRANCH_W), rowq(IDX_HEADS * IDX_DIM), rowq(LANES), rows(LANES), rows(2 * HEAD_DIM),
                  pl.BlockSpec((4, N_HEADS, tq, tq), lambda b, i: (0, 1, 0, 0))],
        out_specs=rowq(BRANCH_W),
        scratch_shapes=[pltpu.VMEM((N_HEADS * tq, HEAD_DIM), BF16),
                        pltpu.VMEM((IDX_HEADS * tq, IDX_DIM), BF16),
                        pltpu.VMEM((nq, tq, tq), I32),
                        pltpu.VMEM((N_HEADS, tq, 1), F32), pltpu.VMEM((N_HEADS, tq, 1), F32),
                        pltpu.VMEM((N_HEADS * tq, HEAD_DIM), F32)],
        compiler_params=_cp(("arbitrary", "arbitrary")),
        name="dsa_prompt",
    )(dsa_q, idx_q, small, small, dsa_kv, bias_t)


def _mla_prompt_kernel(qc_ref, kc_ref, wuv_ref, o_ref, q4_s, m_s, l_s, acc_s, *, tq):
    tk = tq
    qb = pl.program_id(1)
    t = qb * tq + _iota((tq, 1), 0)
    scale = MLA_HEAD_QK ** -0.5
    _stack_heads(qc_ref[...], q4_s, tq, MLA_QW)
    q4 = q4_s[...]
    _flash_init(m_s, l_s, acc_s)

    def body(c, carry):
        s0 = pl.multiple_of(c * tk, tk)
        kc = kc_ref[pl.ds(s0, tk), :]
        lg = _dot_nt(q4, kc) * scale
        _flash_chunk(lg, s0 + _iota((1, tk), 1) <= t, kc[:, 0:KV_LORA], m_s, l_s, acc_s, tq, tk)
        return carry

    lax.fori_loop(0, qb + 1, body, 0)
    o_lat = _flash_out(l_s, acc_s, tq)
    for h in range(N_HEADS):
        o_ref[:, h * V_DIM:(h + 1) * V_DIM] = _dot(o_lat[h * tq:(h + 1) * tq], wuv_ref[h])


def _mla_prompt(qc, kc, wuv, batch, seq, tq):
    nq = seq // tq
    return pl.pallas_call(
        functools.partial(_mla_prompt_kernel, tq=tq),
        out_shape=jax.ShapeDtypeStruct((batch * seq, BRANCH_W), F32),
        grid=(batch, nq),
        in_specs=[pl.BlockSpec((tq, N_HEADS * MLA_QW), lambda b, i: (b * nq + i, 0)),
                  pl.BlockSpec((seq, MLA_QW), lambda b, i: (b, 0)),
                  pl.BlockSpec(wuv.shape, lambda b, i: (0, 0, 0))],
        out_specs=pl.BlockSpec((tq, BRANCH_W), lambda b, i: (b * nq + i, 0)),
        scratch_shapes=[pltpu.VMEM((N_HEADS * tq, MLA_QW), BF16),
                        pltpu.VMEM((N_HEADS, tq, 1), F32), pltpu.VMEM((N_HEADS, tq, 1), F32),
                        pltpu.VMEM((N_HEADS * tq, KV_LORA), F32)],
        compiler_params=_cp(("arbitrary", "arbitrary")),
        name="mla_prompt",
    )(qc, kc, wuv)


def _hgrn_gates(hg, lb):
    w = BRANCH_W
    q, fl, iv, gg = hg[:, 0:w], hg[:, w:2 * w], hg[:, 2 * w:3 * w], hg[:, 3 * w:4 * w]
    f = lb + (1.0 - lb) * _sigmoid(fl)
    return _silu(q), f, 1.0 - f, iv, gg


def _hgrn_finish(o, gg, hgn, ones_bf):
    ms = _dot_hilo(o * o, ones_bf) * (1.0 / HEAD_DIM)
    return o * lax.rsqrt(ms + EPS) * hgn * _silu(gg)


def _hgrn_prompt_kernel(hg_ref, lb_ref, hgn_ref, o_ref, s_ref, st_s, q_s, b_s, k_s, v_s, o_s, w_s, *, tc):
    C = HGRN_CHUNK
    w = BRANCH_W
    i = pl.program_id(1)

    @pl.when(i == 0)
    def _():
        st_s[...] = jnp.zeros(st_s.shape, F32)

    qf, f, k, iv, gg = _hgrn_gates(hg_ref[...], lb_ref[...])
    b = jnp.log(jnp.maximum(f, 1e-20))
    row = _iota((tc, 1), 0) % C
    for s in (1, 2, 4, 8):
        b = b + jnp.where(row >= s, pltpu.roll(b, s, 0), 0.0)
    q_s[...] = qf
    b_s[...] = b
    k_s[...] = k
    v_s[...] = iv
    same_head = _head_block_ones(w)
    ones_bf = same_head.astype(BF16)
    group = (_iota((C, C * C), 0) == _iota((C, C * C), 1) // C).astype(BF16)
    s_idx = _iota((C, 1), 0)

    def chunk(ci, carry):
        r0 = pl.multiple_of(ci * C, C)
        qc, bc, kc, vc = q_s[pl.ds(r0, C), :], b_s[pl.ds(r0, C), :], k_s[pl.ds(r0, C), :], v_s[pl.ds(r0, C), :]
        for tt in range(C):
            dec = jnp.exp(jnp.where(s_idx <= tt, bc[tt:tt + 1, :] - bc, -jnp.inf))
            w_s[tt * C:(tt + 1) * C, :] = dec * qc[tt:tt + 1, :] * kc
        a_rep = _dot_hilo(w_s[...], ones_bf)
        prod = (a_rep.reshape(C, C, w) * vc[None]).reshape(C * C, w)
        hi = prod.astype(BF16)
        lo = (prod - hi.astype(F32)).astype(BF16)
        o_intra = jnp.dot(group, hi, preferred_element_type=F32) + jnp.dot(group, lo, preferred_element_type=F32)
        st = st_s[...]
        o_s[pl.ds(r0, C), :] = o_intra + _dot_nt(qc * jnp.exp(bc), st)
        bl = bc[C - 1:C, :]
        upd = _dot_tn(vc, kc * jnp.exp(bl - bc))
        st_s[...] = st * jnp.exp(bl) + jnp.where(same_head, upd, 0.0)
        return carry

    lax.fori_loop(0, tc // C, chunk, 0)
    o_ref[...] = _hgrn_finish(o_s[...], gg, hgn_ref[...], ones_bf)

    @pl.when(i == pl.num_programs(1) - 1)
    def _():
        for h in range(N_HEADS):
            s_ref[0, h] = st_s[h * HEAD_DIM:(h + 1) * HEAD_DIM, h * HEAD_DIM:(h + 1) * HEAD_DIM].T


def _hgrn_prompt(hg, lower, hgn, batch, seq, tc):
    nt = seq // tc
    w = BRANCH_W
    return pl.pallas_call(
        functools.partial(_hgrn_prompt_kernel, tc=tc),
        out_shape=(jax.ShapeDtypeStruct((batch * seq, w), F32),
                   jax.ShapeDtypeStruct((batch, N_HEADS, HEAD_DIM, HEAD_DIM), F32)),
        grid=(batch, nt),
        in_specs=[pl.BlockSpec((tc, 4 * w), lambda b, i: (b * nt + i, 0)),
                  pl.BlockSpec((1, w), lambda b, i: (0, 0)),
                  pl.BlockSpec((1, w), lambda b, i: (0, 0))],
        out_specs=(pl.BlockSpec((tc, w), lambda b, i: (b * nt + i, 0)),
                   pl.BlockSpec((1, N_HEADS, HEAD_DIM, HEAD_DIM), lambda b, i: (b, 0, 0, 0))),
        scratch_shapes=[pltpu.VMEM((w, w), F32)] + [pltpu.VMEM((tc, w), F32)] * 5
                       + [pltpu.VMEM((HGRN_CHUNK * HGRN_CHUNK, w), F32)],
        compiler_params=_cp(("arbitrary", "arbitrary")),
        name="hgrn_prompt",
    )(hg, lower, hgn)


def _hgrn_step_kernel(hg_ref, lb_ref, hgn_ref, s0_ref, o_ref, s_ref, o_s, *, bt):
    qf, f, k, iv, gg = _hgrn_gates(hg_ref[...], lb_ref[...])
    fT = jnp.maximum(f, 1e-20).T
    kT = k.T
    qT = qf.T
    for bi in range(bt):
        for h in range(N_HEADS):
            hs = slice(h * HEAD_DIM, (h + 1) * HEAD_DIM)
            s_new = fT[hs, bi:bi + 1] * s0_ref[bi, h] + kT[hs, bi:bi + 1] * iv[bi:bi + 1, hs]
            s_ref[bi, h] = s_new
            o_s[bi:bi + 1, hs] = jnp.sum(qT[hs, bi:bi + 1] * s_new, axis=0, keepdims=True)
    o_ref[...] = _hgrn_finish(o_s[...], gg, hgn_ref[...], _head_block_ones(BRANCH_W).astype(BF16))


def _hgrn_step(hg, lower, hgn, s0, bt):
    m = hg.shape[0]
    w = BRANCH_W
    sblk = pl.BlockSpec((bt, N_HEADS, HEAD_DIM, HEAD_DIM), lambda i: (i, 0, 0, 0))
    return pl.pallas_call(
        functools.partial(_hgrn_step_kernel, bt=bt),
        out_shape=(jax.ShapeDtypeStruct((m, w), F32), jax.ShapeDtypeStruct(s0.shape, F32)),
        grid=(m // bt,),
        in_specs=[pl.BlockSpec((bt, 4 * w), lambda i: (i, 0)),
                  pl.BlockSpec((1, w), lambda i: (0, 0)), pl.BlockSpec((1, w), lambda i: (0, 0)), sblk],
        out_specs=(pl.BlockSpec((bt, w), lambda i: (i, 0)), sblk),
        scratch_shapes=[pltpu.VMEM((bt, w), F32)],
        compiler_params=_cp(("arbitrary",)),
        name="hgrn_step",
    )(hg, lower, hgn, s0)


def _merge_kernel(x_ref, o0_ref, o1_ref, o2_ref, o3_ref, mg_ref, gate_ref, gpost_ref, wb_ref, wo_ref, y_ref):
    d = x_ref.shape[1]
    mixed = None
    for n, o_ref in enumerate((o0_ref, o1_ref, o2_ref, o3_ref)):
        term = _sigmoid(mg_ref[:, n * d:(n + 1) * d]) * _dot(o_ref[...], wb_ref[n])
        mixed = term if mixed is None else mixed + term
    y = _dot(mixed, wo_ref[...])
    y_ref[...] = x_ref[...] + gate_ref[0] * _rms(y, gpost_ref[...])


def _merge(x, branches, merge_g, gate, gpost, wb, wo, tm, tps):
    m, d = x.shape
    row = lambda w: pl.BlockSpec((tm, w), lambda i: (i, 0))
    return pl.pallas_call(
        _merge_kernel,
        out_shape=jax.ShapeDtypeStruct((m, d), F32),
        grid=(m // tm,),
        in_specs=[row(d)] + [row(BRANCH_W)] * 4 + [row(N_BRANCH * d), _mod_spec(gate, tm, tps),
                  pl.BlockSpec((1, d), lambda i: (0, 0)),
                  pl.BlockSpec(wb.shape, lambda i: (0, 0, 0), pipeline_mode=pl.Buffered(1)),
                  pl.BlockSpec(wo.shape, lambda i: (0, 0), pipeline_mode=pl.Buffered(1))],
        out_specs=row(d),
        compiler_params=_cp(("arbitrary",)),
        name="merge",
    )(x, *branches, merge_g, gate, gpost, wb, wo)


def _gelu_tanh(x):
    return 0.5 * x * (1.0 + jnp.tanh(math.sqrt(2.0 / math.pi) * (x + 0.044715 * (x * x * x))))


def _ffn_kernel(*refs, tm, tps, nff, stepwise):
    if stepwise:
        (x_ref, gpre_ref, sh_ref, sc_ref, gate_ref, gpost_ref, wg_ref, wv_ref, cwg_ref, cwv_ref, cbg_ref, cbv_ref,
         wd_ref, p0g_ref, p0v_ref, p1g_ref, p1v_ref, y_ref, ug_ref, uv_ref, h_s, acc_s) = refs
    else:
        (x_ref, gpre_ref, sh_ref, sc_ref, gate_ref, gpost_ref, wg_ref, wv_ref, cwg_ref, cwv_ref, cbg_ref, cbv_ref,
         wd_ref, y_ref, csg_ref, csv_ref, h_s, acc_s, carry_g, carry_v) = refs
    i = pl.program_id(0)
    j = pl.program_id(1)

    @pl.when(j == 0)
    def _():
        h = _rms(x_ref[...], gpre_ref[...]) * (1.0 + sc_ref[0]) + sh_ref[0]
        h_s[...] = h.astype(BF16)
        acc_s[...] = jnp.zeros(acc_s.shape, F32)

    hb = h_s[...]
    ug = jnp.dot(hb, wg_ref[...], preferred_element_type=F32)
    uv = jnp.dot(hb, wv_ref[...], preferred_element_type=F32)

    if stepwise:
        def conv(u, cw_ref, cb_ref, p0_ref, p1_ref):
            return cb_ref[...] + p0_ref[...] * cw_ref[0:1, :] + p1_ref[...] * cw_ref[1:2, :] + u * cw_ref[2:3, :]
        cg = conv(ug, cwg_ref, cbg_ref, p0g_ref, p1g_ref)
        cv = conv(uv, cwv_ref, cbv_ref, p0v_ref, p1v_ref)
        ug_ref[...] = ug
        uv_ref[...] = uv
    else:
        first = (i % tps) == 0
        row = _iota((tm, 1), 0)

        @pl.when(i == 0)
        def _():
            carry_g[j] = jnp.zeros(carry_g.shape[1:], F32)
            carry_v[j] = jnp.zeros(carry_v.shape[1:], F32)

        def conv(u, cw_ref, cb_ref, carry):
            prev = jnp.where(first, 0.0, carry[j])
            um1 = jnp.where(row == 0, prev[1:2, :], pltpu.roll(u, 1, 0))
            um2 = jnp.where(row == 0, prev[0:1, :], jnp.where(row == 1, prev[1:2, :], pltpu.roll(u, 2, 0)))
            return cb_ref[...] + um2 * cw_ref[0:1, :] + um1 * cw_ref[1:2, :] + u * cw_ref[2:3, :]
        cg = conv(ug, cwg_ref, cbg_ref, carry_g)
        cv = conv(uv, cwv_ref, cbv_ref, carry_v)
        carry_g[j] = ug[tm - 2:tm, :]
        carry_v[j] = uv[tm - 2:tm, :]
        csg_ref[0, j] = ug[tm - 2:tm, :]
        csv_ref[0, j] = uv[tm - 2:tm, :]

    acc_s[...] += _dot(_gelu_tanh(cg) * cv, wd_ref[...])

    @pl.when(j == nff - 1)
    def _():
        y_ref[...] = x_ref[...] + gate_ref[0] * _rms(acc_s[...], gpost_ref[...])


def _ffn(x, gpre, shift, scale, gate, gpost, w_up, conv_w, conv_b, w_down, tm, tps, prev=None):
    m, d = x.shape
    dff = w_down.shape[0]
    nff = 2 if dff % (2 * LANES) == 0 else 1
    fc = dff // nff
    stepwise = prev is not None
    vec = lambda: pl.BlockSpec((1, d), lambda i, j: (0, 0))
    colg = lambda r: pl.BlockSpec((r, fc), lambda i, j: (0, j))
    colv = lambda r: pl.BlockSpec((r, fc), lambda i, j: (0, nff + j))
    in_specs = [pl.BlockSpec((tm, d), lambda i, j: (i, 0)), vec(),
                _mod_spec(shift, tm, tps), _mod_spec(scale, tm, tps), _mod_spec(gate, tm, tps), vec(),
                colg(d), colv(d), colg(CONV_W), colv(CONV_W), colg(1), colv(1),
                pl.BlockSpec((fc, d), lambda i, j: (j, 0))]
    args = [x, gpre, shift, scale, gate, gpost, w_up, w_up, conv_w, conv_w, conv_b, conv_b, w_down]
    scratch = [pltpu.VMEM((tm, d), BF16), pltpu.VMEM((tm, d), F32)]
    if stepwise:
        p0, p1 = prev
        in_specs += [pl.BlockSpec((tm, fc), lambda i, j: (i, j)), pl.BlockSpec((tm, fc), lambda i, j: (i, nff + j))] * 2
        args += [p0, p0, p1, p1]
        out_shape = (jax.ShapeDtypeStruct((m, d), F32), jax.ShapeDtypeStruct((m, dff), F32), jax.ShapeDtypeStruct((m, dff), F32))
        out_specs = (pl.BlockSpec((tm, d), lambda i, j: (i, 0)),
                     pl.BlockSpec((tm, fc), lambda i, j: (i, j)), pl.BlockSpec((tm, fc), lambda i, j: (i, j)))
    else:
        nseq = m // (tm * tps)
        out_shape = (jax.ShapeDtypeStruct((m, d), F32),
                     jax.ShapeDtypeStruct((nseq, nff, CONV_W - 1, fc), F32),
                     jax.ShapeDtypeStruct((nseq, nff, CONV_W - 1, fc), F32))
        cs = pl.BlockSpec((1, nff, CONV_W - 1, fc), lambda i, j: (i // tps, 0, 0, 0))
        out_specs = (pl.BlockSpec((tm, d), lambda i, j: (i, 0)), cs, cs)
        scratch += [pltpu.VMEM((nff, CONV_W - 1, fc), F32)] * 2
    return pl.pallas_call(
        functools.partial(_ffn_kernel, tm=tm, tps=tps, nff=nff, stepwise=stepwise),
        out_shape=out_shape,
        grid=(m // tm, nff),
        in_specs=in_specs,
        out_specs=out_specs,
        scratch_shapes=scratch,
        compiler_params=_cp(("arbitrary", "arbitrary")),
        name="ffn_step" if stepwise else "ffn_seq",
    )(*args)


def _page_specs(layer, pg, block, col):
    def spec(k):
        return pl.BlockSpec((None, None) + block, lambda b, j, pt, *_: (layer, pt[b, j * pg + k], 0, col))
    return [spec(k) for k in range(pg)]


def _rows8(x, w):
    return jnp.concatenate([x[:, h * w:(h + 1) * w] for h in range(N_HEADS)] + [jnp.zeros((8 - N_HEADS, w), x.dtype)], axis=0)


def _col8(x):
    r = _iota((8, 1), 0)
    out = jnp.zeros((8, 1), F32)
    for h in range(N_HEADS):
        out = jnp.where(r == h, x[:, h:h + 1], out)
    return out


def _bias8(d, tab_ref, head0):
    bs = _rel_bias(d, tab_ref, range(head0, head0 + N_HEADS))
    r = _iota((8, d.shape[1]), 0)
    out = jnp.zeros((8, d.shape[1]), F32)
    for h in range(N_HEADS):
        out = jnp.where(r == h, bs[h], out)
    return out


def _softmax_rows(lg, valid):
    lg = jnp.where(valid, lg, -jnp.inf)
    mx = jnp.max(lg, axis=-1, keepdims=True)
    mx = jnp.where(mx == -jnp.inf, 0.0, mx)
    p = jnp.exp(lg - mx)
    return p / jnp.maximum(jnp.sum(p, axis=-1, keepdims=True), 1e-30)


def _write_heads(o_ref, o8, w):
    for h in range(N_HEADS):
        o_ref[0, :, h * w:(h + 1) * w] = o8[h:h + 1, :]


def _nsa_cmp_step_kernel(pt_ref, tab_ref, q_ref, *rest, pg, past, n_pick):
    pages, (o_ref, idx_ref, cmp_s) = rest[:pg], rest[pg:]
    j = pl.program_id(1)
    nsel = past // NSA_SEL_BLOCK
    nc = 2 * nsel
    per_page = PAGE_SIZE // NSA_SEL_BLOCK
    for k in range(pg):
        page = j * pg + k
        for u in range(per_page):
            r0 = u * NSA_SEL_BLOCK
            dst = page * per_page + u
            cmp_s[pl.ds(dst, 1), :] = jnp.sum(pages[k][r0:r0 + NSA_CMP_BLOCK, :], axis=0, keepdims=True) * (1.0 / NSA_CMP_BLOCK)
            cmp_s[pl.ds(nsel + dst, 1), :] = jnp.sum(pages[k][r0 + NSA_CMP_BLOCK:r0 + 2 * NSA_CMP_BLOCK, :],
                                                     axis=0, keepdims=True) * (1.0 / NSA_CMP_BLOCK)

    @pl.when(j == pl.num_programs(1) - 1)
    def _():
        q8 = _rows8(q_ref[0], HEAD_DIM)
        jj = _iota((1, nc), 1)
        c_end = jnp.where(jj < nsel, NSA_SEL_BLOCK * jj + (NSA_CMP_BLOCK - 1),
                          NSA_SEL_BLOCK * (jj - nsel) + (NSA_SEL_BLOCK - 1))
        lg = _dot_nt(q8, cmp_s[:, 0:HEAD_DIM]) * HEAD_DIM ** -0.5 + _bias8(past - c_end, tab_ref, 0)
        p = _softmax_rows(lg, c_end <= past)
        _write_heads(o_ref, _dot(p, cmp_s[:, HEAD_DIM:2 * HEAD_DIM]), HEAD_DIM)
        ps = p[0:1] + p[1:2] + p[2:3] + p[3:4]
        imp = ps[:, 0:nsel] + ps[:, nsel:nc]
        blk = _iota((1, nsel), 1)
        imp = jnp.where(blk == 0, -jnp.inf, imp)
        lane = _iota((1, LANES), 1)
        idx = jnp.where(lane == n_pick + 1, nsel, 0)
        for s in range(n_pick):
            top = jnp.max(imp, axis=-1, keepdims=True)
            first = jnp.min(jnp.where(imp == top, blk, nsel), axis=-1, keepdims=True)
            idx = jnp.where(lane == s + 1, first, idx)
            imp = jnp.where(blk == first, -jnp.inf, imp)
        idx_ref[0] = idx


def _nsa_cmp_step(page_table, rel_table, q3, cache_view, layer, pg, past):
    bsz, n_pages = page_table.shape
    nsel = past // NSA_SEL_BLOCK
    n_pick = min(NSA_TOP_N, nsel + 1) - 2
    grid_spec = pltpu.PrefetchScalarGridSpec(
        num_scalar_prefetch=1,
        grid=(bsz, n_pages // pg),
        in_specs=[pl.BlockSpec(memory_space=pltpu.SMEM),
                  pl.BlockSpec((1, 1, BRANCH_W), lambda b, j, pt: (b, 0, 0))]
                 + _page_specs(layer, pg, (PAGE_SIZE, 2 * HEAD_DIM), 0),
        out_specs=(pl.BlockSpec((1, 1, BRANCH_W), lambda b, j, pt: (b, 0, 0)),
                   pl.BlockSpec((1, 1, LANES), lambda b, j, pt: (b, 0, 0))),
        scratch_shapes=[pltpu.VMEM((2 * nsel, 2 * HEAD_DIM), F32)])
    return pl.pallas_call(
        functools.partial(_nsa_cmp_step_kernel, pg=pg, past=past, n_pick=n_pick),
        out_shape=(jax.ShapeDtypeStruct((bsz, 1, BRANCH_W), F32), jax.ShapeDtypeStruct((bsz, 1, LANES), I32)),
        grid_spec=grid_spec,
        compiler_params=_cp(("arbitrary", "arbitrary")),
        name="nsa_cmp_step",
    )(page_table, rel_table, q3, *([cache_view] * pg))


def _nsa_sel_step_kernel(idx_ref, pt_ref, tab_ref, q_ref, sm_ref, kv_ref, nw_ref, ocmp_ref, win_ref, *rest,
                         n_past, past):
    blocks, (o_ref, kv_s, win_s) = rest[:n_past], rest[n_past:]
    b = pl.program_id(0)
    sb = NSA_SEL_BLOCK
    scale = HEAD_DIM ** -0.5
    q8 = _rows8(q_ref[0], HEAD_DIM)
    nkeys = kv_s.shape[0]
    lane = _iota((1, nkeys), 1)
    pos = jnp.full((1, nkeys), past, I32)
    for k in range(n_past):
        kv_s[k * sb:(k + 1) * sb, :] = blocks[k][...]
        pos = jnp.where(lane // sb == k, idx_ref[b, k] * sb + lane % sb, pos)
    tail = jnp.concatenate([kv_ref[0][:, 2 * HEAD_DIM:4 * HEAD_DIM], jnp.zeros((sb - 1, 2 * HEAD_DIM), F32)], axis=0)
    kv_s[n_past * sb:(n_past + 1) * sb, :] = tail
    lg = _dot_nt(q8, kv_s[:, 0:HEAD_DIM]) * scale + _bias8(past - pos, tab_ref, 0)
    p = _softmax_rows(lg, lane <= n_past * sb)
    o_sel = _dot(p, kv_s[:, HEAD_DIM:2 * HEAD_DIM])

    wb = win_ref.shape[0]
    win_s[0:wb, :] = win_ref[...]
    win_s[wb:wb + 8, :] = jnp.concatenate([nw_ref[0], jnp.zeros((7, 2 * HEAD_DIM), F32)], axis=0)
    wl = _iota((1, wb + 8), 1)
    d = past - (past - wb + wl)
    lg = _dot_nt(q8, win_s[:, 0:HEAD_DIM]) * scale + _bias8(d, tab_ref, 0)
    p = _softmax_rows(lg, (wl <= wb) & (d <= NSA_WINDOW) & (past - wb + wl >= 0))
    o_win = _dot(p, win_s[:, HEAD_DIM:2 * HEAD_DIM])

    g = _sigmoid(sm_ref[0][:, SM_NSAG:SM_NSAG + 3 * N_HEADS])
    for h in range(N_HEADS):
        hs = slice(h * HEAD_DIM, (h + 1) * HEAD_DIM)
        o_ref[0, :, hs] = (g[:, 3 * h:3 * h + 1] * ocmp_ref[0][:, hs] + g[:, 3 * h + 1:3 * h + 2] * o_sel[h:h + 1, :]
                           + g[:, 3 * h + 2:3 * h + 3] * o_win[h:h + 1, :])


def _nsa_sel_step(idx, page_table, rel_table, q3, sm3, kv3, nw3, ocmp3, win_view, cache_view, layer, past):
    bsz = page_table.shape[0]
    nsel = past // NSA_SEL_BLOCK
    n_past = min(NSA_TOP_N, nsel + 1) - 1
    half = PAGE_SIZE // NSA_SEL_BLOCK
    wb = win_view.shape[2]
    tok = lambda w: pl.BlockSpec((1, 1, w), lambda b, idx, pt: (b, 0, 0))

    def blk_spec(k):
        return pl.BlockSpec((None, None, NSA_SEL_BLOCK, 2 * HEAD_DIM),
                            lambda b, idx, pt: (layer, pt[b, idx[b, k] // half], idx[b, k] % half, 1))
    grid_spec = pltpu.PrefetchScalarGridSpec(
        num_scalar_prefetch=2,
        grid=(bsz,),
        in_specs=[pl.BlockSpec(memory_space=pltpu.SMEM), tok(BRANCH_W), tok(LANES), tok(4 * HEAD_DIM), tok(2 * HEAD_DIM),
                  tok(BRANCH_W),
                  pl.BlockSpec((None, None, wb, 2 * HEAD_DIM), lambda b, idx, pt: (layer, b, 0, 0))]
                 + [blk_spec(k) for k in range(n_past)],
        out_specs=tok(BRANCH_W),
        scratch_shapes=[pltpu.VMEM(((n_past + 1) * NSA_SEL_BLOCK, 2 * HEAD_DIM), F32),
                        pltpu.VMEM((wb + 8, 2 * HEAD_DIM), F32)])
    return pl.pallas_call(
        functools.partial(_nsa_sel_step_kernel, n_past=n_past, past=past),
        out_shape=jax.ShapeDtypeStruct((bsz, 1, BRANCH_W), F32),
        grid_spec=grid_spec,
        compiler_params=_cp(("arbitrary",)),
        name="nsa_sel_step",
    )(idx, page_table, rel_table, q3, sm3, kv3, nw3, ocmp3, win_view, *([cache_view] * n_past))


def _index_weights(sm):
    return _col8(sm[:, SM_IDXW:SM_IDXW + IDX_HEADS])


def _dsa_score_step_kernel(pt_ref, iq_ref, sm_ref, *rest, pg):
    pages, (sc_ref, new_ref) = rest[:pg], rest[pg:]
    qi8 = _rows8(iq_ref[0], IDX_DIM)
    sm = sm_ref[0]
    wcol = _index_weights(sm)
    cst = IDX_DIM ** -0.5 * IDX_HEADS ** -0.5
    for k in range(pg):
        s = jnp.maximum(_dot_nt(qi8, pages[k][...]), 0.0)
        sc_ref[0, k:k + 1, :] = jnp.sum(s * wcol, axis=0, keepdims=True) * cst
    s_new = jnp.maximum(_dot_nt(qi8, jnp.broadcast_to(sm[:, SM_IDXK:SM_IDXK + IDX_DIM], (8, IDX_DIM))), 0.0)
    sc_new = jnp.sum(s_new[:, 0:1] * wcol, axis=0, keepdims=True) * cst
    new_ref[0] = jnp.where(_iota((1, LANES), 1) == 0, sc_new, -jnp.inf)


def _dsa_score_step(page_table, iq3, sm3, cache_idx, layer, pg):
    bsz, n_pages = page_table.shape
    tok = lambda w: pl.BlockSpec((1, 1, w), lambda b, j, pt: (b, 0, 0))
    grid_spec = pltpu.PrefetchScalarGridSpec(
        num_scalar_prefetch=1,
        grid=(bsz, n_pages // pg),
        in_specs=[tok(IDX_HEADS * IDX_DIM), tok(LANES)] + _page_specs(layer, pg, (PAGE_SIZE, IDX_DIM), 0),
        out_specs=(pl.BlockSpec((1, pg, PAGE_SIZE), lambda b, j, pt: (b, j, 0)), tok(LANES)))
    return pl.pallas_call(
        functools.partial(_dsa_score_step_kernel, pg=pg),
        out_shape=(jax.ShapeDtypeStruct((bsz, n_pages, PAGE_SIZE), F32), jax.ShapeDtypeStruct((bsz, 1, LANES), F32)),
        grid_spec=grid_spec,
        compiler_params=_cp(("arbitrary", "arbitrary")),
        name="dsa_score_step",
    )(page_table, iq3, sm3, *([cache_idx] * pg))


def _dsa_thr_step_kernel(sc_ref, new_ref, thr_ref, need_ref, *, topk):
    key = _ordered_key(sc_ref[...])
    key_new = _ordered_key(new_ref[...])
    bt = key.shape[0]

    def count(pred):
        return (jnp.sum(pred(key).astype(I32), axis=-1, keepdims=True)
                + jnp.sum(pred(key_new).astype(I32), axis=-1, keepdims=True))

    thr = _kth_largest_key(lambda cand: count(lambda x: x >= cand), topk, (bt, 1))
    need = topk - count(lambda x: x > thr)
    thr_ref[...] = jnp.broadcast_to(thr, thr_ref.shape)
    need_ref[...] = jnp.broadcast_to(need, need_ref.shape)


def _dsa_thr_step(scores, new, topk, bt):
    bsz, p = scores.shape
    return pl.pallas_call(
        functools.partial(_dsa_thr_step_kernel, topk=topk),
        out_shape=(jax.ShapeDtypeStruct((bsz, LANES), I32), jax.ShapeDtypeStruct((bsz, LANES), I32)),
        grid=(bsz // bt,),
        in_specs=[pl.BlockSpec((bt, p), lambda i: (i, 0)), pl.BlockSpec((bt, LANES), lambda i: (i, 0))],
        out_specs=(pl.BlockSpec((bt, LANES), lambda i: (i, 0)), pl.BlockSpec((bt, LANES), lambda i: (i, 0))),
        compiler_params=_cp(("arbitrary",)),
        name="dsa_thr_step",
    )(scores, new)


def _step_update(lg, valid, v, m_s, l_s, acc_s):
    p, alpha = _softmax_step(lg, valid, m_s, l_s)
    acc_s[...] = alpha * acc_s[...] + _dot(p, v)


def _dsa_att_step_kernel(pt_ref, tab_ref, q_ref, kv_ref, sc_ref, new_ref, thr_ref, need_ref, *rest, pg, past):
    pages, (o_ref, m_s, l_s, acc_s, run_s) = rest[:pg], rest[pg:]
    j = pl.program_id(1)
    scale = HEAD_DIM ** -0.5

    @pl.when(j == 0)
    def _():
        _flash_init(m_s, l_s, acc_s)
        run_s[...] = jnp.zeros(run_s.shape, F32)

    q8 = _rows8(q_ref[0], HEAD_DIM)
    thr = thr_ref[0][:, 0:1]
    need = need_ref[0][:, 0:1].astype(F32)
    strict_upper = (_iota((PAGE_SIZE, PAGE_SIZE), 0) < _iota((PAGE_SIZE, PAGE_SIZE), 1)).astype(BF16)
    lane = _iota((1, PAGE_SIZE), 1)
    far = jnp.concatenate([jnp.full((1, PAGE_SIZE), tab_ref[REL_BUCKETS - 1, N_HEADS + h], F32) for h in range(N_HEADS)]
                          + [jnp.zeros((8 - N_HEADS, PAGE_SIZE), F32)], axis=0)
    for k in range(pg):
        page = j * pg + k
        key = _ordered_key(sc_ref[0, k:k + 1, :])
        eq = key == thr
        eqb = jnp.where(eq, 1.0, 0.0).astype(BF16)
        before = jnp.dot(jnp.broadcast_to(eqb, (8, PAGE_SIZE)), strict_upper, preferred_element_type=F32)[0:1] + run_s[...]
        chosen = (key > thr) | (eq & (before < need))
        run_s[...] += jnp.sum(eqb.astype(F32), axis=-1, keepdims=True)
        d0 = past - page * PAGE_SIZE
        bias = lax.cond(d0 - (PAGE_SIZE - 1) >= REL_MAX_DIST,
                        lambda: far, lambda: _bias8(d0 - lane, tab_ref, N_HEADS))
        lg = _dot_nt(q8, pages[k][:, 0:HEAD_DIM]) * scale + bias
        _step_update(lg, chosen, pages[k][:, HEAD_DIM:2 * HEAD_DIM], m_s, l_s, acc_s)

    @pl.when(j == pl.num_programs(1) - 1)
    def _():
        key_new = _ordered_key(new_ref[0])
        eq = key_new == thr
        chosen = ((key_new > thr) | (eq & (run_s[...] < need))) & (lane == 0)
        kv8 = jnp.broadcast_to(kv_ref[0], (8, 2 * HEAD_DIM))
        lg = _dot_nt(q8, kv8[:, 0:HEAD_DIM])[:, 0:1] * scale + _bias8(jnp.zeros((1, 1), I32), tab_ref, N_HEADS)
        lg = jnp.broadcast_to(lg, (8, PAGE_SIZE))
        vrows = jnp.concatenate([kv_ref[0][:, HEAD_DIM:2 * HEAD_DIM], jnp.zeros((PAGE_SIZE - 1, HEAD_DIM), F32)], axis=0)
        _step_update(lg, chosen, vrows, m_s, l_s, acc_s)
        _write_heads(o_ref, acc_s[...] / jnp.maximum(l_s[...], 1e-30), HEAD_DIM)


def _dsa_att_step(page_table, rel_table, q3, kv3, scores, new, thr, need, cache_view, layer, pg, past):
    bsz, n_pages = page_table.shape
    tok = lambda w: pl.BlockSpec((1, 1, w), lambda b, j, pt: (b, 0, 0))
    grid_spec = pltpu.PrefetchScalarGridSpec(
        num_scalar_prefetch=1,
        grid=(bsz, n_pages // pg),
        in_specs=[pl.BlockSpec(memory_space=pltpu.SMEM), tok(BRANCH_W), tok(2 * HEAD_DIM),
                  pl.BlockSpec((1, pg, PAGE_SIZE), lambda b, j, pt: (b, j, 0)), tok(LANES), tok(LANES), tok(LANES)]
                 + _page_specs(layer, pg, (PAGE_SIZE, 2 * HEAD_DIM), 0),
        out_specs=tok(BRANCH_W),
        scratch_shapes=[pltpu.VMEM((8, 1), F32), pltpu.VMEM((8, 1), F32), pltpu.VMEM((8, HEAD_DIM), F32),
                        pltpu.VMEM((1, 1), F32)])
    return pl.pallas_call(
        functools.partial(_dsa_att_step_kernel, pg=pg, past=past),
        out_shape=jax.ShapeDtypeStruct((bsz, 1, BRANCH_W), F32),
        grid_spec=grid_spec,
        compiler_params=_cp(("arbitrary", "arbitrary")),
        name="dsa_att_step",
    )(page_table, rel_table, q3, kv3, scores, new, thr, need, *([cache_view] * pg))


def _mla_step_kernel(pt_ref, qc_ref, kc_ref, wuv_ref, *rest, pg):
    pages, (o_ref, m_s, l_s, acc_s) = rest[:pg], rest[pg:]
    j = pl.program_id(1)
    scale = MLA_HEAD_QK ** -0.5
    kw = KV_LORA + QK_ROPE

    @pl.when(j == 0)
    def _():
        _flash_init(m_s, l_s, acc_s)

    q8 = _rows8(qc_ref[0], MLA_QW)[:, 0:kw]
    for k in range(pg):
        page = pages[k][...]
        _step_update(_dot_nt(q8, page) * scale, True, page[:, 0:KV_LORA], m_s, l_s, acc_s)

    @pl.when(j == pl.num_programs(1) - 1)
    def _():
        k8 = jnp.broadcast_to(kc_ref[0][:, 0:kw], (8, kw))
        lg = jnp.broadcast_to(_dot_nt(q8, k8)[:, 0:1] * scale, (8, PAGE_SIZE))
        vrows = jnp.concatenate([kc_ref[0][:, 0:KV_LORA].astype(F32), jnp.zeros((PAGE_SIZE - 1, KV_LORA), F32)], axis=0)
        _step_update(lg, _iota((1, PAGE_SIZE), 1) == 0, vrows, m_s, l_s, acc_s)
        o_lat = acc_s[...] / jnp.maximum(l_s[...], 1e-30)
        for h in range(N_HEADS):
            o_ref[0, :, h * V_DIM:(h + 1) * V_DIM] = _dot(o_lat, wuv_ref[h])[h:h + 1, :]


def _mla_step(page_table, qc3, kc3, wuv, cache_mla, layer, pg):
    bsz, n_pages = page_table.shape
    tok = lambda w: pl.BlockSpec((1, 1, w), lambda b, j, pt: (b, 0, 0))
    grid_spec = pltpu.PrefetchScalarGridSpec(
        num_scalar_prefetch=1,
        grid=(bsz, n_pages // pg),
        in_specs=[tok(N_HEADS * MLA_QW), tok(MLA_QW), pl.BlockSpec(wuv.shape, lambda b, j, pt: (0, 0, 0))]
                 + _page_specs(layer, pg, (PAGE_SIZE, KV_LORA + QK_ROPE), 0),
        out_specs=tok(BRANCH_W),
        scratch_shapes=[pltpu.VMEM((8, 1), F32), pltpu.VMEM((8, 1), F32), pltpu.VMEM((8, KV_LORA), F32)])
    return pl.pallas_call(
        functools.partial(_mla_step_kernel, pg=pg),
        out_shape=jax.ShapeDtypeStruct((bsz, 1, BRANCH_W), F32),
        grid_spec=grid_spec,
        compiler_params=_cp(("arbitrary", "arbitrary")),
        name="mla_step",
    )(page_table, qc3, kc3, wuv, *([cache_mla] * pg))


def _permute_w_in(w_in):
    d = w_in.shape[0]
    o = {}
    off = 0
    for name, w in (('nsa_q', 256), ('nsa_kv', 384), ('nsa_g', 12), ('dsa_q', 256), ('dsa_kv', 128), ('idx_q', 256),
                    ('idx_k', 64), ('idx_w', 4), ('hg', 1024), ('mla_cq', 256), ('mla_ckv', 128), ('mla_kr', 32),
                    ('merge_g', w_in.shape[1] - 2800)):
        o[name] = w_in[:, off:off + w]
        off += w
    pad = jnp.zeros((d, LANES - 112), w_in.dtype)
    cols = [o['nsa_q'], o['nsa_kv'], o['dsa_q'], o['dsa_kv'], o['idx_q'], o['hg'], o['mla_cq'], o['mla_ckv'],
            o['idx_k'], o['mla_kr'], o['nsa_g'], o['idx_w'], pad, o['merge_g']]
    return jnp.concatenate(cols, axis=1).astype(BF16)


def _layer_weights(l, w):
    hq = MLA_HEAD_QK
    wuq = w['w_uq'][l].reshape(Q_LORA, N_HEADS, hq)
    hr = QK_ROPE // 2
    return dict(
        w_in=_permute_w_in(w['w_in'][l]),
        wn=wuq[:, :, :QK_NOPE].reshape(Q_LORA, N_HEADS * QK_NOPE).astype(BF16),
        wr1=wuq[:, :, QK_NOPE:QK_NOPE + hr].reshape(Q_LORA, N_HEADS * hr).astype(BF16),
        wr2=wuq[:, :, QK_NOPE + hr:].reshape(Q_LORA, N_HEADS * hr).astype(BF16),
        wukT=jnp.transpose(w['w_uk'][l], (1, 2, 0)).astype(BF16),
        wuv=jnp.transpose(w['w_uv'][l], (1, 0, 2)).astype(BF16),
        wb=w['w_branch'][l].astype(BF16),
        wo=w['w_out'][l].astype(BF16),
        w_up=w['w_up'][l].astype(BF16),
        w_down=w['w_down'][l].astype(BF16),
        conv_w=w['conv_w'][l],
        conv_b=w['conv_b'][l][None, :],
        g_pre_mix=w['g_pre_mix'][l][None, :], g_post_mix=w['g_post_mix'][l][None, :],
        g_pre_ffn=w['g_pre_ffn'][l][None, :], g_post_ffn=w['g_post_ffn'][l][None, :],
        gq=w['mla_q_norm'][l][None, :], gkv=w['mla_kv_norm'][l][None, :],
        hgn=jnp.tile(w['hg_norm'][l], N_HEADS)[None, :],
    )


def _largest_divisor(n, cap):
    for c in range(min(n, cap), 0, -1):
        if n % c == 0:
            return c
    return 1


def kernel(x_prompt, x_sample, cache_nsa_kv, cache_dsa_kv, cache_dsa_idx, cache_mla, state_nsa_win, state_hgrn, state_ffn_conv, page_table, c_prompt, c_sample, rel_table, w_ada, b_ada, g_pre_mix, g_post_mix, g_pre_ffn, g_post_ffn, w_in, hg_lb, hg_norm, mla_q_norm, mla_kv_norm, w_uq, w_uk, w_uv, w_branch, w_out, w_up, conv_w, conv_b, w_down):
    weights = dict(w_in=w_in, w_uq=w_uq, w_uk=w_uk, w_uv=w_uv, w_branch=w_branch, w_out=w_out, w_up=w_up,
                   w_down=w_down, conv_w=conv_w, conv_b=conv_b, g_pre_mix=g_pre_mix, g_post_mix=g_post_mix,
                   g_pre_ffn=g_pre_ffn, g_post_ffn=g_post_ffn, mla_q_norm=mla_q_norm, mla_kv_norm=mla_kv_norm,
                   hg_norm=hg_norm)
    depth = w_in.shape[0]
    batch, seq, d = x_prompt.shape
    dec, dec_seq, _ = x_sample.shape
    assert dec_seq == 1 and seq % 256 == 0 and dec % 8 == 0
    n_pool = cache_nsa_kv.shape[1]
    n_pages = page_table.shape[1]
    past = n_pages * PAGE_SIZE
    dff = w_down.shape[1]

    tq = 256
    tm_p = 256
    tps_p = seq // tm_p
    tm_s = _largest_divisor(dec, 128)
    pg = _largest_divisor(n_pages, 16)

    gam = jax.nn.softmax(hg_lb.astype(F32), axis=0)
    cum = jnp.cumsum(gam, axis=0)
    lower = cum - cum[0]

    inv = ROPE_THETA ** (-jnp.arange(0, QK_ROPE, 2, dtype=F32) / QK_ROPE)
    inv4 = jnp.tile(inv, N_HEADS)[None, :]

    nsel = seq // NSA_SEL_BLOCK
    jj = jnp.arange(2 * nsel, dtype=I32)
    cend = jnp.where(jj < nsel, NSA_SEL_BLOCK * jj + (NSA_CMP_BLOCK - 1),
                     NSA_SEL_BLOCK * (jj - nsel) + (NSA_SEL_BLOCK - 1))[None, :]
    bias_t, bias_cmp = _bias_tiles(rel_table, cend, tq, tq, seq)

    mod = _ada(jnp.concatenate([c_prompt, c_sample], axis=0), w_ada, b_ada)

    cache_nsa_v = cache_nsa_kv.reshape(depth, n_pool, PAGE_SIZE, 4 * HEAD_DIM)
    cache_dsa_v = cache_dsa_kv.reshape(depth, n_pool, PAGE_SIZE, 2 * HEAD_DIM)
    win_v = state_nsa_win.reshape(depth, dec, state_nsa_win.shape[2], 2 * HEAD_DIM)

    xp = x_prompt.reshape(batch * seq, d)
    xs = x_sample.reshape(dec, d)
    outs_p, outs_s = [], []
    for l in range(depth):
        lw = _layer_weights(l, weights)
        lower_l = lower[l][None, :]
        mp = [mod[l, :batch, k * d:(k + 1) * d][:, None, :] for k in range(6)]
        ms = [mod[l, batch:, k * d:(k + 1) * d][None, :, :] for k in range(6)]

        z = dict(zip(OUT_NAMES, _inproj(xp, lw['g_pre_mix'], mp[0], mp[1], lw['w_in'], tm_p, tps_p)))
        qc, kc, mla_st = _mla_proj(z['mla_cq'], z['mla_ckv'], z['small'], inv4, lw['gq'], lw['gkv'],
                                   lw['wn'], lw['wr1'], lw['wr2'], lw['wukT'], tm_p, seq, 0)
        o_nsa = _nsa_prompt(z['nsa_q'], z['small'], z['nsa_kv4'], z['nsa_win'], bias_t, bias_cmp, batch, seq, tq)
        o_dsa = _dsa_prompt(z['dsa_q'], z['idx_q'], z['small'], z['dsa_kv'], bias_t, batch, seq, tq)
        o_mla = _mla_prompt(qc, kc, lw['wuv'], batch, seq, tq)
        o_hg, s_new = _hgrn_prompt(z['hg'], lower_l, lw['hgn'], batch, seq, 256)
        xp = _merge(xp, (o_nsa, o_dsa, o_hg, o_mla), z['merge_g'], mp[2], lw['g_post_mix'], lw['wb'], lw['wo'], tm_p, tps_p)
        xp, csg, csv = _ffn(xp, lw['g_pre_ffn'], mp[3], mp[4], mp[5], lw['g_post_ffn'], lw['w_up'], lw['conv_w'],
                            lw['conv_b'], lw['w_down'], tm_p, tps_p)
        wl = min(NSA_WINDOW, seq)
        outs_p.append((z['nsa_kv4'].reshape(batch, seq, 4, HEAD_DIM),
                       z['dsa_kv'].reshape(batch, seq, 2, HEAD_DIM),
                       z['small'][:, SM_IDXK:SM_IDXK + IDX_DIM].reshape(batch, seq, IDX_DIM),
                       mla_st.reshape(batch, seq, KV_LORA + QK_ROPE),
                       z['nsa_win'].reshape(batch, seq, 2, HEAD_DIM)[:, seq - wl:],
                       s_new,
                       jnp.concatenate([jnp.swapaxes(csg, 1, 2).reshape(batch, CONV_W - 1, dff),
                                        jnp.swapaxes(csv, 1, 2).reshape(batch, CONV_W - 1, dff)], axis=-1)))

        z = dict(zip(OUT_NAMES, _inproj(xs, lw['g_pre_mix'], ms[0], ms[1], lw['w_in'], tm_s, 1)))
        qc, kc, mla_st = _mla_proj(z['mla_cq'], z['mla_ckv'], z['small'], inv4, lw['gq'], lw['gkv'],
                                   lw['wn'], lw['wr1'], lw['wr2'], lw['wukT'], tm_s, 1, past)
        r3 = lambda a: a.reshape(dec, 1, a.shape[-1])
        q3, sm3, kv3, nw3 = r3(z['nsa_q']), r3(z['small']), r3(z['nsa_kv4']), r3(z['nsa_win'])
        o_cmp, sel_idx = _nsa_cmp_step(page_table, rel_table, q3, cache_nsa_v, l, pg, past)
        o_nsa = _nsa_sel_step(sel_idx.reshape(dec, LANES), page_table, rel_table, q3, sm3, kv3, nw3, o_cmp, win_v,
                              cache_nsa_v, l, past)
        scores, sc_new = _dsa_score_step(page_table, r3(z['idx_q']), sm3, cache_dsa_idx, l, pg)
        thr, need = _dsa_thr_step(scores.reshape(dec, past), sc_new.reshape(dec, LANES),
                                  min(DSA_TOPK_MAX, (past + 1) // 4), 8)
        o_dsa = _dsa_att_step(page_table, rel_table, r3(z['dsa_q']), r3(z['dsa_kv']), scores, sc_new,
                              r3(thr), r3(need), cache_dsa_v, l, pg, past)
        o_mla = _mla_step(page_table, r3(qc), r3(kc), lw['wuv'], cache_mla, l, pg)
        o_hg, s_new = _hgrn_step(z['hg'], lower_l, lw['hgn'], state_hgrn[l], 8)
        xs = _merge(xs, (o_nsa.reshape(dec, BRANCH_W), o_dsa.reshape(dec, BRANCH_W), o_hg, o_mla.reshape(dec, BRANCH_W)),
                    z['merge_g'], ms[2], lw['g_post_mix'], lw['wb'], lw['wo'], tm_s, 1)
        prev = state_ffn_conv[l]
        xs, ug, uv = _ffn(xs, lw['g_pre_ffn'], ms[3], ms[4], ms[5], lw['g_post_ffn'], lw['w_up'], lw['conv_w'],
                          lw['conv_b'], lw['w_down'], tm_s, 1, prev=(prev[:, 0], prev[:, 1]))
        win_all = jnp.concatenate([state_nsa_win[l], z['nsa_win'].reshape(dec, 1, 2, HEAD_DIM)], axis=1)
        wl = min(NSA_WINDOW, win_all.shape[1])
        outs_s.append((z['nsa_kv4'].reshape(dec, 1, 4, HEAD_DIM),
                       z['dsa_kv'].reshape(dec, 1, 2, HEAD_DIM),
                       z['small'][:, SM_IDXK:SM_IDXK + IDX_DIM].reshape(dec, 1, IDX_DIM),
                       mla_st.reshape(dec, 1, KV_LORA + QK_ROPE),
                       win_all[:, win_all.shape[1] - wl:],
                       s_new,
                       jnp.stack([prev[:, 1], jnp.concatenate([ug, uv], axis=-1)], axis=1)))

    sp = [jnp.stack(v) for v in zip(*outs_p)]
    ss = [jnp.stack(v) for v in zip(*outs_s)]
    return (xp.reshape(batch, seq, d), xs.reshape(dec, 1, d),
            sp[0], ss[0], sp[1], ss[1], sp[2], ss[2], sp[3], ss[3], sp[4], ss[4], sp[5], ss[5], sp[6], ss[6])
```

```python
import functools
import math

import jax
import jax.numpy as jnp
from jax import lax
from jax.experimental import pallas as pl
from jax.experimental.pallas import tpu as pltpu

F32, BF16, I32 = jnp.float32, jnp.bfloat16, jnp.int32

N_HEADS = 4
HEAD_DIM = 64
BRANCH_W = N_HEADS * HEAD_DIM
N_BRANCH = 4
NSA_CMP_BLOCK = 32
NSA_SEL_BLOCK = 64
NSA_TOP_N = 16
NSA_WINDOW = 512
FORCE_SCORE = 1e9
IDX_HEADS = 4
IDX_DIM = 64
DSA_TOPK_MAX = 256
HGRN_CHUNK = 16
Q_LORA = 256
KV_LORA = 128
QK_NOPE = 64
QK_ROPE = 32
V_DIM = 64
MLA_HEAD_QK = QK_NOPE + QK_ROPE
ROPE_THETA = 10000.0
REL_BUCKETS = 32
REL_MAX_DIST = 512
CONV_W = 3
EPS = 1e-6
PAGE_SIZE = 128

LANES = 128
VMEM_LIMIT = 56 * 1024 * 1024

OUT_WIDTHS = (256, 256, 128, 256, 128, 256, 1024, 256, 128, 128, 4096)
OUT_NAMES = ('nsa_q', 'nsa_kv4', 'nsa_win', 'dsa_q', 'dsa_kv', 'idx_q', 'hg', 'mla_cq', 'mla_ckv', 'small', 'merge_g')
SM_IDXK, SM_KR, SM_NSAG, SM_IDXW = 0, 64, 96, 108
MLA_QW = 256
INT_MIN = -2 ** 31


def _cp(sem, vmem=VMEM_LIMIT):
    return pltpu.CompilerParams(dimension_semantics=sem, vmem_limit_bytes=vmem)


def _dot(a, b):
    return jnp.dot(a.astype(BF16), b.astype(BF16), preferred_element_type=F32)


def _dot_nt(a, b):
    return lax.dot_general(a.astype(BF16), b.astype(BF16), (((1,), (1,)), ((), ())), preferred_element_type=F32)


def _dot_tn(a, b):
    return lax.dot_general(a.astype(BF16), b.astype(BF16), (((0,), (0,)), ((), ())), preferred_element_type=F32)


def _dot_hilo(a, g):
    hi = a.astype(BF16)
    lo = (a - hi.astype(F32)).astype(BF16)
    return jnp.dot(hi, g, preferred_element_type=F32) + jnp.dot(lo, g, preferred_element_type=F32)


def _rms(x, g):
    return x * lax.rsqrt(jnp.mean(x * x, axis=-1, keepdims=True) + EPS) * g


def _sigmoid(x):
    return 1.0 / (1.0 + jnp.exp(-x))


def _silu(x):
    return x * _sigmoid(x)


def _iota(shape, dim):
    return lax.broadcasted_iota(I32, shape, dim)


def _head_block_ones(n):
    return (_iota((n, n), 0) // HEAD_DIM == _iota((n, n), 1) // HEAD_DIM)


def _rel_bias(d, tab_ref, heads):
    exact = REL_BUCKETS // 2
    n = jnp.maximum(d, 0)
    nf = jnp.maximum(n, 1).astype(F32)
    log_b = exact + (jnp.log(nf / exact) / math.log(REL_MAX_DIST / exact) * (REL_BUCKETS - exact)).astype(I32)
    bucket = jnp.where(n < exact, n, jnp.minimum(log_b, REL_BUCKETS - 1))
    outs = []
    for h in heads:
        o = jnp.full(d.shape, tab_ref[0, h], F32)
        for k in range(1, REL_BUCKETS):
            o = jnp.where(bucket == k, tab_ref[k, h], o)
        outs.append(o)
    return outs


def _ordered_key(x):
    x = jnp.where(x == 0.0, 0.0, x)
    bits = pltpu.bitcast(x, I32)
    return jnp.where(bits < 0, bits ^ 0x7FFFFFFF, bits)


def _kth_largest_key(count_ge, k, shape):
    t = jnp.full(shape, INT_MIN, I32)
    zero = jnp.zeros(shape, I32)
    t = jnp.where(count_ge(zero) >= k, zero, t)
    for bit in range(30, -1, -1):
        cand = t + (1 << bit)
        t = jnp.where(count_ge(cand) >= k, cand, t)
    return t


def _softmax_step(lg, mask, m_ref, l_ref):
    lg = jnp.where(mask, lg, -jnp.inf)
    m_old = m_ref[...]
    m_new = jnp.maximum(m_old, jnp.max(lg, axis=-1, keepdims=True))
    m_safe = jnp.where(m_new == -jnp.inf, 0.0, m_new)
    p = jnp.exp(lg - m_safe)
    alpha = jnp.exp(m_old - m_safe)
    l_ref[...] = alpha * l_ref[...] + jnp.sum(p, axis=-1, keepdims=True)
    m_ref[...] = m_new
    return p, alpha


def _ada_kernel(c_ref, w_ref, b_ref, o_ref):
    o_ref[0] = _dot(_silu(c_ref[...]), w_ref[0]) + b_ref[0]


def _ada(c, w_ada, b_ada):
    depth, d, n = w_ada.shape
    bc = c.shape[0]
    tn = 1536 if n % 1536 == 0 else n
    return pl.pallas_call(
        _ada_kernel,
        out_shape=jax.ShapeDtypeStruct((depth, bc, n), F32),
        grid=(depth, n // tn),
        in_specs=[pl.BlockSpec((bc, d), lambda l, j: (0, 0)),
                  pl.BlockSpec((1, d, tn), lambda l, j: (l, 0, j)),
                  pl.BlockSpec((1, 1, tn), lambda l, j: (l, 0, j))],
        out_specs=pl.BlockSpec((1, bc, tn), lambda l, j: (l, 0, j)),
        compiler_params=_cp(("arbitrary", "arbitrary")),
        name="ada",
    )(c, w_ada, b_ada.reshape(depth, 1, n))


def _bias_kernel(tab_ref, cend_ref, tile_ref, cmp_ref, *, tq, tk, seq):
    i = _iota((tq, tk), 0)
    j = _iota((tq, tk), 1)
    for k in range(4):
        bs = _rel_bias(k * tk + i - j, tab_ref, range(2 * N_HEADS))
        for h in range(2 * N_HEADS):
            tile_ref[k, h] = bs[h]
    t = _iota((seq, cend_ref.shape[1]), 0)
    bs = _rel_bias(t - cend_ref[...], tab_ref, range(N_HEADS))
    for h in range(N_HEADS):
        cmp_ref[h] = bs[h]


def _bias_tiles(rel_table, cend, tq, tk, seq):
    nc = cend.shape[1]
    return pl.pallas_call(
        functools.partial(_bias_kernel, tq=tq, tk=tk, seq=seq),
        out_shape=(jax.ShapeDtypeStruct((4, 2 * N_HEADS, tq, tk), F32),
                   jax.ShapeDtypeStruct((N_HEADS, seq, nc), F32)),
        in_specs=[pl.BlockSpec(memory_space=pltpu.SMEM),
                  pl.BlockSpec(memory_space=pltpu.VMEM)],
        out_specs=(pl.BlockSpec(memory_space=pltpu.VMEM), pl.BlockSpec(memory_space=pltpu.VMEM)),
        compiler_params=pltpu.CompilerParams(vmem_limit_bytes=VMEM_LIMIT),
        name="rel_bias_tiles",
    )(rel_table, cend)


def _inproj_kernel(x_ref, g_ref, sh_ref, sc_ref, w_ref, *o_refs):
    h = _rms(x_ref[...], g_ref[...]) * (1.0 + sc_ref[0]) + sh_ref[0]
    hb = h.astype(BF16)
    off = 0
    for o_ref, w in zip(o_refs, OUT_WIDTHS):
        o_ref[...] = jnp.dot(hb, w_ref[:, off:off + w], preferred_element_type=F32)
        off += w


def _mod_spec(arr, tm, tps):
    d = arr.shape[-1]
    if arr.shape[1] == 1:
        return pl.BlockSpec((1, 1, d), lambda i, *_: (i // tps, 0, 0))
    return pl.BlockSpec((1, tm, d), lambda i, *_: (0, i, 0))


def _inproj(x, g, shift, scale, w_p, tm, tps):
    m, d = x.shape
    n = w_p.shape[1]
    return pl.pallas_call(
        _inproj_kernel,
        out_shape=tuple(jax.ShapeDtypeStruct((m, w), F32) for w in OUT_WIDTHS),
        grid=(m // tm,),
        in_specs=[pl.BlockSpec((tm, d), lambda i: (i, 0)),
                  pl.BlockSpec((1, d), lambda i: (0, 0)),
                  _mod_spec(shift, tm, tps), _mod_spec(scale, tm, tps),
                  pl.BlockSpec((d, n), lambda i: (0, 0), pipeline_mode=pl.Buffered(1))],
        out_specs=tuple(pl.BlockSpec((tm, w), lambda i: (i, 0)) for w in OUT_WIDTHS),
        compiler_params=_cp(("arbitrary",)),
        name="inproj",
    )(x, g, shift, scale, w_p)


def _mla_proj_kernel(cq_ref, ckv_ref, sm_ref, inv_ref, gq_ref, gkv_ref, wn_ref, wr1_ref, wr2_ref, wuk_ref,
                     qc_ref, kc_ref, st_ref, *, tm, seq, pos0):
    i = pl.program_id(0)
    qn = _rms(cq_ref[...], gq_ref[...]).astype(BF16)
    q_nope = jnp.dot(qn, wn_ref[...], preferred_element_type=F32)
    r1 = jnp.dot(qn, wr1_ref[...], preferred_element_type=F32)
    r2 = jnp.dot(qn, wr2_ref[...], preferred_element_type=F32)
    pos = ((i * tm + _iota((tm, 1), 0)) % seq + pos0).astype(F32)
    ang = pos * inv_ref[...]
    c, s = jnp.cos(ang), jnp.sin(ang)
    rot1 = r1 * c - r2 * s
    rot2 = r1 * s + r2 * c
    hr = QK_ROPE // 2
    zpad = jnp.zeros((tm, MLA_QW - KV_LORA - QK_ROPE), F32)
    for h in range(N_HEADS):
        q_lat = _dot(q_nope[:, h * QK_NOPE:(h + 1) * QK_NOPE], wuk_ref[h])
        qh = jnp.concatenate([q_lat, rot1[:, h * hr:(h + 1) * hr], rot2[:, h * hr:(h + 1) * hr], zpad], axis=-1)
        qc_ref[:, h * MLA_QW:(h + 1) * MLA_QW] = qh.astype(BF16)
    ckv = _rms(ckv_ref[...], gkv_ref[...])
    sm = sm_ref[...]
    x1 = sm[:, SM_KR:SM_KR + hr]
    x2 = sm[:, SM_KR + hr:SM_KR + 2 * hr]
    ang1 = pos * inv_ref[:, 0:hr]
    c1, s1 = jnp.cos(ang1), jnp.sin(ang1)
    kr = jnp.concatenate([x1 * c1 - x2 * s1, x1 * s1 + x2 * c1], axis=-1)
    st = jnp.concatenate([ckv, kr], axis=-1)
    st_ref[...] = st
    kc_ref[...] = jnp.concatenate([st, zpad], axis=-1).astype(BF16)


def _mla_proj(cq, ckv, small, inv4, gq, gkv, wn, wr1, wr2, wukT, tm, seq, pos0):
    m = cq.shape[0]
    row = lambda w: pl.BlockSpec((tm, w), lambda i: (i, 0))
    full = lambda a: pl.BlockSpec(a.shape, lambda i: (0,) * a.ndim)
    return pl.pallas_call(
        functools.partial(_mla_proj_kernel, tm=tm, seq=seq, pos0=pos0),
        out_shape=(jax.ShapeDtypeStruct((m, N_HEADS * MLA_QW), BF16),
                   jax.ShapeDtypeStruct((m, MLA_QW), BF16),
                   jax.ShapeDtypeStruct((m, KV_LORA + QK_ROPE), F32)),
        grid=(m // tm,),
        in_specs=[row(Q_LORA), row(KV_LORA), row(LANES), full(inv4), full(gq), full(gkv),
                  full(wn), full(wr1), full(wr2), full(wukT)],
        out_specs=(row(N_HEADS * MLA_QW), row(MLA_QW), row(KV_LORA + QK_ROPE)),
        compiler_params=_cp(("arbitrary",)),
        name="mla_proj",
    )(cq, ckv, small, inv4, gq, gkv, wn, wr1, wr2, wukT)


def _stack_heads(q, dst_ref, tq, w):
    for h in range(N_HEADS):
        dst_ref[h * tq:(h + 1) * tq, :] = q[:, h * w:(h + 1) * w].astype(BF16)


def _flash_chunk(lg, mask, v, m_ref, l_ref, acc_ref, tq, tk):
    p, alpha = _softmax_step(lg.reshape(N_HEADS, tq, tk), mask[None], m_ref, l_ref)
    acc_ref[...] = alpha.reshape(N_HEADS * tq, 1) * acc_ref[...] + _dot(p.reshape(N_HEADS * tq, tk), v)


def _flash_init(m_ref, l_ref, acc_ref):
    m_ref[...] = jnp.full(m_ref.shape, -jnp.inf, F32)
    l_ref[...] = jnp.zeros(l_ref.shape, F32)
    acc_ref[...] = jnp.zeros(acc_ref.shape, F32)


def _flash_out(l_ref, acc_ref, tq):
    return acc_ref[...] / jnp.maximum(l_ref[...], 1e-30).reshape(N_HEADS * tq, 1)


def _nsa_prompt_kernel(q_ref, sm_ref, kv_ref, win_ref, bias_ref, bcmp_ref, o_ref,
                       cmp_s, q4_s, m_s, l_s, acc_s, *, seq, tq, n_top):
    tk = tq
    qb = pl.program_id(1)
    t0 = qb * tq
    nsel = seq // NSA_SEL_BLOCK
    nc = 2 * nsel
    scale = HEAD_DIM ** -0.5

    @pl.when(qb == 0)
    def _():
        for j in range(nsel):
            r0 = j * NSA_SEL_BLOCK
            cmp_s[j:j + 1, :] = jnp.sum(kv_ref[r0:r0 + NSA_CMP_BLOCK, 0:2 * HEAD_DIM], axis=0, keepdims=True) * (1.0 / NSA_CMP_BLOCK)
            cmp_s[nsel + j:nsel + j + 1, :] = jnp.sum(kv_ref[r0 + NSA_CMP_BLOCK:r0 + 2 * NSA_CMP_BLOCK, 0:2 * HEAD_DIM],
                                                      axis=0, keepdims=True) * (1.0 / NSA_CMP_BLOCK)

    _stack_heads(q_ref[...], q4_s, tq, HEAD_DIM)
    q4 = q4_s[...]
    t = t0 + _iota((tq, 1), 0)

    jj = _iota((1, nc), 1)
    c_end = jnp.where(jj < nsel, NSA_SEL_BLOCK * jj + (NSA_CMP_BLOCK - 1), NSA_SEL_BLOCK * (jj - nsel) + (NSA_SEL_BLOCK - 1))
    lg = (_dot_nt(q4, cmp_s[:, 0:HEAD_DIM]) * scale).reshape(N_HEADS, tq, nc) + bcmp_ref[...]
    lg = jnp.where((c_end <= t)[None], lg, -jnp.inf)
    mx = jnp.max(lg, axis=-1, keepdims=True)
    mx = jnp.where(mx == -jnp.inf, 0.0, mx)
    p = jnp.exp(lg - mx)
    p = p / jnp.maximum(jnp.sum(p, axis=-1, keepdims=True), 1e-30)
    o_cmp = _dot(p.reshape(N_HEADS * tq, nc), cmp_s[:, HEAD_DIM:2 * HEAD_DIM])

    ps = p[0] + p[1] + p[2] + p[3]
    imp = ps[:, 0:nsel] + ps[:, nsel:nc]
    blk = _iota((1, nsel), 1)
    cur = t // NSA_SEL_BLOCK
    imp = jnp.where((blk == cur) | (blk == 0), FORCE_SCORE, imp)
    imp = jnp.where(blk <= cur, imp, -jnp.inf)
    selm = jnp.zeros((tq, nsel), F32)
    for _ in range(n_top):
        top = jnp.max(imp, axis=-1, keepdims=True)
        first = jnp.min(jnp.where(imp == top, blk, nsel), axis=-1, keepdims=True)
        pick = blk == first
        selm = jnp.where(pick, 1.0, selm)
        imp = jnp.where(pick, -jnp.inf, imp)
    selm = selm.astype(BF16)

    _flash_init(m_s, l_s, acc_s)

    def sel_body(c, carry):
        s0 = pl.multiple_of(c * tk, tk)
        k = kv_ref[pl.ds(s0, tk), 2 * HEAD_DIM:3 * HEAD_DIM]
        v = kv_ref[pl.ds(s0, tk), 3 * HEAD_DIM:4 * HEAD_DIM]
        lg = (_dot_nt(q4, k) * scale).reshape(N_HEADS, tq, tk) + bias_ref[jnp.minimum(qb - c, 3)]
        s_pos = s0 + _iota((1, tk), 1)
        expand = (_iota((nsel, tk), 0) == (s0 + _iota((nsel, tk), 1)) // NSA_SEL_BLOCK).astype(BF16)
        chosen = jnp.dot(selm, expand, preferred_element_type=F32) > 0.5
        _flash_chunk(lg.reshape(N_HEADS * tq, tk), chosen & (s_pos <= t), v, m_s, l_s, acc_s, tq, tk)
        return carry

    lax.fori_loop(0, qb + 1, sel_body, 0)
    o_sel = _flash_out(l_s, acc_s, tq)

    _flash_init(m_s, l_s, acc_s)

    def win_body(c, carry):
        s0 = pl.multiple_of(c * tk, tk)
        k = win_ref[pl.ds(s0, tk), 0:HEAD_DIM]
        v = win_ref[pl.ds(s0, tk), HEAD_DIM:2 * HEAD_DIM]
        lg = (_dot_nt(q4, k) * scale).reshape(N_HEADS, tq, tk) + bias_ref[jnp.minimum(qb - c, 3)]
        d = t - (s0 + _iota((1, tk), 1))
        _flash_chunk(lg.reshape(N_HEADS * tq, tk), (d >= 0) & (d <= NSA_WINDOW), v, m_s, l_s, acc_s, tq, tk)
        return carry

    lax.fori_loop(jnp.maximum(qb - (NSA_WINDOW + tk - 1) // tk, 0), qb + 1, win_body, 0)
    o_win = _flash_out(l_s, acc_s, tq)

    g = _sigmoid(sm_ref[:, SM_NSAG:SM_NSAG + 3 * N_HEADS])
    for h in range(N_HEADS):
        rows = slice(h * tq, (h + 1) * tq)
        o_ref[:, h * HEAD_DIM:(h + 1) * HEAD_DIM] = (g[:, 3 * h:3 * h + 1] * o_cmp[rows]
                                                      + g[:, 3 * h + 1:3 * h + 2] * o_sel[rows]
                                                      + g[:, 3 * h + 2:3 * h + 3] * o_win[rows])


def _nsa_prompt(nsa_q, small, kv4, win, bias_t, bias_cmp, batch, seq, tq):
    nq = seq // tq
    nsel = seq // NSA_SEL_BLOCK
    rowq = lambda w: pl.BlockSpec((tq, w), lambda b, i: (b * nq + i, 0))
    rows = lambda w: pl.BlockSpec((seq, w), lambda b, i: (b, 0))
    return pl.pallas_call(
        functools.partial(_nsa_prompt_kernel, seq=seq, tq=tq, n_top=min(NSA_TOP_N, nsel)),
        out_shape=jax.ShapeDtypeStruct((batch * seq, BRANCH_W), F32),
        grid=(batch, nq),
        in_specs=[rowq(BRANCH_W), rowq(LANES), rows(4 * HEAD_DIM), rows(2 * HEAD_DIM),
                  pl.BlockSpec((4, N_HEADS, tq, tq), lambda b, i: (0, 0, 0, 0)),
                  pl.BlockSpec((N_HEADS, tq, 2 * nsel), lambda b, i: (0, i, 0))],
        out_specs=rowq(BRANCH_W),
        scratch_shapes=[pltpu.VMEM((2 * nsel, 2 * HEAD_DIM), F32),
                        pltpu.VMEM((N_HEADS * tq, HEAD_DIM), BF16),
                        pltpu.VMEM((N_HEADS, tq, 1), F32), pltpu.VMEM((N_HEADS, tq, 1), F32),
                        pltpu.VMEM((N_HEADS * tq, HEAD_DIM), F32)],
        compiler_params=_cp(("arbitrary", "arbitrary")),
        name="nsa_prompt",
    )(nsa_q, small, kv4, win, bias_t, bias_cmp)


def _dsa_prompt_kernel(q_ref, iq_ref, sm_ref, smf_ref, kv_ref, bias_ref, o_ref,
                       q4_s, qi4_s, key_s, m_s, l_s, acc_s, *, tq, topk):
    tk = tq
    qb = pl.program_id(1)
    t0 = qb * tq
    scale = HEAD_DIM ** -0.5
    _stack_heads(q_ref[...], q4_s, tq, HEAD_DIM)
    _stack_heads(iq_ref[...], qi4_s, tq, IDX_DIM)
    q4, qi4 = q4_s[...], qi4_s[...]
    t = t0 + _iota((tq, 1), 0)
    wi = sm_ref[:, SM_IDXW:SM_IDXW + IDX_HEADS]
    cst = IDX_DIM ** -0.5 * IDX_HEADS ** -0.5

    def score_body(c, carry):
        s0 = pl.multiple_of(c * tk, tk)
        ki = smf_ref[pl.ds(s0, tk), SM_IDXK:SM_IDXK + IDX_DIM]
        s = jnp.maximum(_dot_nt(qi4, ki), 0.0).reshape(IDX_HEADS, tq, tk)
        sc = (s[0] * wi[:, 0:1] + s[1] * wi[:, 1:2] + s[2] * wi[:, 2:3] + s[3] * wi[:, 3:4]) * cst
        sc = jnp.where(s0 + _iota((1, tk), 1) <= t, sc, -jnp.inf)
        key_s[c] = _ordered_key(sc)
        return carry

    lax.fori_loop(0, qb + 1, score_body, 0)

    def count(pred):
        def body(c, acc):
            hit = jnp.where(pred(key_s[c]), 1, 0)
            part = hit[:, 0:LANES]
            for u in range(1, tk // LANES):
                part = part + hit[:, u * LANES:(u + 1) * LANES]
            return acc + part
        acc = lax.fori_loop(0, qb + 1, body, jnp.zeros((tq, LANES), I32))
        return jnp.sum(acc, axis=-1, keepdims=True)

    thr = _kth_largest_key(lambda cand: count(lambda key: key >= cand), topk, (tq, 1))
    need = (topk - count(lambda key: key > thr)).astype(F32)

    strict_upper = (_iota((tk, tk), 0) < _iota((tk, tk), 1)).astype(BF16)
    _flash_init(m_s, l_s, acc_s)

    def att_body(c, run):
        s0 = pl.multiple_of(c * tk, tk)
        key = key_s[c]
        eq = key == thr
        eqb = jnp.where(eq, 1.0, 0.0).astype(BF16)
        before = jnp.dot(eqb, strict_upper, preferred_element_type=F32) + run
        chosen = (key > thr) | (eq & (before < need))
        k = kv_ref[pl.ds(s0, tk), 0:HEAD_DIM]
        v = kv_ref[pl.ds(s0, tk), HEAD_DIM:2 * HEAD_DIM]
        lg = (_dot_nt(q4, k) * scale).reshape(N_HEADS, tq, tk) + bias_ref[jnp.minimum(qb - c, 3)]
        mask = chosen & (s0 + _iota((1, tk), 1) <= t)
        _flash_chunk(lg.reshape(N_HEADS * tq, tk), mask, v, m_s, l_s, acc_s, tq, tk)
        return run + jnp.sum(eqb.astype(F32), axis=-1, keepdims=True)

    lax.fori_loop(0, qb + 1, att_body, jnp.zeros((tq, 1), F32))
    o = _flash_out(l_s, acc_s, tq)
    for h in range(N_HEADS):
        o_ref[:, h * HEAD_DIM:(h + 1) * HEAD_DIM] = o[h * tq:(h + 1) * tq]


def _dsa_prompt(dsa_q, idx_q, small, dsa_kv, bias_t, batch, seq, tq):
    nq = seq // tq
    rowq = lambda w: pl.BlockSpec((tq, w), lambda b, i: (b * nq + i, 0))
    rows = lambda w: pl.BlockSpec((seq, w), lambda b, i: (b, 0))
    return pl.pallas_call(
        functools.partial(_dsa_prompt_kernel, tq=tq, topk=min(DSA_TOPK_MAX, seq // 4)),
        out_shape=jax.ShapeDtypeStruct((batch * seq, BRANCH_W), F32),
        grid=(batch, nq),
        in_specs=[rowq(BRANCH_W), rowq(IDX_HEADS * IDX_DIM), rowq(LANES), rows(LANES), rows(2 * HEAD_DIM),
                  pl.BlockSpec((4, N_HEADS, tq, tq), lambda b, i: (0, 1, 0, 0))],
        out_specs=rowq(BRANCH_W),
        scratch_shapes=[pltpu.VMEM((N_HEADS * tq, HEAD_DIM), BF16),
                        pltpu.VMEM((IDX_HEADS * tq, IDX_DIM), BF16),
                        pltpu.VMEM((nq, tq, tq), I32),
                        pltpu.VMEM((N_HEADS, tq, 1), F32), pltpu.VMEM((N_HEADS, tq, 1), F32),
                        pltpu.VMEM((N_HEADS * tq, HEAD_DIM), F32)],
        compiler_params=_cp(("arbitrary", "arbitrary")),
        name="dsa_prompt",
    )(dsa_q, idx_q, small, small, dsa_kv, bias_t)


def _mla_prompt_kernel(qc_ref, kc_ref, wuv_ref, o_ref, q4_s, m_s, l_s, acc_s, *, tq):
    tk = tq
    qb = pl.program_id(1)
    t = qb * tq + _iota((tq, 1), 0)
    scale = MLA_HEAD_QK ** -0.5
    _stack_heads(qc_ref[...], q4_s, tq, MLA_QW)
    q4 = q4_s[...]
    _flash_init(m_s, l_s, acc_s)

    def body(c, carry):
        s0 = pl.multiple_of(c * tk, tk)
        kc = kc_ref[pl.ds(s0, tk), :]
        lg = _dot_nt(q4, kc) * scale
        _flash_chunk(lg, s0 + _iota((1, tk), 1) <= t, kc[:, 0:KV_LORA], m_s, l_s, acc_s, tq, tk)
        return carry

    lax.fori_loop(0, qb + 1, body, 0)
    o_lat = _flash_out(l_s, acc_s, tq)
    for h in range(N_HEADS):
        o_ref[:, h * V_DIM:(h + 1) * V_DIM] = _dot(o_lat[h * tq:(h + 1) * tq], wuv_ref[h])


def _mla_prompt(qc, kc, wuv, batch, seq, tq):
    nq = seq // tq
    return pl.pallas_call(
        functools.partial(_mla_prompt_kernel, tq=tq),
        out_shape=jax.ShapeDtypeStruct((batch * seq, BRANCH_W), F32),
        grid=(batch, nq),
        in_specs=[pl.BlockSpec((tq, N_HEADS * MLA_QW), lambda b, i: (b * nq + i, 0)),
                  pl.BlockSpec((seq, MLA_QW), lambda b, i: (b, 0)),
                  pl.BlockSpec(wuv.shape, lambda b, i: (0, 0, 0))],
        out_specs=pl.BlockSpec((tq, BRANCH_W), lambda b, i: (b * nq + i, 0)),
        scratch_shapes=[pltpu.VMEM((N_HEADS * tq, MLA_QW), BF16),
                        pltpu.VMEM((N_HEADS, tq, 1), F32), pltpu.VMEM((N_HEADS, tq, 1), F32),
                        pltpu.VMEM((N_HEADS * tq, KV_LORA), F32)],
        compiler_params=_cp(("arbitrary", "arbitrary")),
        name="mla_prompt",
    )(qc, kc, wuv)


def _hgrn_gates(hg, lb):
    w = BRANCH_W
    q, fl, iv, gg = hg[:, 0:w], hg[:, w:2 * w], hg[:, 2 * w:3 * w], hg[:, 3 * w:4 * w]
    f = lb + (1.0 - lb) * _sigmoid(fl)
    return _silu(q), f, 1.0 - f, iv, gg


def _hgrn_finish(o, gg, hgn, ones_bf):
    ms = _dot_hilo(o * o, ones_bf) * (1.0 / HEAD_DIM)
    return o * lax.rsqrt(ms + EPS) * hgn * _silu(gg)


def _hgrn_prompt_kernel(hg_ref, lb_ref, hgn_ref, o_ref, s_ref, st_s, q_s, b_s, k_s, v_s, o_s, w_s, *, tc):
    C = HGRN_CHUNK
    w = BRANCH_W
    i = pl.program_id(1)

    @pl.when(i == 0)
    def _():
        st_s[...] = jnp.zeros(st_s.shape, F32)

    qf, f, k, iv, gg = _hgrn_gates(hg_ref[...], lb_ref[...])
    b = jnp.log(jnp.maximum(f, 1e-20))
    row = _iota((tc, 1), 0) % C
    for s in (1, 2, 4, 8):
        b = b + jnp.where(row >= s, pltpu.roll(b, s, 0), 0.0)
    q_s[...] = qf
    b_s[...] = b
    k_s[...] = k
    v_s[...] = iv
    same_head = _head_block_ones(w)
    ones_bf = same_head.astype(BF16)
    group = (_iota((C, C * C), 0) == _iota((C, C * C), 1) // C).astype(BF16)
    s_idx = _iota((C, 1), 0)

    def chunk(ci, carry):
        r0 = pl.multiple_of(ci * C, C)
        qc, bc, kc, vc = q_s[pl.ds(r0, C), :], b_s[pl.ds(r0, C), :], k_s[pl.ds(r0, C), :], v_s[pl.ds(r0, C), :]
        for tt in range(C):
            dec = jnp.exp(jnp.where(s_idx <= tt, bc[tt:tt + 1, :] - bc, -jnp.inf))
            w_s[tt * C:(tt + 1) * C, :] = dec * qc[tt:tt + 1, :] * kc
        a_rep = _dot_hilo(w_s[...], ones_bf)
        prod = (a_rep.reshape(C, C, w) * vc[None]).reshape(C * C, w)
        hi = prod.astype(BF16)
        lo = (prod - hi.astype(F32)).astype(BF16)
        o_intra = jnp.dot(group, hi, preferred_element_type=F32) + jnp.dot(group, lo, preferred_element_type=F32)
        st = st_s[...]
        o_s[pl.ds(r0, C), :] = o_intra + _dot_nt(qc * jnp.exp(bc), st)
        bl = bc[C - 1:C, :]
        upd = _dot_tn(vc, kc * jnp.exp(bl - bc))
        st_s[...] = st * jnp.exp(bl) + jnp.where(same_head, upd, 0.0)
        return carry

    lax.fori_loop(0, tc // C, chunk, 0)
    o_ref[...] = _hgrn_finish(o_s[...], gg, hgn_ref[...], ones_bf)

    @pl.when(i == pl.num_programs(1) - 1)
    def _():
        for h in range(N_HEADS):
            s_ref[0, h] = st_s[h * HEAD_DIM:(h + 1) * HEAD_DIM, h * HEAD_DIM:(h + 1) * HEAD_DIM].T


def _hgrn_prompt(hg, lower, hgn, batch, seq, tc):
    nt = seq // tc
    w = BRANCH_W
    return pl.pallas_call(
        functools.partial(_hgrn_prompt_kernel, tc=tc),
        out_shape=(jax.ShapeDtypeStruct((batch * seq, w), F32),
                   jax.ShapeDtypeStruct((batch, N_HEADS, HEAD_DIM, HEAD_DIM), F32)),
        grid=(batch, nt),
        in_specs=[pl.BlockSpec((tc, 4 * w), lambda b, i: (b * nt + i, 0)),
                  pl.BlockSpec((1, w), lambda b, i: (0, 0)),
                  pl.BlockSpec((1, w), lambda b, i: (0, 0))],
        out_specs=(pl.BlockSpec((tc, w), lambda b, i: (b * nt + i, 0)),
                   pl.BlockSpec((1, N_HEADS, HEAD_DIM, HEAD_DIM), lambda b, i: (b, 0, 0, 0))),
        scratch_shapes=[pltpu.VMEM((w, w), F32)] + [pltpu.VMEM((tc, w), F32)] * 5
                       + [pltpu.VMEM((HGRN_CHUNK * HGRN_CHUNK, w), F32)],
        compiler_params=_cp(("arbitrary", "arbitrary")),
        name="hgrn_prompt",
    )(hg, lower, hgn)


def _hgrn_step_kernel(hg_ref, lb_ref, hgn_ref, s0_ref, o_ref, s_ref, o_s, *, bt):
    qf, f, k, iv, gg = _hgrn_gates(hg_ref[...], lb_ref[...])
    fT = jnp.maximum(f, 1e-20).T
    kT = k.T
    qT = qf.T
    for bi in range(bt):
        for h in range(N_HEADS):
            hs = slice(h * HEAD_DIM, (h + 1) * HEAD_DIM)
            s_new = fT[hs, bi:bi + 1] * s0_ref[bi, h] + kT[hs, bi:bi + 1] * iv[bi:bi + 1, hs]
            s_ref[bi, h] = s_new
            o_s[bi:bi + 1, hs] = jnp.sum(qT[hs, bi:bi + 1] * s_new, axis=0, keepdims=True)
    o_ref[...] = _hgrn_finish(o_s[...], gg, hgn_ref[...], _head_block_ones(BRANCH_W).astype(BF16))


def _hgrn_step(hg, lower, hgn, s0, bt):
    m = hg.shape[0]
    w = BRANCH_W
    sblk = pl.BlockSpec((bt, N_HEADS, HEAD_DIM, HEAD_DIM), lambda i: (i, 0, 0, 0))
    return pl.pallas_call(
        functools.partial(_hgrn_step_kernel, bt=bt),
        out_shape=(jax.ShapeDtypeStruct((m, w), F32), jax.ShapeDtypeStruct(s0.shape, F32)),
        grid=(m // bt,),
        in_specs=[pl.BlockSpec((bt, 4 * w), lambda i: (i, 0)),
                  pl.BlockSpec((1, w), lambda i: (0, 0)), pl.BlockSpec((1, w), lambda i: (0, 0)), sblk],
        out_specs=(pl.BlockSpec((bt, w), lambda i: (i, 0)), sblk),
        scratch_shapes=[pltpu.VMEM((bt, w), F32)],
        compiler_params=_cp(("arbitrary",)),
        name="hgrn_step",
    )(hg, lower, hgn, s0)


def _merge_kernel(x_ref, o0_ref, o1_ref, o2_ref, o3_ref, mg_ref, gate_ref, gpost_ref, wb_ref, wo_ref, y_ref):
    d = x_ref.shape[1]
    mixed = None
    for n, o_ref in enumerate((o0_ref, o1_ref, o2_ref, o3_ref)):
        term = _sigmoid(mg_ref[:, n * d:(n + 1) * d]) * _dot(o_ref[...], wb_ref[n])
        mixed = term if mixed is None else mixed + term
    y = _dot(mixed, wo_ref[...])
    y_ref[...] = x_ref[...] + gate_ref[0] * _rms(y, gpost_ref[...])


def _merge(x, branches, merge_g, gate, gpost, wb, wo, tm, tps):
    m, d = x.shape
    row = lambda w: pl.BlockSpec((tm, w), lambda i: (i, 0))
    return pl.pallas_call(
        _merge_kernel,
        out_shape=jax.ShapeDtypeStruct((m, d), F32),
        grid=(m // tm,),
        in_specs=[row(d)] + [row(BRANCH_W)] * 4 + [row(N_BRANCH * d), _mod_spec(gate, tm, tps),
                  pl.BlockSpec((1, d), lambda i: (0, 0)),
                  pl.BlockSpec(wb.shape, lambda i: (0, 0, 0), pipeline_mode=pl.Buffered(1)),
                  pl.BlockSpec(wo.shape, lambda i: (0, 0), pipeline_mode=pl.Buffered(1))],
        out_specs=row(d),
        compiler_params=_cp(("arbitrary",)),
        name="merge",
    )(x, *branches, merge_g, gate, gpost, wb, wo)


def _gelu_tanh(x):
    return 0.5 * x * (1.0 + jnp.tanh(math.sqrt(2.0 / math.pi) * (x + 0.044715 * (x * x * x))))


def _ffn_kernel(*refs, tm, tps, nff, stepwise):
    if stepwise:
        (x_ref, gpre_ref, sh_ref, sc_ref, gate_ref, gpost_ref, wg_ref, wv_ref, cwg_ref, cwv_ref, cbg_ref, cbv_ref,
         wd_ref, p0g_ref, p0v_ref, p1g_ref, p1v_ref, y_ref, ug_ref, uv_ref, h_s, acc_s) = refs
    else:
        (x_ref, gpre_ref, sh_ref, sc_ref, gate_ref, gpost_ref, wg_ref, wv_ref, cwg_ref, cwv_ref, cbg_ref, cbv_ref,
         wd_ref, y_ref, csg_ref, csv_ref, h_s, acc_s, carry_g, carry_v) = refs
    i = pl.program_id(0)
    j = pl.program_id(1)

    @pl.when(j == 0)
    def _():
        h = _rms(x_ref[...], gpre_ref[...]) * (1.0 + sc_ref[0]) + sh_ref[0]
        h_s[...] = h.astype(BF16)
        acc_s[...] = jnp.zeros(acc_s.shape, F32)

    hb = h_s[...]
    ug = jnp.dot(hb, wg_ref[...], preferred_element_type=F32)
    uv = jnp.dot(hb, wv_ref[...], preferred_element_type=F32)

    if stepwise:
        def conv(u, cw_ref, cb_ref, p0_ref, p1_ref):
            return cb_ref[...] + p0_ref[...] * cw_ref[0:1, :] + p1_ref[...] * cw_ref[1:2, :] + u * cw_ref[2:3, :]
        cg = conv(ug, cwg_ref, cbg_ref, p0g_ref, p1g_ref)
        cv = conv(uv, cwv_ref, cbv_ref, p0v_ref, p1v_ref)
        ug_ref[...] = ug
        uv_ref[...] = uv
    else:
        first = (i % tps) == 0
        row = _iota((tm, 1), 0)

        @pl.when(i == 0)
        def _():
            carry_g[j] = jnp.zeros(carry_g.shape[1:], F32)
            carry_v[j] = jnp.zeros(carry_v.shape[1:], F32)

        def conv(u, cw_ref, cb_ref, carry):
            prev = jnp.where(first, 0.0, carry[j])
            um1 = jnp.where(row == 0, prev[1:2, :], pltpu.roll(u, 1, 0))
            um2 = jnp.where(row == 0, prev[0:1, :], jnp.where(row == 1, prev[1:2, :], pltpu.roll(u, 2, 0)))
            return cb_ref[...] + um2 * cw_ref[0:1, :] + um1 * cw_ref[1:2, :] + u * cw_ref[2:3, :]
        cg = conv(ug, cwg_ref, cbg_ref, carry_g)
        cv = conv(uv, cwv_ref, cbv_ref, carry_v)
        carry_g[j] = ug[tm - 2:tm, :]
        carry_v[j] = uv[tm - 2:tm, :]
        csg_ref[0, j] = ug[tm - 2:tm, :]
        csv_ref[0, j] = uv[tm - 2:tm, :]

    acc_s[...] += _dot(_gelu_tanh(cg) * cv, wd_ref[...])

    @pl.when(j == nff - 1)
    def _():
        y_ref[...] = x_ref[...] + gate_ref[0] * _rms(acc_s[...], gpost_ref[...])


def _ffn(x, gpre, shift, scale, gate, gpost, w_up, conv_w, conv_b, w_down, tm, tps, prev=None):
    m, d = x.shape
    dff = w_down.shape[0]
    nff = 2 if dff % (2 * LANES) == 0 else 1
    fc = dff // nff
    stepwise = prev is not None
    vec = lambda: pl.BlockSpec((1, d), lambda i, j: (0, 0))
    colg = lambda r: pl.BlockSpec((r, fc), lambda i, j: (0, j))
    colv = lambda r: pl.BlockSpec((r, fc), lambda i, j: (0, nff + j))
    in_specs = [pl.BlockSpec((tm, d), lambda i, j: (i, 0)), vec(),
                _mod_spec(shift, tm, tps), _mod_spec(scale, tm, tps), _mod_spec(gate, tm, tps), vec(),
                colg(d), colv(d), colg(CONV_W), colv(CONV_W), colg(1), colv(1),
                pl.BlockSpec((fc, d), lambda i, j: (j, 0))]
    args = [x, gpre, shift, scale, gate, gpost, w_up, w_up, conv_w, conv_w, conv_b, conv_b, w_down]
    scratch = [pltpu.VMEM((tm, d), BF16), pltpu.VMEM((tm, d), F32)]
    if stepwise:
        p0, p1 = prev
        in_specs += [pl.BlockSpec((tm, fc), lambda i, j: (i, j)), pl.BlockSpec((tm, fc), lambda i, j: (i, nff + j))] * 2
        args += [p0, p0, p1, p1]
        out_shape = (jax.ShapeDtypeStruct((m, d), F32), jax.ShapeDtypeStruct((m, dff), F32), jax.ShapeDtypeStruct((m, dff), F32))
        out_specs = (pl.BlockSpec((tm, d), lambda i, j: (i, 0)),
                     pl.BlockSpec((tm, fc), lambda i, j: (i, j)), pl.BlockSpec((tm, fc), lambda i, j: (i, j)))
    else:
        nseq = m // (tm * tps)
        out_shape = (jax.ShapeDtypeStruct((m, d), F32),
                     jax.ShapeDtypeStruct((nseq, nff, CONV_W - 1, fc), F32),
                     jax.ShapeDtypeStruct((nseq, nff, CONV_W - 1, fc), F32))
        cs = pl.BlockSpec((1, nff, CONV_W - 1, fc), lambda i, j: (i // tps, 0, 0, 0))
        out_specs = (pl.BlockSpec((tm, d), lambda i, j: (i, 0)), cs, cs)
        scratch += [pltpu.VMEM((nff, CONV_W - 1, fc), F32)] * 2
    return pl.pallas_call(
        functools.partial(_ffn_kernel, tm=tm, tps=tps, nff=nff, stepwise=stepwise),
        out_shape=out_shape,
        grid=(m // tm, nff),
        in_specs=in_specs,
        out_specs=out_specs,
        scratch_shapes=scratch,
        compiler_params=_cp(("arbitrary", "arbitrary")),
        name="ffn_step" if stepwise else "ffn_seq",
    )(*args)


def _page_specs(layer, pg, rows, rowblk):
    def spec(k):
        return pl.BlockSpec((None, None, rows, PAGE_SIZE), lambda b, j, pt, *_: (layer, pt[b, j * pg + k], rowblk, 0))
    return [spec(k) for k in range(pg)]


def _softmax_with_self(lg, valid, lg_self):
    lg = jnp.where(valid, lg, -jnp.inf)
    m = jnp.maximum(jnp.max(lg, axis=-1, keepdims=True), lg_self)
    p = jnp.exp(lg - m)
    p_self = jnp.exp(lg_self - m)
    den = jnp.sum(p, axis=-1, keepdims=True) + p_self
    return p / den, p_self / den


def _self_logit(q8, k_row):
    qf = q8.astype(BF16).astype(F32)
    kf = k_row.astype(BF16).astype(F32)
    return jnp.sum(qf * kf, axis=-1, keepdims=True)


def _rows8(x, w):
    return jnp.concatenate([x[:, h * w:(h + 1) * w] for h in range(N_HEADS)] + [jnp.zeros((8 - N_HEADS, w), x.dtype)], axis=0)


def _col8(x):
    r = _iota((8, 1), 0)
    out = jnp.zeros((8, 1), F32)
    for h in range(N_HEADS):
        out = jnp.where(r == h, x[:, h:h + 1], out)
    return out


def _bias8(d, tab_ref, head0):
    bs = _rel_bias(d, tab_ref, range(head0, head0 + N_HEADS))
    r = _iota((8, d.shape[1]), 0)
    out = jnp.zeros((8, d.shape[1]), F32)
    for h in range(N_HEADS):
        out = jnp.where(r == h, bs[h], out)
    return out


def _softmax_rows(lg, valid):
    lg = jnp.where(valid, lg, -jnp.inf)
    mx = jnp.max(lg, axis=-1, keepdims=True)
    mx = jnp.where(mx == -jnp.inf, 0.0, mx)
    p = jnp.exp(lg - mx)
    return p / jnp.maximum(jnp.sum(p, axis=-1, keepdims=True), 1e-30)


def _write_heads(o_ref, o8, w):
    for h in range(N_HEADS):
        o_ref[0, :, h * w:(h + 1) * w] = o8[h:h + 1, :]


def _nsa_cmp_step_kernel(pt_ref, tab_ref, q_ref, *rest, pg, past, n_pick):
    pages, (o_ref, idx_ref, cmp_s) = rest[:pg], rest[pg:]
    j = pl.program_id(1)
    nsel = past // NSA_SEL_BLOCK
    per_page = PAGE_SIZE // NSA_CMP_BLOCK
    nb = per_page * pg
    half = nb // 2
    nsteps = cmp_s.shape[0]
    key = _iota((PAGE_SIZE, nb), 0)
    col = _iota((PAGE_SIZE, nb), 1)
    acc = jnp.zeros((2 * HEAD_DIM, nb), F32)
    for k in range(pg):
        g = per_page * k + key // NSA_CMP_BLOCK
        pool = jnp.where(col == (g % 2) * half + g // 2, 1.0 / NSA_CMP_BLOCK, 0.0).astype(BF16)
        acc = acc + jnp.dot(pages[k][...].astype(BF16), pool, preferred_element_type=F32)
    cmp_s[j] = acc

    @pl.when(j == pl.num_programs(1) - 1)
    def _():
        q8 = _rows8(q_ref[0], HEAD_DIM)
        cc = _iota((1, nb), 1)
        ps_all, o_cmp = [], None
        lgs = []
        for s in range(nsteps):
            sel = s * half + jnp.where(cc < half, cc, cc - half)
            c_end = NSA_SEL_BLOCK * sel + jnp.where(cc < half, NSA_CMP_BLOCK - 1, NSA_SEL_BLOCK - 1)
            lgs.append(_dot(q8, cmp_s[s, 0:HEAD_DIM, :]) * HEAD_DIM ** -0.5 + _bias8(past - c_end, tab_ref, 0))
        lg = jnp.concatenate(lgs, axis=-1)
        p = _softmax_rows(lg, True)
        for s in range(nsteps):
            term = _dot_nt(p[:, s * nb:(s + 1) * nb], cmp_s[s, HEAD_DIM:2 * HEAD_DIM, :])
            o_cmp = term if o_cmp is None else o_cmp + term
            ps = p[0:1, s * nb:(s + 1) * nb] + p[1:2, s * nb:(s + 1) * nb] + p[2:3, s * nb:(s + 1) * nb] + p[3:4, s * nb:(s + 1) * nb]
            ps_all.append(ps[:, 0:half] + ps[:, half:nb])
        _write_heads(o_ref, o_cmp, HEAD_DIM)
        imp = jnp.concatenate(ps_all, axis=-1)
        blk = _iota((1, nsel), 1)
        imp = jnp.where(blk == 0, -jnp.inf, imp)
        lane = _iota((1, LANES), 1)
        idx = jnp.where(lane == n_pick + 1, nsel, 0)
        for s in range(n_pick):
            top = jnp.max(imp, axis=-1, keepdims=True)
            first = jnp.min(jnp.where(imp == top, blk, nsel), axis=-1, keepdims=True)
            idx = jnp.where(lane == s + 1, first, idx)
            imp = jnp.where(blk == first, -jnp.inf, imp)
        idx_ref[0] = idx


def _nsa_cmp_step(page_table, rel_table, q3, cache_t, layer, pg, past):
    bsz, n_pages = page_table.shape
    nsel = past // NSA_SEL_BLOCK
    n_pick = min(NSA_TOP_N, nsel + 1) - 2
    nb = pg * (PAGE_SIZE // NSA_CMP_BLOCK)
    grid_spec = pltpu.PrefetchScalarGridSpec(
        num_scalar_prefetch=1,
        grid=(bsz, n_pages // pg),
        in_specs=[pl.BlockSpec(memory_space=pltpu.SMEM),
                  pl.BlockSpec((1, 1, BRANCH_W), lambda b, j, pt: (b, 0, 0))]
                 + _page_specs(layer, pg, 2 * HEAD_DIM, 0),
        out_specs=(pl.BlockSpec((1, 1, BRANCH_W), lambda b, j, pt: (b, 0, 0)),
                   pl.BlockSpec((1, 1, LANES), lambda b, j, pt: (b, 0, 0))),
        scratch_shapes=[pltpu.VMEM((n_pages // pg, 2 * HEAD_DIM, nb), F32)])
    return pl.pallas_call(
        functools.partial(_nsa_cmp_step_kernel, pg=pg, past=past, n_pick=n_pick),
        out_shape=(jax.ShapeDtypeStruct((bsz, 1, BRANCH_W), F32), jax.ShapeDtypeStruct((bsz, 1, LANES), I32)),
        grid_spec=grid_spec,
        compiler_params=_cp(("arbitrary", "arbitrary")),
        name="nsa_cmp_step",
    )(page_table, rel_table, q3, *([cache_t] * pg))


def _nsa_sel_step_kernel(idx_ref, pt_ref, tab_ref, q_ref, sm_ref, kv_ref, nw_ref, ocmp_ref, win_ref, *rest,
                         n_past, past):
    blocks, (o_ref,) = rest[:n_past], rest[n_past:]
    b = pl.program_id(0)
    sb = NSA_SEL_BLOCK
    per_page = PAGE_SIZE // sb
    scale = HEAD_DIM ** -0.5
    q8 = _rows8(q_ref[0], HEAD_DIM)
    zero_d = jnp.zeros((1, 1), I32)

    row = _iota((1, PAGE_SIZE), 1)
    lgs, valids = [], []
    for k in range(n_past):
        blk = idx_ref[b, k]
        pos = (blk // per_page) * PAGE_SIZE + row
        lgs.append(_dot(q8, blocks[k][0:HEAD_DIM, :]) * scale + _bias8(past - pos, tab_ref, 0))
        valids.append(row // sb == blk % per_page)
    new_kv = kv_ref[0]
    lg_self = _self_logit(q8, new_kv[:, 2 * HEAD_DIM:3 * HEAD_DIM]) * scale + _bias8(zero_d, tab_ref, 0)
    p, p_self = _softmax_with_self(jnp.concatenate(lgs, axis=-1), jnp.concatenate(valids, axis=-1), lg_self)
    o_sel = p_self * new_kv[:, 3 * HEAD_DIM:4 * HEAD_DIM]
    for k in range(n_past):
        o_sel = o_sel + _dot_nt(p[:, k * PAGE_SIZE:(k + 1) * PAGE_SIZE], blocks[k][HEAD_DIM:2 * HEAD_DIM, :])

    wb = win_ref.shape[1]
    d = wb - _iota((1, wb), 1)
    lg = _dot(q8, win_ref[0:HEAD_DIM, :]) * scale + _bias8(d, tab_ref, 0)
    new_win = nw_ref[0]
    lg_self = _self_logit(q8, new_win[:, 0:HEAD_DIM]) * scale + _bias8(zero_d, tab_ref, 0)
    p, p_self = _softmax_with_self(lg, (d <= NSA_WINDOW) & (past - d >= 0), lg_self)
    o_win = _dot_nt(p, win_ref[HEAD_DIM:2 * HEAD_DIM, :]) + p_self * new_win[:, HEAD_DIM:2 * HEAD_DIM]

    g = _sigmoid(sm_ref[0][:, SM_NSAG:SM_NSAG + 3 * N_HEADS])
    for h in range(N_HEADS):
        hs = slice(h * HEAD_DIM, (h + 1) * HEAD_DIM)
        o_ref[0, :, hs] = (g[:, 3 * h:3 * h + 1] * ocmp_ref[0][:, hs] + g[:, 3 * h + 1:3 * h + 2] * o_sel[h:h + 1, :]
                           + g[:, 3 * h + 2:3 * h + 3] * o_win[h:h + 1, :])


def _nsa_sel_step(idx, page_table, rel_table, q3, sm3, kv3, nw3, ocmp3, win_t, cache_t, layer, past):
    bsz = page_table.shape[0]
    nsel = past // NSA_SEL_BLOCK
    n_past = min(NSA_TOP_N, nsel + 1) - 1
    per_page = PAGE_SIZE // NSA_SEL_BLOCK
    wb = win_t.shape[3]
    tok = lambda w: pl.BlockSpec((1, 1, w), lambda b, idx, pt: (b, 0, 0))

    def blk_spec(k):
        return pl.BlockSpec((None, None, 2 * HEAD_DIM, PAGE_SIZE),
                            lambda b, idx, pt: (layer, pt[b, idx[b, k] // per_page], 1, 0))
    grid_spec = pltpu.PrefetchScalarGridSpec(
        num_scalar_prefetch=2,
        grid=(bsz,),
        in_specs=[pl.BlockSpec(memory_space=pltpu.SMEM), tok(BRANCH_W), tok(LANES), tok(4 * HEAD_DIM), tok(2 * HEAD_DIM),
                  tok(BRANCH_W),
                  pl.BlockSpec((None, None, 2 * HEAD_DIM, wb), lambda b, idx, pt: (layer, b, 0, 0))]
                 + [blk_spec(k) for k in range(n_past)],
        out_specs=tok(BRANCH_W))
    return pl.pallas_call(
        functools.partial(_nsa_sel_step_kernel, n_past=n_past, past=past),
        out_shape=jax.ShapeDtypeStruct((bsz, 1, BRANCH_W), F32),
        grid_spec=grid_spec,
        compiler_params=_cp(("arbitrary",)),
        name="nsa_sel_step",
    )(idx, page_table, rel_table, q3, sm3, kv3, nw3, ocmp3, win_t, *([cache_t] * n_past))


def _index_weights(sm):
    return _col8(sm[:, SM_IDXW:SM_IDXW + IDX_HEADS])


def _dsa_score_step_kernel(pt_ref, iq_ref, sm_ref, *rest, pg):
    pages, (sc_ref, new_ref) = rest[:pg], rest[pg:]
    qi8 = _rows8(iq_ref[0], IDX_DIM)
    sm = sm_ref[0]
    wcol = _index_weights(sm)
    cst = IDX_DIM ** -0.5 * IDX_HEADS ** -0.5
    rows = []
    for k in range(pg):
        s = jnp.maximum(_dot(qi8, pages[k][...]), 0.0)
        rows.append(jnp.sum(s * wcol, axis=0, keepdims=True) * cst)
    sc_ref[0] = jnp.concatenate(rows, axis=0)
    s_new = jnp.maximum(_self_logit(qi8, sm[:, SM_IDXK:SM_IDXK + IDX_DIM]), 0.0)
    sc_new = jnp.sum(s_new * wcol, axis=0, keepdims=True) * cst
    new_ref[0] = jnp.where(_iota((1, LANES), 1) == 0, sc_new, -jnp.inf)


def _dsa_score_step(page_table, iq3, sm3, cache_idx_t, layer, pg):
    bsz, n_pages = page_table.shape
    tok = lambda w: pl.BlockSpec((1, 1, w), lambda b, j, pt: (b, 0, 0))
    grid_spec = pltpu.PrefetchScalarGridSpec(
        num_scalar_prefetch=1,
        grid=(bsz, n_pages // pg),
        in_specs=[tok(IDX_HEADS * IDX_DIM), tok(LANES)] + _page_specs(layer, pg, IDX_DIM, 0),
        out_specs=(pl.BlockSpec((1, pg, PAGE_SIZE), lambda b, j, pt: (b, j, 0)), tok(LANES)))
    return pl.pallas_call(
        functools.partial(_dsa_score_step_kernel, pg=pg),
        out_shape=(jax.ShapeDtypeStruct((bsz, n_pages, PAGE_SIZE), F32), jax.ShapeDtypeStruct((bsz, 1, LANES), F32)),
        grid_spec=grid_spec,
        compiler_params=_cp(("arbitrary", "arbitrary")),
        name="dsa_score_step",
    )(page_table, iq3, sm3, *([cache_idx_t] * pg))


def _dsa_thr_step_kernel(sc_ref, new_ref, thr_ref, need_ref, tie_ref, *, topk):
    key = _ordered_key(sc_ref[...])
    key_new = _ordered_key(new_ref[...])
    bt = key.shape[0]

    def count(pred):
        return (jnp.sum(jnp.where(pred(key), 1, 0), axis=-1, keepdims=True)
                + jnp.sum(jnp.where(pred(key_new), 1, 0), axis=-1, keepdims=True))

    thr = _kth_largest_key(lambda cand: count(lambda x: x >= cand), topk, (bt, 1))
    need = topk - count(lambda x: x > thr)
    thr_ref[...] = jnp.broadcast_to(thr, thr_ref.shape)
    need_ref[...] = jnp.broadcast_to(need, need_ref.shape)
    tie_ref[...] = jnp.broadcast_to(jnp.where(count(lambda x: x == thr) > need, 1, 0), tie_ref.shape)


def _dsa_thr_step(scores, new, topk, bt):
    bsz, p = scores.shape
    out = pl.BlockSpec((bt, LANES), lambda i: (i, 0))
    return pl.pallas_call(
        functools.partial(_dsa_thr_step_kernel, topk=topk),
        out_shape=(jax.ShapeDtypeStruct((bsz, LANES), I32),) * 3,
        grid=(bsz // bt,),
        in_specs=[pl.BlockSpec((bt, p), lambda i: (i, 0)), pl.BlockSpec((bt, LANES), lambda i: (i, 0))],
        out_specs=(out, out, out),
        compiler_params=_cp(("arbitrary",)),
        name="dsa_thr_step",
    )(scores, new)


def _online_self(lg_self, v_row, m_s, l_s, acc_s):
    m_old = m_s[...]
    m_new = jnp.maximum(m_old, lg_self)
    m_safe = jnp.where(m_new == -jnp.inf, 0.0, m_new)
    alpha = jnp.exp(m_old - m_safe)
    p_self = jnp.exp(lg_self - m_safe)
    den = alpha * l_s[...] + p_self
    return (alpha * acc_s[...] + p_self * v_row) / jnp.maximum(den, 1e-30)


def _dsa_att_step_kernel(pt_ref, tie_ref, tab_ref, q_ref, kv_ref, sc_ref, new_ref, thr_ref, need_ref, *rest, pg, past):
    pages, (o_ref, m_s, l_s, acc_s, run_s) = rest[:pg], rest[pg:]
    b = pl.program_id(0)
    j = pl.program_id(1)
    scale = HEAD_DIM ** -0.5
    nk = pg * PAGE_SIZE

    @pl.when(j == 0)
    def _():
        _flash_init(m_s, l_s, acc_s)
        run_s[...] = jnp.zeros(run_s.shape, F32)

    q8 = _rows8(q_ref[0], HEAD_DIM)
    thr = thr_ref[0][:, 0:1]
    need = need_ref[0][:, 0:1].astype(F32)
    has_ties = tie_ref[b] > 0
    key = _ordered_key(sc_ref[0])

    def pick_plain(run):
        return jnp.where(key >= thr, 1.0, 0.0), run

    def pick_ties(run):
        eq = key == thr
        eqf = jnp.where(eq, 1.0, 0.0)
        strict_upper = (_iota((PAGE_SIZE, PAGE_SIZE), 0) < _iota((PAGE_SIZE, PAGE_SIZE), 1)).astype(BF16)
        inside = jnp.dot(eqf.astype(BF16), strict_upper, preferred_element_type=F32)
        cnt = jnp.sum(eqf, axis=-1, keepdims=True)
        rows = []
        for k in range(pg):
            rows.append(inside[k:k + 1] + run)
            run = run + cnt[k:k + 1]
        before = jnp.concatenate(rows, axis=0)
        return jnp.where((key > thr) | (eq & (before < need)), 1.0, 0.0), run

    chosen, run = lax.cond(has_ties, pick_ties, pick_plain, run_s[...])
    run_s[...] = run

    lg = jnp.concatenate([_dot(q8, pages[k][0:HEAD_DIM, :]) for k in range(pg)], axis=-1) * scale
    mask = jnp.concatenate([chosen[k:k + 1] for k in range(pg)], axis=-1) > 0.5
    r8 = _iota((8, 1), 0)
    far = jnp.zeros((8, 1), F32)
    for h in range(N_HEADS):
        far = jnp.where(r8 == h, tab_ref[REL_BUCKETS - 1, N_HEADS + h], far)
    d0 = past - j * nk
    bias = lax.cond(d0 - (nk - 1) >= REL_MAX_DIST,
                    lambda: jnp.broadcast_to(far, (8, nk)),
                    lambda: _bias8(d0 - _iota((1, nk), 1), tab_ref, N_HEADS))
    p, alpha = _softmax_step(lg + bias, mask, m_s, l_s)
    acc = alpha * acc_s[...]
    for k in range(pg):
        acc = acc + _dot_nt(p[:, k * PAGE_SIZE:(k + 1) * PAGE_SIZE], pages[k][HEAD_DIM:2 * HEAD_DIM, :])
    acc_s[...] = acc

    @pl.when(j == pl.num_programs(1) - 1)
    def _():
        key_new = _ordered_key(new_ref[0][:, 0:1])
        limit = jnp.where(has_ties, need, jnp.float32(3.0e38))
        take = (key_new > thr) | ((key_new == thr) & (run < limit))
        new_kv = kv_ref[0]
        lg_self = _self_logit(q8, new_kv[:, 0:HEAD_DIM]) * scale + _bias8(jnp.zeros((1, 1), I32), tab_ref, N_HEADS)
        lg_self = jnp.where(take, lg_self, -jnp.inf)
        _write_heads(o_ref, _online_self(lg_self, new_kv[:, HEAD_DIM:2 * HEAD_DIM], m_s, l_s, acc_s), HEAD_DIM)


def _dsa_att_step(page_table, tie, rel_table, q3, kv3, scores, new, thr, need, cache_t, layer, pg, past):
    bsz, n_pages = page_table.shape
    tok = lambda w: pl.BlockSpec((1, 1, w), lambda b, j, pt, tie: (b, 0, 0))
    grid_spec = pltpu.PrefetchScalarGridSpec(
        num_scalar_prefetch=2,
        grid=(bsz, n_pages // pg),
        in_specs=[pl.BlockSpec(memory_space=pltpu.SMEM), tok(BRANCH_W), tok(2 * HEAD_DIM),
                  pl.BlockSpec((1, pg, PAGE_SIZE), lambda b, j, pt, tie: (b, j, 0)), tok(LANES), tok(LANES), tok(LANES)]
                 + _page_specs(layer, pg, 2 * HEAD_DIM, 0),
        out_specs=tok(BRANCH_W),
        scratch_shapes=[pltpu.VMEM((8, 1), F32), pltpu.VMEM((8, 1), F32), pltpu.VMEM((8, HEAD_DIM), F32),
                        pltpu.VMEM((1, 1), F32)])
    return pl.pallas_call(
        functools.partial(_dsa_att_step_kernel, pg=pg, past=past),
        out_shape=jax.ShapeDtypeStruct((bsz, 1, BRANCH_W), F32),
        grid_spec=grid_spec,
        compiler_params=_cp(("arbitrary", "arbitrary")),
        name="dsa_att_step",
    )(page_table, tie, rel_table, q3, kv3, scores, new, thr, need, *([cache_t] * pg))


def _mla_step_kernel(pt_ref, qc_ref, kc_ref, wuv_ref, *rest, pg):
    pages, (o_ref, m_s, l_s, acc_s) = rest[:pg], rest[pg:]
    j = pl.program_id(1)
    scale = MLA_HEAD_QK ** -0.5
    kw = KV_LORA + QK_ROPE

    @pl.when(j == 0)
    def _():
        _flash_init(m_s, l_s, acc_s)

    q8 = _rows8(qc_ref[0], MLA_QW)[:, 0:kw]
    lg = jnp.concatenate([_dot(q8, pages[k][...]) for k in range(pg)], axis=-1) * scale
    p, alpha = _softmax_step(lg, True, m_s, l_s)
    acc = alpha * acc_s[...]
    for k in range(pg):
        acc = acc + _dot_nt(p[:, k * PAGE_SIZE:(k + 1) * PAGE_SIZE], pages[k][0:KV_LORA, :])
    acc_s[...] = acc

    @pl.when(j == pl.num_programs(1) - 1)
    def _():
        new_k = kc_ref[0].astype(F32)
        lg_self = _self_logit(q8, new_k[:, 0:kw]) * scale
        o_lat = _online_self(lg_self, new_k[:, 0:KV_LORA], m_s, l_s, acc_s)
        for h in range(N_HEADS):
            o_ref[0, :, h * V_DIM:(h + 1) * V_DIM] = _dot(o_lat, wuv_ref[h])[h:h + 1, :]


def _mla_step(page_table, qc3, kc3, wuv, cache_mla, layer, pg):
    bsz, n_pages = page_table.shape
    tok = lambda w: pl.BlockSpec((1, 1, w), lambda b, j, pt: (b, 0, 0))
    grid_spec = pltpu.PrefetchScalarGridSpec(
        num_scalar_prefetch=1,
        grid=(bsz, n_pages // pg),
        in_specs=[tok(N_HEADS * MLA_QW), tok(MLA_QW), pl.BlockSpec(wuv.shape, lambda b, j, pt: (0, 0, 0))]
                 + _page_specs(layer, pg, KV_LORA + QK_ROPE, 0),
        out_specs=tok(BRANCH_W),
        scratch_shapes=[pltpu.VMEM((8, 1), F32), pltpu.VMEM((8, 1), F32), pltpu.VMEM((8, KV_LORA), F32)])
    return pl.pallas_call(
        functools.partial(_mla_step_kernel, pg=pg),
        out_shape=jax.ShapeDtypeStruct((bsz, 1, BRANCH_W), F32),
        grid_spec=grid_spec,
        compiler_params=_cp(("arbitrary", "arbitrary")),
        name="mla_step",
    )(page_table, qc3, kc3, wuv, *([cache_mla] * pg))


def _permute_w_in(w_in):
    d = w_in.shape[0]
    o = {}
    off = 0
    for name, w in (('nsa_q', 256), ('nsa_kv', 384), ('nsa_g', 12), ('dsa_q', 256), ('dsa_kv', 128), ('idx_q', 256),
                    ('idx_k', 64), ('idx_w', 4), ('hg', 1024), ('mla_cq', 256), ('mla_ckv', 128), ('mla_kr', 32),
                    ('merge_g', w_in.shape[1] - 2800)):
        o[name] = w_in[:, off:off + w]
        off += w
    pad = jnp.zeros((d, LANES - 112), w_in.dtype)
    cols = [o['nsa_q'], o['nsa_kv'], o['dsa_q'], o['dsa_kv'], o['idx_q'], o['hg'], o['mla_cq'], o['mla_ckv'],
            o['idx_k'], o['mla_kr'], o['nsa_g'], o['idx_w'], pad, o['merge_g']]
    return jnp.concatenate(cols, axis=1).astype(BF16)


def _layer_weights(l, w):
    hq = MLA_HEAD_QK
    wuq = w['w_uq'][l].reshape(Q_LORA, N_HEADS, hq)
    hr = QK_ROPE // 2
    return dict(
        w_in=_permute_w_in(w['w_in'][l]),
        wn=wuq[:, :, :QK_NOPE].reshape(Q_LORA, N_HEADS * QK_NOPE).astype(BF16),
        wr1=wuq[:, :, QK_NOPE:QK_NOPE + hr].reshape(Q_LORA, N_HEADS * hr).astype(BF16),
        wr2=wuq[:, :, QK_NOPE + hr:].reshape(Q_LORA, N_HEADS * hr).astype(BF16),
        wukT=jnp.transpose(w['w_uk'][l], (1, 2, 0)).astype(BF16),
        wuv=jnp.transpose(w['w_uv'][l], (1, 0, 2)).astype(BF16),
        wb=w['w_branch'][l].astype(BF16),
        wo=w['w_out'][l].astype(BF16),
        w_up=w['w_up'][l].astype(BF16),
        w_down=w['w_down'][l].astype(BF16),
        conv_w=w['conv_w'][l],
        conv_b=w['conv_b'][l][None, :],
        g_pre_mix=w['g_pre_mix'][l][None, :], g_post_mix=w['g_post_mix'][l][None, :],
        g_pre_ffn=w['g_pre_ffn'][l][None, :], g_post_ffn=w['g_post_ffn'][l][None, :],
        gq=w['mla_q_norm'][l][None, :], gkv=w['mla_kv_norm'][l][None, :],
        hgn=jnp.tile(w['hg_norm'][l], N_HEADS)[None, :],
    )


def _largest_divisor(n, cap):
    for c in range(min(n, cap), 0, -1):
        if n % c == 0:
            return c
    return 1


def kernel(x_prompt, x_sample, cache_nsa_kv, cache_dsa_kv, cache_dsa_idx, cache_mla, state_nsa_win, state_hgrn, state_ffn_conv, page_table, c_prompt, c_sample, rel_table, w_ada, b_ada, g_pre_mix, g_post_mix, g_pre_ffn, g_post_ffn, w_in, hg_lb, hg_norm, mla_q_norm, mla_kv_norm, w_uq, w_uk, w_uv, w_branch, w_out, w_up, conv_w, conv_b, w_down):
    weights = dict(w_in=w_in, w_uq=w_uq, w_uk=w_uk, w_uv=w_uv, w_branch=w_branch, w_out=w_out, w_up=w_up,
                   w_down=w_down, conv_w=conv_w, conv_b=conv_b, g_pre_mix=g_pre_mix, g_post_mix=g_post_mix,
                   g_pre_ffn=g_pre_ffn, g_post_ffn=g_post_ffn, mla_q_norm=mla_q_norm, mla_kv_norm=mla_kv_norm,
                   hg_norm=hg_norm)
    depth = w_in.shape[0]
    batch, seq, d = x_prompt.shape
    dec, dec_seq, _ = x_sample.shape
    assert dec_seq == 1 and seq % 256 == 0 and dec % 8 == 0
    n_pool = cache_nsa_kv.shape[1]
    n_pages = page_table.shape[1]
    past = n_pages * PAGE_SIZE
    dff = w_down.shape[1]

    tq = 256
    tm_p = 256
    tps_p = seq // tm_p
    tm_s = _largest_divisor(dec, 128)
    pg = _largest_divisor(n_pages, 16)
    pg_cmp = _largest_divisor(n_pages, 32)

    gam = jax.nn.softmax(hg_lb.astype(F32), axis=0)
    cum = jnp.cumsum(gam, axis=0)
    lower = cum - cum[0]

    inv = ROPE_THETA ** (-jnp.arange(0, QK_ROPE, 2, dtype=F32) / QK_ROPE)
    inv4 = jnp.tile(inv, N_HEADS)[None, :]

    nsel = seq // NSA_SEL_BLOCK
    jj = jnp.arange(2 * nsel, dtype=I32)
    cend = jnp.where(jj < nsel, NSA_SEL_BLOCK * jj + (NSA_CMP_BLOCK - 1),
                     NSA_SEL_BLOCK * (jj - nsel) + (NSA_SEL_BLOCK - 1))[None, :]
    bias_t, bias_cmp = _bias_tiles(rel_table, cend, tq, tq, seq)

    mod = _ada(jnp.concatenate([c_prompt, c_sample], axis=0), w_ada, b_ada)

    nsa_t = jnp.transpose(cache_nsa_kv, (0, 1, 3, 4, 2)).reshape(depth, n_pool, 4 * HEAD_DIM, PAGE_SIZE)
    dsa_t = jnp.transpose(cache_dsa_kv, (0, 1, 3, 4, 2)).reshape(depth, n_pool, 2 * HEAD_DIM, PAGE_SIZE)
    idx_t = jnp.transpose(cache_dsa_idx, (0, 1, 3, 2))
    mla_t = jnp.transpose(cache_mla, (0, 1, 3, 2))
    win_t = jnp.transpose(state_nsa_win, (0, 1, 3, 4, 2)).reshape(depth, dec, 2 * HEAD_DIM, state_nsa_win.shape[2])

    xp = x_prompt.reshape(batch * seq, d)
    xs = x_sample.reshape(dec, d)
    outs_p, outs_s = [], []
    for l in range(depth):
        lw = _layer_weights(l, weights)
        lower_l = lower[l][None, :]
        mp = [mod[l, :batch, k * d:(k + 1) * d][:, None, :] for k in range(6)]
        ms = [mod[l, batch:, k * d:(k + 1) * d][None, :, :] for k in range(6)]

        z = dict(zip(OUT_NAMES, _inproj(xp, lw['g_pre_mix'], mp[0], mp[1], lw['w_in'], tm_p, tps_p)))
        qc, kc, mla_st = _mla_proj(z['mla_cq'], z['mla_ckv'], z['small'], inv4, lw['gq'], lw['gkv'],
                                   lw['wn'], lw['wr1'], lw['wr2'], lw['wukT'], tm_p, seq, 0)
        o_nsa = _nsa_prompt(z['nsa_q'], z['small'], z['nsa_kv4'], z['nsa_win'], bias_t, bias_cmp, batch, seq, tq)
        o_dsa = _dsa_prompt(z['dsa_q'], z['idx_q'], z['small'], z['dsa_kv'], bias_t, batch, seq, tq)
        o_mla = _mla_prompt(qc, kc, lw['wuv'], batch, seq, tq)
        o_hg, s_new = _hgrn_prompt(z['hg'], lower_l, lw['hgn'], batch, seq, 256)
        xp = _merge(xp, (o_nsa, o_dsa, o_hg, o_mla), z['merge_g'], mp[2], lw['g_post_mix'], lw['wb'], lw['wo'], tm_p, tps_p)
        xp, csg, csv = _ffn(xp, lw['g_pre_ffn'], mp[3], mp[4], mp[5], lw['g_post_ffn'], lw['w_up'], lw['conv_w'],
                            lw['conv_b'], lw['w_down'], tm_p, tps_p)
        wl = min(NSA_WINDOW, seq)
        outs_p.append((z['nsa_kv4'].reshape(batch, seq, 4, HEAD_DIM),
                       z['dsa_kv'].reshape(batch, seq, 2, HEAD_DIM),
                       z['small'][:, SM_IDXK:SM_IDXK + IDX_DIM].reshape(batch, seq, IDX_DIM),
                       mla_st.reshape(batch, seq, KV_LORA + QK_ROPE),
                       z['nsa_win'].reshape(batch, seq, 2, HEAD_DIM)[:, seq - wl:],
                       s_new,
                       jnp.concatenate([jnp.swapaxes(csg, 1, 2).reshape(batch, CONV_W - 1, dff),
                                        jnp.swapaxes(csv, 1, 2).reshape(batch, CONV_W - 1, dff)], axis=-1)))

        z = dict(zip(OUT_NAMES, _inproj(xs, lw['g_pre_mix'], ms[0], ms[1], lw['w_in'], tm_s, 1)))
        qc, kc, mla_st = _mla_proj(z['mla_cq'], z['mla_ckv'], z['small'], inv4, lw['gq'], lw['gkv'],
                                   lw['wn'], lw['wr1'], lw['wr2'], lw['wukT'], tm_s, 1, past)
        r3 = lambda a: a.reshape(dec, 1, a.shape[-1])
        q3, sm3, kv3, nw3 = r3(z['nsa_q']), r3(z['small']), r3(z['nsa_kv4']), r3(z['nsa_win'])
        o_cmp, sel_idx = _nsa_cmp_step(page_table, rel_table, q3, nsa_t, l, pg_cmp, past)
        o_nsa = _nsa_sel_step(sel_idx.reshape(dec, LANES), page_table, rel_table, q3, sm3, kv3, nw3, o_cmp, win_t,
                              nsa_t, l, past)
        scores, sc_new = _dsa_score_step(page_table, r3(z['idx_q']), sm3, idx_t, l, pg)
        thr, need, tie = _dsa_thr_step(scores.reshape(dec, past), sc_new.reshape(dec, LANES),
                                       min(DSA_TOPK_MAX, (past + 1) // 4), 8)
        o_dsa = _dsa_att_step(page_table, tie[:, 0], rel_table, r3(z['dsa_q']), r3(z['dsa_kv']), scores, sc_new,
                              r3(thr), r3(need), dsa_t, l, pg, past)
        o_mla = _mla_step(page_table, r3(qc), r3(kc), lw['wuv'], mla_t, l, pg)
        o_hg, s_new = _hgrn_step(z['hg'], lower_l, lw['hgn'], state_hgrn[l], 8)
        xs = _merge(xs, (o_nsa.reshape(dec, BRANCH_W), o_dsa.reshape(dec, BRANCH_W), o_hg, o_mla.reshape(dec, BRANCH_W)),
                    z['merge_g'], ms[2], lw['g_post_mix'], lw['wb'], lw['wo'], tm_s, 1)
        prev = state_ffn_conv[l]
        xs, ug, uv = _ffn(xs, lw['g_pre_ffn'], ms[3], ms[4], ms[5], lw['g_post_ffn'], lw['w_up'], lw['conv_w'],
                          lw['conv_b'], lw['w_down'], tm_s, 1, prev=(prev[:, 0], prev[:, 1]))
        win_all = jnp.concatenate([state_nsa_win[l], z['nsa_win'].reshape(dec, 1, 2, HEAD_DIM)], axis=1)
        wl = min(NSA_WINDOW, win_all.shape[1])
        outs_s.append((z['nsa_kv4'].reshape(dec, 1, 4, HEAD_DIM),
                       z['dsa_kv'].reshape(dec, 1, 2, HEAD_DIM),
                       z['small'][:, SM_IDXK:SM_IDXK + IDX_DIM].reshape(dec, 1, IDX_DIM),
                       mla_st.reshape(dec, 1, KV_LORA + QK_ROPE),
                       win_all[:, win_all.shape[1] - wl:],
                       s_new,
                       jnp.stack([prev[:, 1], jnp.concatenate([ug, uv], axis=-1)], axis=1)))

    sp = [jnp.stack(v) for v in zip(*outs_p)]
    ss = [jnp.stack(v) for v in zip(*outs_s)]
    return (xp.reshape(batch, seq, d), xs.reshape(dec, 1, d),
            sp[0], ss[0], sp[1], ss[1], sp[2], ss[2], sp[3], ss[3], sp[4], ss[4], sp[5], ss[5], sp[6], ss[6])
```

```python
import functools
import math

import jax
import jax.numpy as jnp
from jax import lax
from jax.experimental import pallas as pl
from jax.experimental.pallas import tpu as pltpu

F32, BF16, I32 = jnp.float32, jnp.bfloat16, jnp.int32

N_HEADS = 4
HEAD_DIM = 64
BRANCH_W = N_HEADS * HEAD_DIM
N_BRANCH = 4
NSA_CMP_BLOCK = 32
NSA_SEL_BLOCK = 64
NSA_TOP_N = 16
NSA_WINDOW = 512
FORCE_SCORE = 1e9
IDX_HEADS = 4
IDX_DIM = 64
DSA_TOPK_MAX = 256
HGRN_CHUNK = 16
Q_LORA = 256
KV_LORA = 128
QK_NOPE = 64
QK_ROPE = 32
V_DIM = 64
MLA_HEAD_QK = QK_NOPE + QK_ROPE
ROPE_THETA = 10000.0
REL_BUCKETS = 32
REL_MAX_DIST = 512
CONV_W = 3
EPS = 1e-6
PAGE_SIZE = 128

LANES = 128
VMEM_LIMIT = 56 * 1024 * 1024

OUT_WIDTHS = (256, 256, 128, 256, 128, 256, 1024, 256, 128, 128, 4096)
OUT_NAMES = ('nsa_q', 'nsa_kv4', 'nsa_win', 'dsa_q', 'dsa_kv', 'idx_q', 'hg', 'mla_cq', 'mla_ckv', 'small', 'merge_g')
SM_IDXK, SM_KR, SM_NSAG, SM_IDXW = 0, 64, 96, 108
MLA_QW = 256
INT_MIN = -2 ** 31


def _cp(sem, vmem=VMEM_LIMIT):
    return pltpu.CompilerParams(dimension_semantics=sem, vmem_limit_bytes=vmem)


def _dot(a, b):
    return jnp.dot(a.astype(BF16), b.astype(BF16), preferred_element_type=F32)


def _dot_nt(a, b):
    return lax.dot_general(a.astype(BF16), b.astype(BF16), (((1,), (1,)), ((), ())), preferred_element_type=F32)


def _dot_tn(a, b):
    return lax.dot_general(a.astype(BF16), b.astype(BF16), (((0,), (0,)), ((), ())), preferred_element_type=F32)


def _dot_hilo(a, g):
    hi = a.astype(BF16)
    lo = (a - hi.astype(F32)).astype(BF16)
    return jnp.dot(hi, g, preferred_element_type=F32) + jnp.dot(lo, g, preferred_element_type=F32)


def _rms(x, g):
    return x * lax.rsqrt(jnp.mean(x * x, axis=-1, keepdims=True) + EPS) * g


def _sigmoid(x):
    return 1.0 / (1.0 + jnp.exp(-x))


def _silu(x):
    return x * _sigmoid(x)


def _iota(shape, dim):
    return lax.broadcasted_iota(I32, shape, dim)


def _head_block_ones(n):
    return (_iota((n, n), 0) // HEAD_DIM == _iota((n, n), 1) // HEAD_DIM)


def _rel_bias(d, tab_ref, heads):
    exact = REL_BUCKETS // 2
    n = jnp.maximum(d, 0)
    nf = jnp.maximum(n, 1).astype(F32)
    log_b = exact + (jnp.log(nf / exact) / math.log(REL_MAX_DIST / exact) * (REL_BUCKETS - exact)).astype(I32)
    bucket = jnp.where(n < exact, n, jnp.minimum(log_b, REL_BUCKETS - 1))
    outs = []
    for h in heads:
        o = jnp.full(d.shape, tab_ref[0, h], F32)
        for k in range(1, REL_BUCKETS):
            o = jnp.where(bucket == k, tab_ref[k, h], o)
        outs.append(o)
    return outs


def _ordered_key(x):
    x = jnp.where(x == 0.0, 0.0, x)
    bits = pltpu.bitcast(x, I32)
    return jnp.where(bits < 0, bits ^ 0x7FFFFFFF, bits)


def _kth_largest_key(count_ge, k, shape):
    t = jnp.full(shape, INT_MIN, I32)
    zero = jnp.zeros(shape, I32)
    t = jnp.where(count_ge(zero) >= k, zero, t)
    for bit in range(30, -1, -1):
        cand = t + (1 << bit)
        t = jnp.where(count_ge(cand) >= k, cand, t)
    return t


def _softmax_step(lg, mask, m_ref, l_ref):
    lg = jnp.where(mask, lg, -jnp.inf)
    m_old = m_ref[...]
    m_new = jnp.maximum(m_old, jnp.max(lg, axis=-1, keepdims=True))
    m_safe = jnp.where(m_new == -jnp.inf, 0.0, m_new)
    p = jnp.exp(lg - m_safe)
    alpha = jnp.exp(m_old - m_safe)
    l_ref[...] = alpha * l_ref[...] + jnp.sum(p, axis=-1, keepdims=True)
    m_ref[...] = m_new
    return p, alpha


def _ada_kernel(c_ref, w_ref, b_ref, o_ref):
    o_ref[0] = _dot(_silu(c_ref[...]), w_ref[0]) + b_ref[0]


def _ada(c, w_ada, b_ada):
    depth, d, n = w_ada.shape
    bc = c.shape[0]
    tn = 1536 if n % 1536 == 0 else n
    return pl.pallas_call(
        _ada_kernel,
        out_shape=jax.ShapeDtypeStruct((depth, bc, n), F32),
        grid=(depth, n // tn),
        in_specs=[pl.BlockSpec((bc, d), lambda l, j: (0, 0)),
                  pl.BlockSpec((1, d, tn), lambda l, j: (l, 0, j)),
                  pl.BlockSpec((1, 1, tn), lambda l, j: (l, 0, j))],
        out_specs=pl.BlockSpec((1, bc, tn), lambda l, j: (l, 0, j)),
        compiler_params=_cp(("arbitrary", "arbitrary")),
        name="ada",
    )(c, w_ada, b_ada.reshape(depth, 1, n))


def _bias_kernel(tab_ref, cend_ref, tile_ref, cmp_ref, *, tq, tk, seq):
    j = _iota((tk, tq), 0)
    i = _iota((tk, tq), 1)
    for k in range(4):
        bs = _rel_bias(k * tk + i - j, tab_ref, range(2 * N_HEADS))
        for h in range(2 * N_HEADS):
            tile_ref[k, h] = bs[h]
    t = _iota((cend_ref.shape[0], seq), 1)
    bs = _rel_bias(t - cend_ref[...], tab_ref, range(N_HEADS))
    for h in range(N_HEADS):
        cmp_ref[h] = bs[h]


def _bias_tiles(rel_table, cend, tq, tk, seq):
    nc = cend.shape[0]
    return pl.pallas_call(
        functools.partial(_bias_kernel, tq=tq, tk=tk, seq=seq),
        out_shape=(jax.ShapeDtypeStruct((4, 2 * N_HEADS, tk, tq), F32),
                   jax.ShapeDtypeStruct((N_HEADS, nc, seq), F32)),
        in_specs=[pl.BlockSpec(memory_space=pltpu.SMEM),
                  pl.BlockSpec(memory_space=pltpu.VMEM)],
        out_specs=(pl.BlockSpec(memory_space=pltpu.VMEM), pl.BlockSpec(memory_space=pltpu.VMEM)),
        compiler_params=pltpu.CompilerParams(vmem_limit_bytes=VMEM_LIMIT),
        name="rel_bias_tiles",
    )(rel_table, cend)


def _inproj_kernel(x_ref, g_ref, sh_ref, sc_ref, w_ref, *o_refs):
    h = _rms(x_ref[...], g_ref[...]) * (1.0 + sc_ref[0]) + sh_ref[0]
    hb = h.astype(BF16)
    off = 0
    for o_ref, w in zip(o_refs, OUT_WIDTHS):
        o_ref[...] = jnp.dot(hb, w_ref[:, off:off + w], preferred_element_type=F32)
        off += w


ROW_WIDTHS = (256, 128, 128, 1024, 256, 128, 128, 4096)
ROW_NAMES = ('nsa_kv4', 'nsa_win', 'dsa_kv', 'hg', 'mla_cq', 'mla_ckv', 'small', 'merge_g')
FM_WIDTHS = (256, 256, 256, 256, 16)
FM_NAMES = ('nsa_qT', 'dsa_qT', 'idx_qT', 'vT', 'smallT')


def _inproj_fm_kernel(x_ref, g_ref, sh_ref, sc_ref, w_ref, wt_ref, *o_refs):
    h = _rms(x_ref[...], g_ref[...]) * (1.0 + sc_ref[0]) + sh_ref[0]
    hb = h.astype(BF16)
    off = 0
    for o_ref, w in zip(o_refs[:len(ROW_WIDTHS)], ROW_WIDTHS):
        o_ref[...] = jnp.dot(hb, w_ref[:, off:off + w], preferred_element_type=F32)
        off += w
    off = 0
    for o_ref, w in zip(o_refs[len(ROW_WIDTHS):], FM_WIDTHS):
        o_ref[0] = _dot_nt(wt_ref[off:off + w, :], hb).astype(o_ref.dtype)
        off += w


def _inproj_fm(x, g, shift, scale, w_row, w_fm, tm, tps):
    m, d = x.shape
    fm_dtypes = (BF16, BF16, BF16, BF16, F32)
    return pl.pallas_call(
        _inproj_fm_kernel,
        out_shape=tuple(jax.ShapeDtypeStruct((m, w), F32) for w in ROW_WIDTHS)
                  + tuple(jax.ShapeDtypeStruct((m // tm, w, tm), dt) for w, dt in zip(FM_WIDTHS, fm_dtypes)),
        grid=(m // tm,),
        in_specs=[pl.BlockSpec((tm, d), lambda i: (i, 0)),
                  pl.BlockSpec((1, d), lambda i: (0, 0)),
                  _mod_spec(shift, tm, tps), _mod_spec(scale, tm, tps),
                  pl.BlockSpec(w_row.shape, lambda i: (0, 0), pipeline_mode=pl.Buffered(1)),
                  pl.BlockSpec(w_fm.shape, lambda i: (0, 0), pipeline_mode=pl.Buffered(1))],
        out_specs=tuple(pl.BlockSpec((tm, w), lambda i: (i, 0)) for w in ROW_WIDTHS)
                  + tuple(pl.BlockSpec((1, w, tm), lambda i: (i, 0, 0)) for w in FM_WIDTHS),
        compiler_params=_cp(("arbitrary",)),
        name="inproj_fm",
    )(x, g, shift, scale, w_row, w_fm)


def _mod_spec(arr, tm, tps):
    d = arr.shape[-1]
    if arr.shape[1] == 1:
        return pl.BlockSpec((1, 1, d), lambda i, *_: (i // tps, 0, 0))
    return pl.BlockSpec((1, tm, d), lambda i, *_: (0, i, 0))


def _inproj(x, g, shift, scale, w_p, tm, tps):
    m, d = x.shape
    n = w_p.shape[1]
    return pl.pallas_call(
        _inproj_kernel,
        out_shape=tuple(jax.ShapeDtypeStruct((m, w), F32) for w in OUT_WIDTHS),
        grid=(m // tm,),
        in_specs=[pl.BlockSpec((tm, d), lambda i: (i, 0)),
                  pl.BlockSpec((1, d), lambda i: (0, 0)),
                  _mod_spec(shift, tm, tps), _mod_spec(scale, tm, tps),
                  pl.BlockSpec((d, n), lambda i: (0, 0), pipeline_mode=pl.Buffered(1))],
        out_specs=tuple(pl.BlockSpec((tm, w), lambda i: (i, 0)) for w in OUT_WIDTHS),
        compiler_params=_cp(("arbitrary",)),
        name="inproj",
    )(x, g, shift, scale, w_p)


def _mla_proj_kernel(cq_ref, ckv_ref, sm_ref, inv_ref, gq_ref, gkv_ref, wn_ref, wr1_ref, wr2_ref, wuk_ref,
                     qc_ref, kc_ref, st_ref, *rest, tm, seq, pos0):
    feature_major = len(rest) > 0
    i = pl.program_id(0)
    qn = _rms(cq_ref[...], gq_ref[...]).astype(BF16)
    q_nope = jnp.dot(qn, wn_ref[...], preferred_element_type=F32)
    r1 = jnp.dot(qn, wr1_ref[...], preferred_element_type=F32)
    r2 = jnp.dot(qn, wr2_ref[...], preferred_element_type=F32)
    pos = ((i * tm + _iota((tm, 1), 0)) % seq + pos0).astype(F32)
    ang = pos * inv_ref[...]
    c, s = jnp.cos(ang), jnp.sin(ang)
    rot1 = r1 * c - r2 * s
    rot2 = r1 * s + r2 * c
    hr = QK_ROPE // 2
    zpad = jnp.zeros((tm, MLA_QW - KV_LORA - QK_ROPE), F32)
    for h in range(N_HEADS):
        q_lat = _dot(q_nope[:, h * QK_NOPE:(h + 1) * QK_NOPE], wuk_ref[h])
        qh = jnp.concatenate([q_lat, rot1[:, h * hr:(h + 1) * hr], rot2[:, h * hr:(h + 1) * hr], zpad], axis=-1)
        if feature_major:
            qc_ref[0, h * MLA_QW:(h + 1) * MLA_QW, :] = qh.T.astype(BF16)
        else:
            qc_ref[:, h * MLA_QW:(h + 1) * MLA_QW] = qh.astype(BF16)
    ckv = _rms(ckv_ref[...], gkv_ref[...])
    if feature_major:
        rest[0][0] = ckv.T.astype(BF16)
    sm = sm_ref[...]
    x1 = sm[:, SM_KR:SM_KR + hr]
    x2 = sm[:, SM_KR + hr:SM_KR + 2 * hr]
    ang1 = pos * inv_ref[:, 0:hr]
    c1, s1 = jnp.cos(ang1), jnp.sin(ang1)
    kr = jnp.concatenate([x1 * c1 - x2 * s1, x1 * s1 + x2 * c1], axis=-1)
    st = jnp.concatenate([ckv, kr], axis=-1)
    st_ref[...] = st
    kc_ref[...] = jnp.concatenate([st, zpad], axis=-1).astype(BF16)


def _mla_proj(cq, ckv, small, inv4, gq, gkv, wn, wr1, wr2, wukT, tm, seq, pos0, feature_major=False):
    m = cq.shape[0]
    row = lambda w: pl.BlockSpec((tm, w), lambda i: (i, 0))
    fm = lambda w: pl.BlockSpec((1, w, tm), lambda i: (i, 0, 0))
    full = lambda a: pl.BlockSpec(a.shape, lambda i: (0,) * a.ndim)
    out_shape = (jax.ShapeDtypeStruct((m, N_HEADS * MLA_QW), BF16),
                 jax.ShapeDtypeStruct((m, MLA_QW), BF16),
                 jax.ShapeDtypeStruct((m, KV_LORA + QK_ROPE), F32))
    out_specs = (row(N_HEADS * MLA_QW), row(MLA_QW), row(KV_LORA + QK_ROPE))
    if feature_major:
        out_shape = (jax.ShapeDtypeStruct((m // tm, N_HEADS * MLA_QW, tm), BF16),) + out_shape[1:] \
                    + (jax.ShapeDtypeStruct((m // tm, KV_LORA, tm), BF16),)
        out_specs = (fm(N_HEADS * MLA_QW),) + out_specs[1:] + (fm(KV_LORA),)
    return pl.pallas_call(
        functools.partial(_mla_proj_kernel, tm=tm, seq=seq, pos0=pos0),
        out_shape=out_shape,
        grid=(m // tm,),
        in_specs=[row(Q_LORA), row(KV_LORA), row(LANES), full(inv4), full(gq), full(gkv),
                  full(wn), full(wr1), full(wr2), full(wukT)],
        out_specs=out_specs,
        compiler_params=_cp(("arbitrary",)),
        name="mla_proj",
    )(cq, ckv, small, inv4, gq, gkv, wn, wr1, wr2, wukT)


def _flash_init(m_ref, l_ref, acc_ref):
    m_ref[...] = jnp.full(m_ref.shape, -jnp.inf, F32)
    l_ref[...] = jnp.zeros(l_ref.shape, F32)
    acc_ref[...] = jnp.zeros(acc_ref.shape, F32)


def _units(tq):
    halves = tq // LANES
    return [(h, half) for h in range(N_HEADS) for half in range(halves)]


def _lane(half):
    return slice(half * LANES, (half + 1) * LANES)


def _flash_t(lgs, masks, v_t, m_s, l_s, acc_s):
    m_all, l_all = m_s[...], l_s[...]
    ps, alphas, new_m, new_l = [], [], [], []
    for u, (lg, mask) in enumerate(zip(lgs, masks)):
        lg = jnp.where(mask, lg, -jnp.inf)
        m_old = m_all[u:u + 1, :]
        m_new = jnp.maximum(m_old, jnp.max(lg, axis=0, keepdims=True))
        m_safe = jnp.where(m_new == -jnp.inf, 0.0, m_new)
        p = jnp.exp(lg - m_safe)
        alpha = jnp.exp(m_old - m_safe)
        new_l.append(alpha * l_all[u:u + 1, :] + jnp.sum(p, axis=0, keepdims=True))
        new_m.append(m_new)
        alphas.append(alpha)
        ps.append(p.astype(BF16))
    m_s[...] = jnp.concatenate(new_m, axis=0)
    l_s[...] = jnp.concatenate(new_l, axis=0)
    for u in range(len(lgs)):
        acc_s[u] = alphas[u] * acc_s[u] + jnp.dot(v_t, ps[u], preferred_element_type=F32)


def _flash_t_out(u, l_s, acc_s):
    return acc_s[u] / jnp.maximum(l_s[u:u + 1, :], 1e-30)


def _store_units_transposed(o_ref, outs, tq):
    halves = tq // LANES
    rows = [jnp.concatenate(outs[h * halves:(h + 1) * halves], axis=-1) for h in range(N_HEADS)]
    o_ref[...] = jnp.concatenate(rows, axis=0).T


def _nsa_prompt_kernel(qt_ref, gt_ref, kv_ref, win_ref, vt_ref, bias_ref, bcmp_ref, o_ref,
                       cmp_s, m_s, l_s, acc_s, *, seq, tq, n_top):
    tk = tq
    qb = pl.program_id(1)
    t0 = qb * tq
    nsel = seq // NSA_SEL_BLOCK
    nc = 2 * nsel
    scale = HEAD_DIM ** -0.5
    units = _units(tq)
    halves = tq // LANES

    @pl.when(qb == 0)
    def _():
        for j in range(nsel):
            r0 = j * NSA_SEL_BLOCK
            cmp_s[j:j + 1, :] = jnp.sum(kv_ref[r0:r0 + NSA_CMP_BLOCK, 0:2 * HEAD_DIM], axis=0, keepdims=True) * (1.0 / NSA_CMP_BLOCK)
            cmp_s[nsel + j:nsel + j + 1, :] = jnp.sum(kv_ref[r0 + NSA_CMP_BLOCK:r0 + 2 * NSA_CMP_BLOCK, 0:2 * HEAD_DIM],
                                                      axis=0, keepdims=True) * (1.0 / NSA_CMP_BLOCK)

    def q_unit(h, half):
        return qt_ref[h * HEAD_DIM:(h + 1) * HEAD_DIM, _lane(half)]
    t_rows = [t0 + half * LANES + _iota((1, LANES), 1) for half in range(halves)]

    kc = cmp_s[:, 0:HEAD_DIM].astype(BF16)
    eye = (_iota((HEAD_DIM, HEAD_DIM), 0) == _iota((HEAD_DIM, HEAD_DIM), 1)).astype(BF16)
    vc_t = _dot_nt(eye, cmp_s[:, HEAD_DIM:2 * HEAD_DIM]).astype(BF16)
    jj = _iota((nc, 1), 0)
    c_end = jnp.where(jj < nsel, NSA_SEL_BLOCK * jj + (NSA_CMP_BLOCK - 1), NSA_SEL_BLOCK * (jj - nsel) + (NSA_SEL_BLOCK - 1))
    lgs = [jnp.dot(kc, q_unit(h, half), preferred_element_type=F32) for h, half in units]
    o_cmp, ps_half = [], [None] * halves
    for u, (h, half) in enumerate(units):
        lg = lgs[u] * scale + bcmp_ref[h, :, _lane(half)]
        lg = jnp.where(c_end <= t_rows[half], lg, -jnp.inf)
        mx = jnp.max(lg, axis=0, keepdims=True)
        mx = jnp.where(mx == -jnp.inf, 0.0, mx)
        p = jnp.exp(lg - mx)
        p = p / jnp.maximum(jnp.sum(p, axis=0, keepdims=True), 1e-30)
        o_cmp.append(jnp.dot(vc_t, p.astype(BF16), preferred_element_type=F32))
        ps_half[half] = p if ps_half[half] is None else ps_half[half] + p

    blk = _iota((nsel, 1), 0)
    selm = []
    for half in range(halves):
        imp = ps_half[half][0:nsel] + ps_half[half][nsel:nc]
        cur = t_rows[half] // NSA_SEL_BLOCK
        imp = jnp.where((blk == cur) | (blk == 0), FORCE_SCORE, imp)
        imp = jnp.where(blk <= cur, imp, -jnp.inf)
        chosen = jnp.zeros((nsel, LANES), F32)
        for _ in range(n_top):
            top = jnp.max(imp, axis=0, keepdims=True)
            first = jnp.min(jnp.where(imp == top, blk, nsel), axis=0, keepdims=True)
            pick = blk == first
            chosen = jnp.where(pick, 1.0, chosen)
            imp = jnp.where(pick, -jnp.inf, imp)
        selm.append(chosen.astype(BF16))

    _flash_init(m_s, l_s, acc_s)

    def sel_body(c, carry):
        s0 = pl.multiple_of(c * tk, tk)
        k = kv_ref[pl.ds(s0, tk), 2 * HEAD_DIM:3 * HEAD_DIM].astype(BF16)
        kk = jnp.minimum(qb - c, 3)
        s_pos = s0 + _iota((tk, 1), 0)
        expand = ((s0 + _iota((tk, nsel), 0)) // NSA_SEL_BLOCK == _iota((tk, nsel), 1)).astype(BF16)
        lgs = [jnp.dot(k, q_unit(h, half), preferred_element_type=F32) for h, half in units]
        picked = [jnp.dot(expand, selm[half], preferred_element_type=F32) > 0.5 for half in range(halves)]
        lgs = [lgs[u] * scale + bias_ref[kk, h, :, _lane(half)] for u, (h, half) in enumerate(units)]
        masks = [picked[half] & (s_pos <= t_rows[half]) for h, half in units]
        _flash_t(lgs, masks, vt_ref[c, HEAD_DIM:2 * HEAD_DIM, :], m_s, l_s, acc_s)
        return carry

    lax.fori_loop(0, qb + 1, sel_body, 0)
    o_sel = [_flash_t_out(u, l_s, acc_s) for u in range(len(units))]

    _flash_init(m_s, l_s, acc_s)

    def win_body(c, carry):
        s0 = pl.multiple_of(c * tk, tk)
        k = win_ref[pl.ds(s0, tk), 0:HEAD_DIM].astype(BF16)
        kk = jnp.minimum(qb - c, 3)
        s_pos = s0 + _iota((tk, 1), 0)
        lgs = [jnp.dot(k, q_unit(h, half), preferred_element_type=F32) for h, half in units]
        lgs = [lgs[u] * scale + bias_ref[kk, h, :, _lane(half)] for u, (h, half) in enumerate(units)]
        masks = [(t_rows[half] - s_pos >= 0) & (t_rows[half] - s_pos <= NSA_WINDOW) for h, half in units]
        _flash_t(lgs, masks, vt_ref[c, 2 * HEAD_DIM:3 * HEAD_DIM, :], m_s, l_s, acc_s)
        return carry

    lax.fori_loop(jnp.maximum(qb - (NSA_WINDOW + tk - 1) // tk, 0), qb + 1, win_body, 0)

    g = _sigmoid(gt_ref[0:3 * N_HEADS, :])
    outs = []
    for u, (h, half) in enumerate(units):
        outs.append(g[3 * h:3 * h + 1, _lane(half)] * o_cmp[u] + g[3 * h + 1:3 * h + 2, _lane(half)] * o_sel[u]
                    + g[3 * h + 2:3 * h + 3, _lane(half)] * _flash_t_out(u, l_s, acc_s))
    _store_units_transposed(o_ref, outs, tq)


def _fm_block(w, tq, nq):
    return pl.BlockSpec((None, w, tq), lambda b, i: (b * nq + i, 0, 0))


def _nsa_prompt(nsa_qt, small_t, kv4, win, v_t, bias_t, bias_cmp, batch, seq, tq):
    nq = seq // tq
    nsel = seq // NSA_SEL_BLOCK
    nu = len(_units(tq))
    rows = lambda w: pl.BlockSpec((seq, w), lambda b, i: (b, 0))
    return pl.pallas_call(
        functools.partial(_nsa_prompt_kernel, seq=seq, tq=tq, n_top=min(NSA_TOP_N, nsel)),
        out_shape=jax.ShapeDtypeStruct((batch * seq, BRANCH_W), F32),
        grid=(batch, nq),
        in_specs=[_fm_block(BRANCH_W, tq, nq), _fm_block(FM_WIDTHS[4], tq, nq), rows(4 * HEAD_DIM), rows(2 * HEAD_DIM),
                  pl.BlockSpec((nq, FM_WIDTHS[3], tq), lambda b, i: (b, 0, 0)),
                  pl.BlockSpec((4, N_HEADS, tq, tq), lambda b, i: (0, 0, 0, 0)),
                  pl.BlockSpec((N_HEADS, 2 * nsel, tq), lambda b, i: (0, 0, i))],
        out_specs=pl.BlockSpec((tq, BRANCH_W), lambda b, i: (b * nq + i, 0)),
        scratch_shapes=[pltpu.VMEM((2 * nsel, 2 * HEAD_DIM), F32),
                        pltpu.VMEM((nu, LANES), F32), pltpu.VMEM((nu, LANES), F32),
                        pltpu.VMEM((nu, HEAD_DIM, LANES), F32)],
        compiler_params=_cp(("arbitrary", "arbitrary")),
        name="nsa_prompt",
    )(nsa_qt, small_t, kv4, win, v_t, bias_t, bias_cmp)


def _dsa_prompt_kernel(qt_ref, iqt_ref, gt_ref, smf_ref, kv_ref, vt_ref, bias_ref, o_ref,
                       key_s, m_s, l_s, acc_s, *, tq, topk):
    tk = tq
    qb = pl.program_id(1)
    t0 = qb * tq
    scale = HEAD_DIM ** -0.5
    units = _units(tq)
    halves = tq // LANES
    t_row = t0 + _iota((1, tq), 1)
    cst = IDX_DIM ** -0.5 * IDX_HEADS ** -0.5
    wi = gt_ref[3 * N_HEADS:3 * N_HEADS + IDX_HEADS, :]

    def score_body(c, carry):
        s0 = pl.multiple_of(c * tk, tk)
        ki = smf_ref[pl.ds(s0, tk), SM_IDXK:SM_IDXK + IDX_DIM].astype(BF16)
        s_pos = s0 + _iota((tk, 1), 0)
        dots = [jnp.dot(ki, iqt_ref[h * IDX_DIM:(h + 1) * IDX_DIM, _lane(half)], preferred_element_type=F32)
                for h, half in units]
        for half in range(halves):
            sc = None
            for u, (h, hf) in enumerate(units):
                if hf == half:
                    term = jnp.maximum(dots[u], 0.0) * wi[h:h + 1, _lane(half)]
                    sc = term if sc is None else sc + term
            sc = jnp.where(s_pos <= t_row[:, _lane(half)], sc * cst, -jnp.inf)
            key_s[c, :, _lane(half)] = _ordered_key(sc)
        return carry

    lax.fori_loop(0, qb + 1, score_body, 0)

    def count(pred):
        def body(c, acc):
            return acc + jnp.sum(jnp.where(pred(key_s[c]), 1, 0), axis=0, keepdims=True)
        return lax.fori_loop(0, qb + 1, body, jnp.zeros((1, tq), I32))

    thr = _kth_largest_key(lambda cand: count(lambda key: key >= cand), topk, (1, tq))
    need = (topk - count(lambda key: key > thr)).astype(F32)

    strict_lower = (_iota((tk, tk), 1) < _iota((tk, tk), 0)).astype(BF16)
    _flash_init(m_s, l_s, acc_s)

    def att_body(c, run):
        s0 = pl.multiple_of(c * tk, tk)
        key = key_s[c]
        eq = key == thr
        eqb = jnp.where(eq, 1.0, 0.0).astype(BF16)
        before = jnp.dot(strict_lower, eqb, preferred_element_type=F32) + run
        chosen = ((key > thr) | (eq & (before < need))) & (s0 + _iota((tk, 1), 0) <= t_row)
        k = kv_ref[pl.ds(s0, tk), 0:HEAD_DIM].astype(BF16)
        kk = jnp.minimum(qb - c, 3)
        lgs = [jnp.dot(k, qt_ref[h * HEAD_DIM:(h + 1) * HEAD_DIM, _lane(half)], preferred_element_type=F32)
               for h, half in units]
        lgs = [lgs[u] * scale + bias_ref[kk, h, :, _lane(half)] for u, (h, half) in enumerate(units)]
        masks = [chosen[:, _lane(half)] for h, half in units]
        _flash_t(lgs, masks, vt_ref[c, 3 * HEAD_DIM:4 * HEAD_DIM, :], m_s, l_s, acc_s)
        return run + jnp.sum(eqb.astype(F32), axis=0, keepdims=True)

    lax.fori_loop(0, qb + 1, att_body, jnp.zeros((1, tq), F32))
    _store_units_transposed(o_ref, [_flash_t_out(u, l_s, acc_s) for u in range(len(units))], tq)


def _dsa_prompt(dsa_qt, idx_qt, small_t, small, dsa_kv, v_t, bias_t, batch, seq, tq):
    nq = seq // tq
    nu = len(_units(tq))
    rows = lambda w: pl.BlockSpec((seq, w), lambda b, i: (b, 0))
    return pl.pallas_call(
        functools.partial(_dsa_prompt_kernel, tq=tq, topk=min(DSA_TOPK_MAX, seq // 4)),
        out_shape=jax.ShapeDtypeStruct((batch * seq, BRANCH_W), F32),
        grid=(batch, nq),
        in_specs=[_fm_block(BRANCH_W, tq, nq), _fm_block(IDX_HEADS * IDX_DIM, tq, nq), _fm_block(FM_WIDTHS[4], tq, nq),
                  rows(LANES), rows(2 * HEAD_DIM),
                  pl.BlockSpec((nq, FM_WIDTHS[3], tq), lambda b, i: (b, 0, 0)),
                  pl.BlockSpec((4, N_HEADS, tq, tq), lambda b, i: (0, 1, 0, 0))],
        out_specs=pl.BlockSpec((tq, BRANCH_W), lambda b, i: (b * nq + i, 0)),
        scratch_shapes=[pltpu.VMEM((nq, tq, tq), I32),
                        pltpu.VMEM((nu, LANES), F32), pltpu.VMEM((nu, LANES), F32),
                        pltpu.VMEM((nu, HEAD_DIM, LANES), F32)],
        compiler_params=_cp(("arbitrary", "arbitrary")),
        name="dsa_prompt",
    )(dsa_qt, idx_qt, small_t, small, dsa_kv, v_t, bias_t)


def _mla_prompt_kernel(qt_ref, kc_ref, vt_ref, wuvt_ref, o_ref, m_s, l_s, acc_s, *, tq):
    tk = tq
    qb = pl.program_id(1)
    scale = MLA_HEAD_QK ** -0.5
    units = _units(tq)
    t_rows = [qb * tq + half * LANES + _iota((1, LANES), 1) for half in range(tq // LANES)]
    _flash_init(m_s, l_s, acc_s)

    def body(c, carry):
        s0 = pl.multiple_of(c * tk, tk)
        kc = kc_ref[pl.ds(s0, tk), :]
        s_pos = s0 + _iota((tk, 1), 0)
        lgs = [jnp.dot(kc, qt_ref[h * MLA_QW:(h + 1) * MLA_QW, _lane(half)], preferred_element_type=F32) * scale
               for h, half in units]
        masks = [s_pos <= t_rows[half] for h, half in units]
        _flash_t(lgs, masks, vt_ref[c], m_s, l_s, acc_s)
        return carry

    lax.fori_loop(0, qb + 1, body, 0)
    outs = [jnp.dot(wuvt_ref[h], _flash_t_out(u, l_s, acc_s).astype(BF16), preferred_element_type=F32)
            for u, (h, half) in enumerate(units)]
    _store_units_transposed(o_ref, outs, tq)


def _mla_prompt(qc_t, kc, v_t, wuv_t, batch, seq, tq):
    nq = seq // tq
    nu = len(_units(tq))
    return pl.pallas_call(
        functools.partial(_mla_prompt_kernel, tq=tq),
        out_shape=jax.ShapeDtypeStruct((batch * seq, BRANCH_W), F32),
        grid=(batch, nq),
        in_specs=[_fm_block(N_HEADS * MLA_QW, tq, nq),
                  pl.BlockSpec((seq, MLA_QW), lambda b, i: (b, 0)),
                  pl.BlockSpec((nq, KV_LORA, tq), lambda b, i: (b, 0, 0)),
                  pl.BlockSpec(wuv_t.shape, lambda b, i: (0, 0, 0))],
        out_specs=pl.BlockSpec((tq, BRANCH_W), lambda b, i: (b * nq + i, 0)),
        scratch_shapes=[pltpu.VMEM((nu, LANES), F32), pltpu.VMEM((nu, LANES), F32),
                        pltpu.VMEM((nu, KV_LORA, LANES), F32)],
        compiler_params=_cp(("arbitrary", "arbitrary")),
        name="mla_prompt",
    )(qc_t, kc, v_t, wuv_t)


def _hgrn_gates(hg, lb):
    w = BRANCH_W
    q, fl, iv, gg = hg[:, 0:w], hg[:, w:2 * w], hg[:, 2 * w:3 * w], hg[:, 3 * w:4 * w]
    f = lb + (1.0 - lb) * _sigmoid(fl)
    return _silu(q), f, 1.0 - f, iv, gg


def _hgrn_finish(o, gg, hgn, ones_bf):
    ms = _dot_hilo(o * o, ones_bf) * (1.0 / HEAD_DIM)
    return o * lax.rsqrt(ms + EPS) * hgn * _silu(gg)


def _hgrn_prompt_kernel(hg_ref, lb_ref, hgn_ref, o_ref, s_ref, st_s, q_s, b_s, k_s, v_s, o_s, w_s, *, tc):
    C = HGRN_CHUNK
    w = BRANCH_W
    i = pl.program_id(1)

    @pl.when(i == 0)
    def _():
        st_s[...] = jnp.zeros(st_s.shape, F32)

    qf, f, k, iv, gg = _hgrn_gates(hg_ref[...], lb_ref[...])
    b = jnp.log(jnp.maximum(f, 1e-20))
    row = _iota((tc, 1), 0) % C
    for s in (1, 2, 4, 8):
        b = b + jnp.where(row >= s, pltpu.roll(b, s, 0), 0.0)
    q_s[...] = qf
    b_s[...] = b
    k_s[...] = k
    v_s[...] = iv
    same_head = _head_block_ones(w)
    ones_bf = same_head.astype(BF16)
    group = (_iota((C, C * C), 0) == _iota((C, C * C), 1) // C).astype(BF16)
    s_idx = _iota((C, 1), 0)

    def chunk(ci, carry):
        r0 = pl.multiple_of(ci * C, C)
        qc, bc, kc, vc = q_s[pl.ds(r0, C), :], b_s[pl.ds(r0, C), :], k_s[pl.ds(r0, C), :], v_s[pl.ds(r0, C), :]
        for tt in range(C):
            dec = jnp.exp(jnp.where(s_idx <= tt, bc[tt:tt + 1, :] - bc, -jnp.inf))
            w_s[tt * C:(tt + 1) * C, :] = dec * qc[tt:tt + 1, :] * kc
        a_rep = _dot_hilo(w_s[...], ones_bf)
        prod = (a_rep.reshape(C, C, w) * vc[None]).reshape(C * C, w)
        hi = prod.astype(BF16)
        lo = (prod - hi.astype(F32)).astype(BF16)
        o_intra = jnp.dot(group, hi, preferred_element_type=F32) + jnp.dot(group, lo, preferred_element_type=F32)
        st = st_s[...]
        o_s[pl.ds(r0, C), :] = o_intra + _dot_nt(qc * jnp.exp(bc), st)
        bl = bc[C - 1:C, :]
        upd = _dot_tn(vc, kc * jnp.exp(bl - bc))
        st_s[...] = st * jnp.exp(bl) + jnp.where(same_head, upd, 0.0)
        return carry

    lax.fori_loop(0, tc // C, chunk, 0)
    o_ref[...] = _hgrn_finish(o_s[...], gg, hgn_ref[...], ones_bf)

    @pl.when(i == pl.num_programs(1) - 1)
    def _():
        for h in range(N_HEADS):
            s_ref[0, h] = st_s[h * HEAD_DIM:(h + 1) * HEAD_DIM, h * HEAD_DIM:(h + 1) * HEAD_DIM].T


def _hgrn_prompt(hg, lower, hgn, batch, seq, tc):
    nt = seq // tc
    w = BRANCH_W
    return pl.pallas_call(
        functools.partial(_hgrn_prompt_kernel, tc=tc),
        out_shape=(jax.ShapeDtypeStruct((batch * seq, w), F32),
                   jax.ShapeDtypeStruct((batch, N_HEADS, HEAD_DIM, HEAD_DIM), F32)),
        grid=(batch, nt),
        in_specs=[pl.BlockSpec((tc, 4 * w), lambda b, i: (b * nt + i, 0)),
                  pl.BlockSpec((1, w), lambda b, i: (0, 0)),
                  pl.BlockSpec((1, w), lambda b, i: (0, 0))],
        out_specs=(pl.BlockSpec((tc, w), lambda b, i: (b * nt + i, 0)),
                   pl.BlockSpec((1, N_HEADS, HEAD_DIM, HEAD_DIM), lambda b, i: (b, 0, 0, 0))),
        scratch_shapes=[pltpu.VMEM((w, w), F32)] + [pltpu.VMEM((tc, w), F32)] * 5
                       + [pltpu.VMEM((HGRN_CHUNK * HGRN_CHUNK, w), F32)],
        compiler_params=_cp(("arbitrary", "arbitrary")),
        name="hgrn_prompt",
    )(hg, lower, hgn)


def _hgrn_step_kernel(hg_ref, lb_ref, hgn_ref, s0_ref, o_ref, s_ref, o_s, *, bt):
    qf, f, k, iv, gg = _hgrn_gates(hg_ref[...], lb_ref[...])
    fT = jnp.maximum(f, 1e-20).T
    kT = k.T
    qT = qf.T
    for bi in range(bt):
        for h in range(N_HEADS):
            hs = slice(h * HEAD_DIM, (h + 1) * HEAD_DIM)
            s_new = fT[hs, bi:bi + 1] * s0_ref[bi, h] + kT[hs, bi:bi + 1] * iv[bi:bi + 1, hs]
            s_ref[bi, h] = s_new
            o_s[bi:bi + 1, hs] = jnp.sum(qT[hs, bi:bi + 1] * s_new, axis=0, keepdims=True)
    o_ref[...] = _hgrn_finish(o_s[...], gg, hgn_ref[...], _head_block_ones(BRANCH_W).astype(BF16))


def _hgrn_step(hg, lower, hgn, s0, bt):
    m = hg.shape[0]
    w = BRANCH_W
    sblk = pl.BlockSpec((bt, N_HEADS, HEAD_DIM, HEAD_DIM), lambda i: (i, 0, 0, 0))
    return pl.pallas_call(
        functools.partial(_hgrn_step_kernel, bt=bt),
        out_shape=(jax.ShapeDtypeStruct((m, w), F32), jax.ShapeDtypeStruct(s0.shape, F32)),
        grid=(m // bt,),
        in_specs=[pl.BlockSpec((bt, 4 * w), lambda i: (i, 0)),
                  pl.BlockSpec((1, w), lambda i: (0, 0)), pl.BlockSpec((1, w), lambda i: (0, 0)), sblk],
        out_specs=(pl.BlockSpec((bt, w), lambda i: (i, 0)), sblk),
        scratch_shapes=[pltpu.VMEM((bt, w), F32)],
        compiler_params=_cp(("arbitrary",)),
        name="hgrn_step",
    )(hg, lower, hgn, s0)


def _merge_kernel(x_ref, o0_ref, o1_ref, o2_ref, o3_ref, mg_ref, gate_ref, gpost_ref, wb_ref, wo_ref, y_ref):
    d = x_ref.shape[1]
    mixed = None
    for n, o_ref in enumerate((o0_ref, o1_ref, o2_ref, o3_ref)):
        term = _sigmoid(mg_ref[:, n * d:(n + 1) * d]) * _dot(o_ref[...], wb_ref[n])
        mixed = term if mixed is None else mixed + term
    y = _dot(mixed, wo_ref[...])
    y_ref[...] = x_ref[...] + gate_ref[0] * _rms(y, gpost_ref[...])


def _merge(x, branches, merge_g, gate, gpost, wb, wo, tm, tps):
    m, d = x.shape
    row = lambda w: pl.BlockSpec((tm, w), lambda i: (i, 0))
    return pl.pallas_call(
        _merge_kernel,
        out_shape=jax.ShapeDtypeStruct((m, d), F32),
        grid=(m // tm,),
        in_specs=[row(d)] + [row(BRANCH_W)] * 4 + [row(N_BRANCH * d), _mod_spec(gate, tm, tps),
                  pl.BlockSpec((1, d), lambda i: (0, 0)),
                  pl.BlockSpec(wb.shape, lambda i: (0, 0, 0), pipeline_mode=pl.Buffered(1)),
                  pl.BlockSpec(wo.shape, lambda i: (0, 0), pipeline_mode=pl.Buffered(1))],
        out_specs=row(d),
        compiler_params=_cp(("arbitrary",)),
        name="merge",
    )(x, *branches, merge_g, gate, gpost, wb, wo)


def _gelu_tanh(x):
    return 0.5 * x * (1.0 + jnp.tanh(math.sqrt(2.0 / math.pi) * (x + 0.044715 * (x * x * x))))


def _ffn_kernel(*refs, tm, tps, nff, stepwise):
    if stepwise:
        (x_ref, gpre_ref, sh_ref, sc_ref, gate_ref, gpost_ref, wg_ref, wv_ref, cwg_ref, cwv_ref, cbg_ref, cbv_ref,
         wd_ref, p0g_ref, p0v_ref, p1g_ref, p1v_ref, y_ref, ug_ref, uv_ref, h_s, acc_s) = refs
    else:
        (x_ref, gpre_ref, sh_ref, sc_ref, gate_ref, gpost_ref, wg_ref, wv_ref, cwg_ref, cwv_ref, cbg_ref, cbv_ref,
         wd_ref, y_ref, csg_ref, csv_ref, h_s, acc_s, carry_g, carry_v) = refs
    i = pl.program_id(0)
    j = pl.program_id(1)

    @pl.when(j == 0)
    def _():
        h = _rms(x_ref[...], gpre_ref[...]) * (1.0 + sc_ref[0]) + sh_ref[0]
        h_s[...] = h.astype(BF16)
        acc_s[...] = jnp.zeros(acc_s.shape, F32)

    hb = h_s[...]
    ug = jnp.dot(hb, wg_ref[...], preferred_element_type=F32)
    uv = jnp.dot(hb, wv_ref[...], preferred_element_type=F32)

    if stepwise:
        def conv(u, cw_ref, cb_ref, p0_ref, p1_ref):
            return cb_ref[...] + p0_ref[...] * cw_ref[0:1, :] + p1_ref[...] * cw_ref[1:2, :] + u * cw_ref[2:3, :]
        cg = conv(ug, cwg_ref, cbg_ref, p0g_ref, p1g_ref)
        cv = conv(uv, cwv_ref, cbv_ref, p0v_ref, p1v_ref)
        ug_ref[...] = ug
        uv_ref[...] = uv
    else:
        first = (i % tps) == 0
        row = _iota((tm, 1), 0)

        @pl.when(i == 0)
        def _():
            carry_g[j] = jnp.zeros(carry_g.shape[1:], F32)
            carry_v[j] = jnp.zeros(carry_v.shape[1:], F32)

        def conv(u, cw_ref, cb_ref, carry):
            prev = jnp.where(first, 0.0, carry[j])
            um1 = jnp.where(row == 0, prev[1:2, :], pltpu.roll(u, 1, 0))
            um2 = jnp.where(row == 0, prev[0:1, :], jnp.where(row == 1, prev[1:2, :], pltpu.roll(u, 2, 0)))
            return cb_ref[...] + um2 * cw_ref[0:1, :] + um1 * cw_ref[1:2, :] + u * cw_ref[2:3, :]
        cg = conv(ug, cwg_ref, cbg_ref, carry_g)
        cv = conv(uv, cwv_ref, cbv_ref, carry_v)
        carry_g[j] = ug[tm - 2:tm, :]
        carry_v[j] = uv[tm - 2:tm, :]
        csg_ref[0, j] = ug[tm - 2:tm, :]
        csv_ref[0, j] = uv[tm - 2:tm, :]

    acc_s[...] += _dot(_gelu_tanh(cg) * cv, wd_ref[...])

    @pl.when(j == nff - 1)
    def _():
        y_ref[...] = x_ref[...] + gate_ref[0] * _rms(acc_s[...], gpost_ref[...])


def _ffn(x, gpre, shift, scale, gate, gpost, w_up, conv_w, conv_b, w_down, tm, tps, prev=None):
    m, d = x.shape
    dff = w_down.shape[0]
    nff = 2 if dff % (2 * LANES) == 0 else 1
    fc = dff // nff
    stepwise = prev is not None
    vec = lambda: pl.BlockSpec((1, d), lambda i, j: (0, 0))
    colg = lambda r: pl.BlockSpec((r, fc), lambda i, j: (0, j))
    colv = lambda r: pl.BlockSpec((r, fc), lambda i, j: (0, nff + j))
    in_specs = [pl.BlockSpec((tm, d), lambda i, j: (i, 0)), vec(),
                _mod_spec(shift, tm, tps), _mod_spec(scale, tm, tps), _mod_spec(gate, tm, tps), vec(),
                colg(d), colv(d), colg(CONV_W), colv(CONV_W), colg(1), colv(1),
                pl.BlockSpec((fc, d), lambda i, j: (j, 0))]
    args = [x, gpre, shift, scale, gate, gpost, w_up, w_up, conv_w, conv_w, conv_b, conv_b, w_down]
    scratch = [pltpu.VMEM((tm, d), BF16), pltpu.VMEM((tm, d), F32)]
    if stepwise:
        p0, p1 = prev
        in_specs += [pl.BlockSpec((tm, fc), lambda i, j: (i, j)), pl.BlockSpec((tm, fc), lambda i, j: (i, nff + j))] * 2
        args += [p0, p0, p1, p1]
        out_shape = (jax.ShapeDtypeStruct((m, d), F32), jax.ShapeDtypeStruct((m, dff), F32), jax.ShapeDtypeStruct((m, dff), F32))
        out_specs = (pl.BlockSpec((tm, d), lambda i, j: (i, 0)),
                     pl.BlockSpec((tm, fc), lambda i, j: (i, j)), pl.BlockSpec((tm, fc), lambda i, j: (i, j)))
    else:
        nseq = m // (tm * tps)
        out_shape = (jax.ShapeDtypeStruct((m, d), F32),
                     jax.ShapeDtypeStruct((nseq, nff, CONV_W - 1, fc), F32),
                     jax.ShapeDtypeStruct((nseq, nff, CONV_W - 1, fc), F32))
        cs = pl.BlockSpec((1, nff, CONV_W - 1, fc), lambda i, j: (i // tps, 0, 0, 0))
        out_specs = (pl.BlockSpec((tm, d), lambda i, j: (i, 0)), cs, cs)
        scratch += [pltpu.VMEM((nff, CONV_W - 1, fc), F32)] * 2
    return pl.pallas_call(
        functools.partial(_ffn_kernel, tm=tm, tps=tps, nff=nff, stepwise=stepwise),
        out_shape=out_shape,
        grid=(m // tm, nff),
        in_specs=in_specs,
        out_specs=out_specs,
        scratch_shapes=scratch,
        compiler_params=_cp(("arbitrary", "arbitrary")),
        name="ffn_step" if stepwise else "ffn_seq",
    )(*args)


def _page_specs(layer, pg, rows, rowblk):
    def spec(k):
        return pl.BlockSpec((None, None, rows, PAGE_SIZE), lambda b, j, pt, *_: (layer, pt[b, j * pg + k], rowblk, 0))
    return [spec(k) for k in range(pg)]


def _softmax_with_self(lg, valid, lg_self):
    lg = jnp.where(valid, lg, -jnp.inf)
    m = jnp.maximum(jnp.max(lg, axis=-1, keepdims=True), lg_self)
    p = jnp.exp(lg - m)
    p_self = jnp.exp(lg_self - m)
    den = jnp.sum(p, axis=-1, keepdims=True) + p_self
    return p / den, p_self / den


def _self_logit(q8, k_row):
    qf = q8.astype(BF16).astype(F32)
    kf = k_row.astype(BF16).astype(F32)
    return jnp.sum(qf * kf, axis=-1, keepdims=True)


def _rows8(x, w):
    return jnp.concatenate([x[:, h * w:(h + 1) * w] for h in range(N_HEADS)] + [jnp.zeros((8 - N_HEADS, w), x.dtype)], axis=0)


def _col8(x):
    r = _iota((8, 1), 0)
    out = jnp.zeros((8, 1), F32)
    for h in range(N_HEADS):
        out = jnp.where(r == h, x[:, h:h + 1], out)
    return out


def _bias8(d, tab_ref, head0):
    bs = _rel_bias(d, tab_ref, range(head0, head0 + N_HEADS))
    r = _iota((8, d.shape[1]), 0)
    out = jnp.zeros((8, d.shape[1]), F32)
    for h in range(N_HEADS):
        out = jnp.where(r == h, bs[h], out)
    return out


def _softmax_rows(lg, valid):
    lg = jnp.where(valid, lg, -jnp.inf)
    mx = jnp.max(lg, axis=-1, keepdims=True)
    mx = jnp.where(mx == -jnp.inf, 0.0, mx)
    p = jnp.exp(lg - mx)
    return p / jnp.maximum(jnp.sum(p, axis=-1, keepdims=True), 1e-30)


def _write_heads(o_ref, o8, w):
    for h in range(N_HEADS):
        o_ref[0, :, h * w:(h + 1) * w] = o8[h:h + 1, :]


def _nsa_cmp_step_kernel(pt_ref, tab_ref, q_ref, *rest, pg, past):
    pages, (o_ref, idx_ref, cmp_s) = rest[:pg], rest[pg:]
    j = pl.program_id(1)
    nsel = past // NSA_SEL_BLOCK
    per_page = PAGE_SIZE // NSA_CMP_BLOCK
    nb = per_page * pg
    half = nb // 2
    nsteps = cmp_s.shape[0]
    key = _iota((PAGE_SIZE, nb), 0)
    col = _iota((PAGE_SIZE, nb), 1)
    acc = jnp.zeros((2 * HEAD_DIM, nb), F32)
    for k in range(pg):
        g = per_page * k + key // NSA_CMP_BLOCK
        pool = jnp.where(col == (g % 2) * half + g // 2, 1.0 / NSA_CMP_BLOCK, 0.0).astype(BF16)
        acc = acc + jnp.dot(pages[k][...].astype(BF16), pool, preferred_element_type=F32)
    cmp_s[j] = acc

    @pl.when(j == pl.num_programs(1) - 1)
    def _():
        q8 = _rows8(q_ref[0], HEAD_DIM)
        cc = _iota((1, nb), 1)
        ps_all, o_cmp = [], None
        lgs = []
        for s in range(nsteps):
            sel = s * half + jnp.where(cc < half, cc, cc - half)
            c_end = NSA_SEL_BLOCK * sel + jnp.where(cc < half, NSA_CMP_BLOCK - 1, NSA_SEL_BLOCK - 1)
            lgs.append(_dot(q8, cmp_s[s, 0:HEAD_DIM, :]) * HEAD_DIM ** -0.5 + _bias8(past - c_end, tab_ref, 0))
        lg = jnp.concatenate(lgs, axis=-1)
        p = _softmax_rows(lg, True)
        for s in range(nsteps):
            term = _dot_nt(p[:, s * nb:(s + 1) * nb], cmp_s[s, HEAD_DIM:2 * HEAD_DIM, :])
            o_cmp = term if o_cmp is None else o_cmp + term
            ps = p[0:1, s * nb:(s + 1) * nb] + p[1:2, s * nb:(s + 1) * nb] + p[2:3, s * nb:(s + 1) * nb] + p[3:4, s * nb:(s + 1) * nb]
            ps_all.append(ps[:, 0:half] + ps[:, half:nb])
        _write_heads(o_ref, o_cmp, HEAD_DIM)
        idx_ref[0] = jnp.concatenate(ps_all, axis=-1)


def _nsa_pick_kernel(imp_ref, idx_ref, *, n_pick):
    imp = imp_ref[...]
    bsz, nsel = imp.shape
    blk = _iota((1, nsel), 1)
    imp = jnp.where(blk == 0, -jnp.inf, imp)
    lane = _iota((1, LANES), 1)
    idx = jnp.where(lane == n_pick + 1, nsel, jnp.zeros((bsz, LANES), I32))
    for s in range(n_pick):
        top = jnp.max(imp, axis=-1, keepdims=True)
        first = jnp.min(jnp.where(imp == top, blk, nsel), axis=-1, keepdims=True)
        idx = jnp.where(lane == s + 1, first, idx)
        imp = jnp.where(blk == first, -jnp.inf, imp)
    idx_ref[...] = idx


def _nsa_pick(imp, n_pick):
    bsz = imp.shape[0]
    return pl.pallas_call(
        functools.partial(_nsa_pick_kernel, n_pick=n_pick),
        out_shape=jax.ShapeDtypeStruct((bsz, LANES), I32),
        in_specs=[pl.BlockSpec(memory_space=pltpu.VMEM)],
        out_specs=pl.BlockSpec(memory_space=pltpu.VMEM),
        name="nsa_pick",
    )(imp)


def _nsa_cmp_step(page_table, rel_table, q3, cache_t, layer, pg, past):
    bsz, n_pages = page_table.shape
    nsel = past // NSA_SEL_BLOCK
    n_pick = min(NSA_TOP_N, nsel + 1) - 2
    nb = pg * (PAGE_SIZE // NSA_CMP_BLOCK)
    grid_spec = pltpu.PrefetchScalarGridSpec(
        num_scalar_prefetch=1,
        grid=(bsz, n_pages // pg),
        in_specs=[pl.BlockSpec(memory_space=pltpu.SMEM),
                  pl.BlockSpec((1, 1, BRANCH_W), lambda b, j, pt: (b, 0, 0))]
                 + _page_specs(layer, pg, 2 * HEAD_DIM, 0),
        out_specs=(pl.BlockSpec((1, 1, BRANCH_W), lambda b, j, pt: (b, 0, 0)),
                   pl.BlockSpec((1, 1, nsel), lambda b, j, pt: (b, 0, 0))),
        scratch_shapes=[pltpu.VMEM((n_pages // pg, 2 * HEAD_DIM, nb), F32)])
    o_cmp, imp = pl.pallas_call(
        functools.partial(_nsa_cmp_step_kernel, pg=pg, past=past),
        out_shape=(jax.ShapeDtypeStruct((bsz, 1, BRANCH_W), F32), jax.ShapeDtypeStruct((bsz, 1, nsel), F32)),
        grid_spec=grid_spec,
        compiler_params=_cp(("arbitrary", "arbitrary")),
        name="nsa_cmp_step",
    )(page_table, rel_table, q3, *([cache_t] * pg))
    return o_cmp, _nsa_pick(imp.reshape(bsz, nsel), n_pick)


def _nsa_sel_step_kernel(idx_ref, pt_ref, tab_ref, q_ref, sm_ref, kv_ref, nw_ref, ocmp_ref, win_ref, *rest,
                         n_past, past):
    blocks, (o_ref,) = rest[:n_past], rest[n_past:]
    b = pl.program_id(0)
    sb = NSA_SEL_BLOCK
    per_page = PAGE_SIZE // sb
    scale = HEAD_DIM ** -0.5
    q8 = _rows8(q_ref[0], HEAD_DIM)
    zero_d = jnp.zeros((1, 1), I32)

    row = _iota((1, PAGE_SIZE), 1)
    lgs, valids = [], []
    for k in range(n_past):
        blk = idx_ref[b, k]
        pos = (blk // per_page) * PAGE_SIZE + row
        lgs.append(_dot(q8, blocks[k][0:HEAD_DIM, :]) * scale + _bias8(past - pos, tab_ref, 0))
        valids.append(row // sb == blk % per_page)
    new_kv = kv_ref[0]
    lg_self = _self_logit(q8, new_kv[:, 2 * HEAD_DIM:3 * HEAD_DIM]) * scale + _bias8(zero_d, tab_ref, 0)
    p, p_self = _softmax_with_self(jnp.concatenate(lgs, axis=-1), jnp.concatenate(valids, axis=-1), lg_self)
    o_sel = p_self * new_kv[:, 3 * HEAD_DIM:4 * HEAD_DIM]
    for k in range(n_past):
        o_sel = o_sel + _dot_nt(p[:, k * PAGE_SIZE:(k + 1) * PAGE_SIZE], blocks[k][HEAD_DIM:2 * HEAD_DIM, :])

    wb = win_ref.shape[1]
    d = wb - _iota((1, wb), 1)
    lg = _dot(q8, win_ref[0:HEAD_DIM, :]) * scale + _bias8(d, tab_ref, 0)
    new_win = nw_ref[0]
    lg_self = _self_logit(q8, new_win[:, 0:HEAD_DIM]) * scale + _bias8(zero_d, tab_ref, 0)
    p, p_self = _softmax_with_self(lg, (d <= NSA_WINDOW) & (past - d >= 0), lg_self)
    o_win = _dot_nt(p, win_ref[HEAD_DIM:2 * HEAD_DIM, :]) + p_self * new_win[:, HEAD_DIM:2 * HEAD_DIM]

    g = _sigmoid(sm_ref[0][:, SM_NSAG:SM_NSAG + 3 * N_HEADS])
    for h in range(N_HEADS):
        hs = slice(h * HEAD_DIM, (h + 1) * HEAD_DIM)
        o_ref[0, :, hs] = (g[:, 3 * h:3 * h + 1] * ocmp_ref[0][:, hs] + g[:, 3 * h + 1:3 * h + 2] * o_sel[h:h + 1, :]
                           + g[:, 3 * h + 2:3 * h + 3] * o_win[h:h + 1, :])


def _nsa_sel_step(idx, page_table, rel_table, q3, sm3, kv3, nw3, ocmp3, win_t, cache_t, layer, past):
    bsz = page_table.shape[0]
    nsel = past // NSA_SEL_BLOCK
    n_past = min(NSA_TOP_N, nsel + 1) - 1
    per_page = PAGE_SIZE // NSA_SEL_BLOCK
    wb = win_t.shape[3]
    tok = lambda w: pl.BlockSpec((1, 1, w), lambda b, idx, pt: (b, 0, 0))

    def blk_spec(k):
        return pl.BlockSpec((None, None, 2 * HEAD_DIM, PAGE_SIZE),
                            lambda b, idx, pt: (layer, pt[b, idx[b, k] // per_page], 1, 0))
    grid_spec = pltpu.PrefetchScalarGridSpec(
        num_scalar_prefetch=2,
        grid=(bsz,),
        in_specs=[pl.BlockSpec(memory_space=pltpu.SMEM), tok(BRANCH_W), tok(LANES), tok(4 * HEAD_DIM), tok(2 * HEAD_DIM),
                  tok(BRANCH_W),
                  pl.BlockSpec((None, None, 2 * HEAD_DIM, wb), lambda b, idx, pt: (layer, b, 0, 0))]
                 + [blk_spec(k) for k in range(n_past)],
        out_specs=tok(BRANCH_W))
    return pl.pallas_call(
        functools.partial(_nsa_sel_step_kernel, n_past=n_past, past=past),
        out_shape=jax.ShapeDtypeStruct((bsz, 1, BRANCH_W), F32),
        grid_spec=grid_spec,
        compiler_params=_cp(("arbitrary",)),
        name="nsa_sel_step",
    )(idx, page_table, rel_table, q3, sm3, kv3, nw3, ocmp3, win_t, *([cache_t] * n_past))


def _index_weights(sm):
    return _col8(sm[:, SM_IDXW:SM_IDXW + IDX_HEADS])


def _dsa_score_step_kernel(pt_ref, iq_ref, sm_ref, *rest, pg):
    pages, (sc_ref, new_ref) = rest[:pg], rest[pg:]
    qi8 = _rows8(iq_ref[0], IDX_DIM)
    sm = sm_ref[0]
    wcol = _index_weights(sm)
    cst = IDX_DIM ** -0.5 * IDX_HEADS ** -0.5
    rows = []
    for k in range(pg):
        s = jnp.maximum(_dot(qi8, pages[k][...]), 0.0)
        rows.append(jnp.sum(s * wcol, axis=0, keepdims=True) * cst)
    sc_ref[0] = jnp.concatenate(rows, axis=0)
    s_new = jnp.maximum(_self_logit(qi8, sm[:, SM_IDXK:SM_IDXK + IDX_DIM]), 0.0)
    sc_new = jnp.sum(s_new * wcol, axis=0, keepdims=True) * cst
    new_ref[0] = jnp.where(_iota((1, LANES), 1) == 0, sc_new, -jnp.inf)


def _dsa_score_step(page_table, iq3, sm3, cache_idx_t, layer, pg):
    bsz, n_pages = page_table.shape
    tok = lambda w: pl.BlockSpec((1, 1, w), lambda b, j, pt: (b, 0, 0))
    grid_spec = pltpu.PrefetchScalarGridSpec(
        num_scalar_prefetch=1,
        grid=(bsz, n_pages // pg),
        in_specs=[tok(IDX_HEADS * IDX_DIM), tok(LANES)] + _page_specs(layer, pg, IDX_DIM, 0),
        out_specs=(pl.BlockSpec((1, pg, PAGE_SIZE), lambda b, j, pt: (b, j, 0)), tok(LANES)))
    return pl.pallas_call(
        functools.partial(_dsa_score_step_kernel, pg=pg),
        out_shape=(jax.ShapeDtypeStruct((bsz, n_pages, PAGE_SIZE), F32), jax.ShapeDtypeStruct((bsz, 1, LANES), F32)),
        grid_spec=grid_spec,
        compiler_params=_cp(("arbitrary", "arbitrary")),
        name="dsa_score_step",
    )(page_table, iq3, sm3, *([cache_idx_t] * pg))


def _dsa_thr_step_kernel(sc_ref, new_ref, thr_ref, need_ref, tie_ref, *, topk):
    key = _ordered_key(sc_ref[...])
    key_new = _ordered_key(new_ref[...])
    bt = key.shape[0]

    def count(pred):
        return (jnp.sum(jnp.where(pred(key), 1, 0), axis=-1, keepdims=True)
                + jnp.sum(jnp.where(pred(key_new), 1, 0), axis=-1, keepdims=True))

    thr = _kth_largest_key(lambda cand: count(lambda x: x >= cand), topk, (bt, 1))
    need = topk - count(lambda x: x > thr)
    thr_ref[...] = jnp.broadcast_to(thr, thr_ref.shape)
    need_ref[...] = jnp.broadcast_to(need, need_ref.shape)
    tie_ref[...] = jnp.broadcast_to(jnp.where(count(lambda x: x == thr) > need, 1, 0), tie_ref.shape)


def _dsa_thr_step(scores, new, topk, bt):
    bsz, p = scores.shape
    out = pl.BlockSpec((bt, LANES), lambda i: (i, 0))
    return pl.pallas_call(
        functools.partial(_dsa_thr_step_kernel, topk=topk),
        out_shape=(jax.ShapeDtypeStruct((bsz, LANES), I32),) * 3,
        grid=(bsz // bt,),
        in_specs=[pl.BlockSpec((bt, p), lambda i: (i, 0)), pl.BlockSpec((bt, LANES), lambda i: (i, 0))],
        out_specs=(out, out, out),
        compiler_params=_cp(("arbitrary",)),
        name="dsa_thr_step",
    )(scores, new)


def _online_self(lg_self, v_row, m_s, l_s, acc_s):
    m_old = m_s[...]
    m_new = jnp.maximum(m_old, lg_self)
    m_safe = jnp.where(m_new == -jnp.inf, 0.0, m_new)
    alpha = jnp.exp(m_old - m_safe)
    p_self = jnp.exp(lg_self - m_safe)
    den = alpha * l_s[...] + p_self
    return (alpha * acc_s[...] + p_self * v_row) / jnp.maximum(den, 1e-30)


def _dsa_att_step_kernel(pt_ref, tie_ref, tab_ref, q_ref, kv_ref, sc_ref, new_ref, thr_ref, need_ref, *rest, pg, past):
    pages, (o_ref, m_s, l_s, acc_s, run_s) = rest[:pg], rest[pg:]
    b = pl.program_id(0)
    j = pl.program_id(1)
    scale = HEAD_DIM ** -0.5
    nk = pg * PAGE_SIZE

    @pl.when(j == 0)
    def _():
        _flash_init(m_s, l_s, acc_s)
        run_s[...] = jnp.zeros(run_s.shape, F32)

    q8 = _rows8(q_ref[0], HEAD_DIM)
    thr = thr_ref[0][:, 0:1]
    need = need_ref[0][:, 0:1].astype(F32)
    has_ties = tie_ref[b] > 0
    key = _ordered_key(sc_ref[0])

    def pick_plain(run):
        return jnp.where(key >= thr, 1.0, 0.0), run

    def pick_ties(run):
        eq = key == thr
        eqf = jnp.where(eq, 1.0, 0.0)
        strict_upper = (_iota((PAGE_SIZE, PAGE_SIZE), 0) < _iota((PAGE_SIZE, PAGE_SIZE), 1)).astype(BF16)
        inside = jnp.dot(eqf.astype(BF16), strict_upper, preferred_element_type=F32)
        cnt = jnp.sum(eqf, axis=-1, keepdims=True)
        rows = []
        for k in range(pg):
            rows.append(inside[k:k + 1] + run)
            run = run + cnt[k:k + 1]
        before = jnp.concatenate(rows, axis=0)
        return jnp.where((key > thr) | (eq & (before < need)), 1.0, 0.0), run

    chosen, run = lax.cond(has_ties, pick_ties, pick_plain, run_s[...])
    run_s[...] = run

    lg = jnp.concatenate([_dot(q8, pages[k][0:HEAD_DIM, :]) for k in range(pg)], axis=-1) * scale
    mask = jnp.concatenate([chosen[k:k + 1] for k in range(pg)], axis=-1) > 0.5
    r8 = _iota((8, 1), 0)
    far = jnp.zeros((8, 1), F32)
    for h in range(N_HEADS):
        far = jnp.where(r8 == h, tab_ref[REL_BUCKETS - 1, N_HEADS + h], far)
    d0 = past - j * nk
    bias = lax.cond(d0 - (nk - 1) >= REL_MAX_DIST,
                    lambda: jnp.broadcast_to(far, (8, nk)),
                    lambda: _bias8(d0 - _iota((1, nk), 1), tab_ref, N_HEADS))
    p, alpha = _softmax_step(lg + bias, mask, m_s, l_s)
    acc = alpha * acc_s[...]
    for k in range(pg):
        acc = acc + _dot_nt(p[:, k * PAGE_SIZE:(k + 1) * PAGE_SIZE], pages[k][HEAD_DIM:2 * HEAD_DIM, :])
    acc_s[...] = acc

    @pl.when(j == pl.num_programs(1) - 1)
    def _():
        key_new = _ordered_key(new_ref[0][:, 0:1])
        limit = jnp.where(has_ties, need, jnp.float32(3.0e38))
        take = (key_new > thr) | ((key_new == thr) & (run < limit))
        new_kv = kv_ref[0]
        lg_self = _self_logit(q8, new_kv[:, 0:HEAD_DIM]) * scale + _bias8(jnp.zeros((1, 1), I32), tab_ref, N_HEADS)
        lg_self = jnp.where(take, lg_self, -jnp.inf)
        _write_heads(o_ref, _online_self(lg_self, new_kv[:, HEAD_DIM:2 * HEAD_DIM], m_s, l_s, acc_s), HEAD_DIM)


def _dsa_att_step(page_table, tie, rel_table, q3, kv3, scores, new, thr, need, cache_t, layer, pg, past):
    bsz, n_pages = page_table.shape
    tok = lambda w: pl.BlockSpec((1, 1, w), lambda b, j, pt, tie: (b, 0, 0))
    grid_spec = pltpu.PrefetchScalarGridSpec(
        num_scalar_prefetch=2,
        grid=(bsz, n_pages // pg),
        in_specs=[pl.BlockSpec(memory_space=pltpu.SMEM), tok(BRANCH_W), tok(2 * HEAD_DIM),
                  pl.BlockSpec((1, pg, PAGE_SIZE), lambda b, j, pt, tie: (b, j, 0)), tok(LANES), tok(LANES), tok(LANES)]
                 + _page_specs(layer, pg, 2 * HEAD_DIM, 0),
        out_specs=tok(BRANCH_W),
        scratch_shapes=[pltpu.VMEM((8, 1), F32), pltpu.VMEM((8, 1), F32), pltpu.VMEM((8, HEAD_DIM), F32),
                        pltpu.VMEM((1, 1), F32)])
    return pl.pallas_call(
        functools.partial(_dsa_att_step_kernel, pg=pg, past=past),
        out_shape=jax.ShapeDtypeStruct((bsz, 1, BRANCH_W), F32),
        grid_spec=grid_spec,
        compiler_params=_cp(("arbitrary", "arbitrary")),
        name="dsa_att_step",
    )(page_table, tie, rel_table, q3, kv3, scores, new, thr, need, *([cache_t] * pg))


def _mla_step_kernel(pt_ref, qc_ref, kc_ref, wuv_ref, *rest, pg):
    pages, (o_ref, m_s, l_s, acc_s) = rest[:pg], rest[pg:]
    j = pl.program_id(1)
    scale = MLA_HEAD_QK ** -0.5
    kw = KV_LORA + QK_ROPE

    @pl.when(j == 0)
    def _():
        _flash_init(m_s, l_s, acc_s)

    q8 = _rows8(qc_ref[0], MLA_QW)[:, 0:kw]
    lg = jnp.concatenate([_dot(q8, pages[k][...]) for k in range(pg)], axis=-1) * scale
    p, alpha = _softmax_step(lg, True, m_s, l_s)
    acc = alpha * acc_s[...]
    for k in range(pg):
        acc = acc + _dot_nt(p[:, k * PAGE_SIZE:(k + 1) * PAGE_SIZE], pages[k][0:KV_LORA, :])
    acc_s[...] = acc

    @pl.when(j == pl.num_programs(1) - 1)
    def _():
        new_k = kc_ref[0].astype(F32)
        lg_self = _self_logit(q8, new_k[:, 0:kw]) * scale
        o_lat = _online_self(lg_self, new_k[:, 0:KV_LORA], m_s, l_s, acc_s)
        for h in range(N_HEADS):
            o_ref[0, :, h * V_DIM:(h + 1) * V_DIM] = _dot(o_lat, wuv_ref[h])[h:h + 1, :]


def _mla_step(page_table, qc3, kc3, wuv, cache_mla, layer, pg):
    bsz, n_pages = page_table.shape
    tok = lambda w: pl.BlockSpec((1, 1, w), lambda b, j, pt: (b, 0, 0))
    grid_spec = pltpu.PrefetchScalarGridSpec(
        num_scalar_prefetch=1,
        grid=(bsz, n_pages // pg),
        in_specs=[tok(N_HEADS * MLA_QW), tok(MLA_QW), pl.BlockSpec(wuv.shape, lambda b, j, pt: (0, 0, 0))]
                 + _page_specs(layer, pg, KV_LORA + QK_ROPE, 0),
        out_specs=tok(BRANCH_W),
        scratch_shapes=[pltpu.VMEM((8, 1), F32), pltpu.VMEM((8, 1), F32), pltpu.VMEM((8, KV_LORA), F32)])
    return pl.pallas_call(
        functools.partial(_mla_step_kernel, pg=pg),
        out_shape=jax.ShapeDtypeStruct((bsz, 1, BRANCH_W), F32),
        grid_spec=grid_spec,
        compiler_params=_cp(("arbitrary", "arbitrary")),
        name="mla_step",
    )(page_table, qc3, kc3, wuv, *([cache_mla] * pg))


def _permute_w_in(w_in):
    d = w_in.shape[0]
    o = {}
    off = 0
    for name, w in (('nsa_q', 256), ('nsa_kv', 384), ('nsa_g', 12), ('dsa_q', 256), ('dsa_kv', 128), ('idx_q', 256),
                    ('idx_k', 64), ('idx_w', 4), ('hg', 1024), ('mla_cq', 256), ('mla_ckv', 128), ('mla_kr', 32),
                    ('merge_g', w_in.shape[1] - 2800)):
        o[name] = w_in[:, off:off + w]
        off += w
    pad = jnp.zeros((d, LANES - 112), w_in.dtype)
    small = [o['idx_k'], o['mla_kr'], o['nsa_g'], o['idx_w'], pad]
    cols = [o['nsa_q'], o['nsa_kv'], o['dsa_q'], o['dsa_kv'], o['idx_q'], o['hg'], o['mla_cq'], o['mla_ckv']] + small \
           + [o['merge_g']]
    w_all = jnp.concatenate(cols, axis=1).astype(BF16)
    w_row = jnp.concatenate([o['nsa_kv'], o['dsa_kv'], o['hg'], o['mla_cq'], o['mla_ckv']] + small + [o['merge_g']],
                            axis=1).astype(BF16)
    hd = HEAD_DIM
    v_cols = [o['nsa_kv'][:, hd:2 * hd], o['nsa_kv'][:, 3 * hd:4 * hd], o['nsa_kv'][:, 5 * hd:6 * hd], o['dsa_kv'][:, hd:2 * hd]]
    w_fm = jnp.concatenate([o['nsa_q'], o['dsa_q'], o['idx_q']] + v_cols + [o['nsa_g'], o['idx_w']], axis=1).T.astype(BF16)
    return w_all, w_row, w_fm


def _layer_weights(l, w):
    hq = MLA_HEAD_QK
    wuq = w['w_uq'][l].reshape(Q_LORA, N_HEADS, hq)
    hr = QK_ROPE // 2
    w_all, w_row, w_fm = _permute_w_in(w['w_in'][l])
    return dict(
        w_in=w_all, w_row=w_row, w_fm=w_fm,
        wuvT=jnp.transpose(w['w_uv'][l], (1, 2, 0)).astype(BF16),
        wn=wuq[:, :, :QK_NOPE].reshape(Q_LORA, N_HEADS * QK_NOPE).astype(BF16),
        wr1=wuq[:, :, QK_NOPE:QK_NOPE + hr].reshape(Q_LORA, N_HEADS * hr).astype(BF16),
        wr2=wuq[:, :, QK_NOPE + hr:].reshape(Q_LORA, N_HEADS * hr).astype(BF16),
        wukT=jnp.transpose(w['w_uk'][l], (1, 2, 0)).astype(BF16),
        wuv=jnp.transpose(w['w_uv'][l], (1, 0, 2)).astype(BF16),
        wb=w['w_branch'][l].astype(BF16),
        wo=w['w_out'][l].astype(BF16),
        w_up=w['w_up'][l].astype(BF16),
        w_down=w['w_down'][l].astype(BF16),
        conv_w=w['conv_w'][l],
        conv_b=w['conv_b'][l][None, :],
        g_pre_mix=w['g_pre_mix'][l][None, :], g_post_mix=w['g_post_mix'][l][None, :],
        g_pre_ffn=w['g_pre_ffn'][l][None, :], g_post_ffn=w['g_post_ffn'][l][None, :],
        gq=w['mla_q_norm'][l][None, :], gkv=w['mla_kv_norm'][l][None, :],
        hgn=jnp.tile(w['hg_norm'][l], N_HEADS)[None, :],
    )


def _largest_divisor(n, cap):
    for c in range(min(n, cap), 0, -1):
        if n % c == 0:
            return c
    return 1


def kernel(x_prompt, x_sample, cache_nsa_kv, cache_dsa_kv, cache_dsa_idx, cache_mla, state_nsa_win, state_hgrn, state_ffn_conv, page_table, c_prompt, c_sample, rel_table, w_ada, b_ada, g_pre_mix, g_post_mix, g_pre_ffn, g_post_ffn, w_in, hg_lb, hg_norm, mla_q_norm, mla_kv_norm, w_uq, w_uk, w_uv, w_branch, w_out, w_up, conv_w, conv_b, w_down):
    weights = dict(w_in=w_in, w_uq=w_uq, w_uk=w_uk, w_uv=w_uv, w_branch=w_branch, w_out=w_out, w_up=w_up,
                   w_down=w_down, conv_w=conv_w, conv_b=conv_b, g_pre_mix=g_pre_mix, g_post_mix=g_post_mix,
                   g_pre_ffn=g_pre_ffn, g_post_ffn=g_post_ffn, mla_q_norm=mla_q_norm, mla_kv_norm=mla_kv_norm,
                   hg_norm=hg_norm)
    depth = w_in.shape[0]
    batch, seq, d = x_prompt.shape
    dec, dec_seq, _ = x_sample.shape
    assert dec_seq == 1 and seq % 256 == 0 and dec % 8 == 0
    n_pool = cache_nsa_kv.shape[1]
    n_pages = page_table.shape[1]
    past = n_pages * PAGE_SIZE
    dff = w_down.shape[1]

    tq = 256
    tm_p = 256
    tps_p = seq // tm_p
    tm_s = _largest_divisor(dec, 128)
    pg = _largest_divisor(n_pages, 16)
    pg_cmp = _largest_divisor(n_pages, 32)

    gam = jax.nn.softmax(hg_lb.astype(F32), axis=0)
    cum = jnp.cumsum(gam, axis=0)
    lower = cum - cum[0]

    inv = ROPE_THETA ** (-jnp.arange(0, QK_ROPE, 2, dtype=F32) / QK_ROPE)
    inv4 = jnp.tile(inv, N_HEADS)[None, :]

    nsel = seq // NSA_SEL_BLOCK
    jj = jnp.arange(2 * nsel, dtype=I32)
    cend = jnp.where(jj < nsel, NSA_SEL_BLOCK * jj + (NSA_CMP_BLOCK - 1),
                     NSA_SEL_BLOCK * (jj - nsel) + (NSA_SEL_BLOCK - 1))[:, None]
    bias_t, bias_cmp = _bias_tiles(rel_table, cend, tq, tq, seq)

    mod = _ada(jnp.concatenate([c_prompt, c_sample], axis=0), w_ada, b_ada)

    nsa_t = jnp.transpose(cache_nsa_kv, (0, 1, 3, 4, 2)).reshape(depth, n_pool, 4 * HEAD_DIM, PAGE_SIZE)
    dsa_t = jnp.transpose(cache_dsa_kv, (0, 1, 3, 4, 2)).reshape(depth, n_pool, 2 * HEAD_DIM, PAGE_SIZE)
    idx_t = jnp.transpose(cache_dsa_idx, (0, 1, 3, 2))
    mla_t = jnp.transpose(cache_mla, (0, 1, 3, 2))
    win_t = jnp.transpose(state_nsa_win, (0, 1, 3, 4, 2)).reshape(depth, dec, 2 * HEAD_DIM, state_nsa_win.shape[2])

    xp = x_prompt.reshape(batch * seq, d)
    xs = x_sample.reshape(dec, d)
    outs_p, outs_s = [], []
    for l in range(depth):
        lw = _layer_weights(l, weights)
        lower_l = lower[l][None, :]
        mp = [mod[l, :batch, k * d:(k + 1) * d][:, None, :] for k in range(6)]
        ms = [mod[l, batch:, k * d:(k + 1) * d][None, :, :] for k in range(6)]

        z = dict(zip(ROW_NAMES + FM_NAMES,
                     _inproj_fm(xp, lw['g_pre_mix'], mp[0], mp[1], lw['w_row'], lw['w_fm'], tm_p, tps_p)))
        qc_t, kc, mla_st, ckv_t = _mla_proj(z['mla_cq'], z['mla_ckv'], z['small'], inv4, lw['gq'], lw['gkv'],
                                            lw['wn'], lw['wr1'], lw['wr2'], lw['wukT'], tm_p, seq, 0, feature_major=True)
        o_nsa = _nsa_prompt(z['nsa_qT'], z['smallT'], z['nsa_kv4'], z['nsa_win'], z['vT'], bias_t, bias_cmp,
                            batch, seq, tq)
        o_dsa = _dsa_prompt(z['dsa_qT'], z['idx_qT'], z['smallT'], z['small'], z['dsa_kv'], z['vT'], bias_t,
                            batch, seq, tq)
        o_mla = _mla_prompt(qc_t, kc, ckv_t, lw['wuvT'], batch, seq, tq)
        o_hg, s_new = _hgrn_prompt(z['hg'], lower_l, lw['hgn'], batch, seq, 256)
        xp = _merge(xp, (o_nsa, o_dsa, o_hg, o_mla), z['merge_g'], mp[2], lw['g_post_mix'], lw['wb'], lw['wo'], tm_p, tps_p)
        xp, csg, csv = _ffn(xp, lw['g_pre_ffn'], mp[3], mp[4], mp[5], lw['g_post_ffn'], lw['w_up'], lw['conv_w'],
                            lw['conv_b'], lw['w_down'], tm_p, tps_p)
        wl = min(NSA_WINDOW, seq)
        outs_p.append((z['nsa_kv4'].reshape(batch, seq, 4, HEAD_DIM),
                       z['dsa_kv'].reshape(batch, seq, 2, HEAD_DIM),
                       z['small'][:, SM_IDXK:SM_IDXK + IDX_DIM].reshape(batch, seq, IDX_DIM),
                       mla_st.reshape(batch, seq, KV_LORA + QK_ROPE),
                       z['nsa_win'].reshape(batch, seq, 2, HEAD_DIM)[:, seq - wl:],
                       s_new,
                       jnp.concatenate([jnp.swapaxes(csg, 1, 2).reshape(batch, CONV_W - 1, dff),
                                        jnp.swapaxes(csv, 1, 2).reshape(batch, CONV_W - 1, dff)], axis=-1)))

        z = dict(zip(OUT_NAMES, _inproj(xs, lw['g_pre_mix'], ms[0], ms[1], lw['w_in'], tm_s, 1)))
        qc, kc, mla_st = _mla_proj(z['mla_cq'], z['mla_ckv'], z['small'], inv4, lw['gq'], lw['gkv'],
                                   lw['wn'], lw['wr1'], lw['wr2'], lw['wukT'], tm_s, 1, past)
        r3 = lambda a: a.reshape(dec, 1, a.shape[-1])
        q3, sm3, kv3, nw3 = r3(z['nsa_q']), r3(z['small']), r3(z['nsa_kv4']), r3(z['nsa_win'])
        o_cmp, sel_idx = _nsa_cmp_step(page_table, rel_table, q3, nsa_t, l, pg_cmp, past)
        o_nsa = _nsa_sel_step(sel_idx, page_table, rel_table, q3, sm3, kv3, nw3, o_cmp, win_t,
                              nsa_t, l, past)
        scores, sc_new = _dsa_score_step(page_table, r3(z['idx_q']), sm3, idx_t, l, pg)
        thr, need, tie = _dsa_thr_step(scores.reshape(dec, past), sc_new.reshape(dec, LANES),
                                       min(DSA_TOPK_MAX, (past + 1) // 4), 8)
        o_dsa = _dsa_att_step(page_table, tie[:, 0], rel_table, r3(z['dsa_q']), r3(z['dsa_kv']), scores, sc_new,
                              r3(thr), r3(need), dsa_t, l, pg, past)
        o_mla = _mla_step(page_table, r3(qc), r3(kc), lw['wuv'], mla_t, l, pg)
        o_hg, s_new = _hgrn_step(z['hg'], lower_l, lw['hgn'], state_hgrn[l], 8)
        xs = _merge(xs, (o_nsa.reshape(dec, BRANCH_W), o_dsa.reshape(dec, BRANCH_W), o_hg, o_mla.reshape(dec, BRANCH_W)),
                    z['merge_g'], ms[2], lw['g_post_mix'], lw['wb'], lw['wo'], tm_s, 1)
        prev = state_ffn_conv[l]
        xs, ug, uv = _ffn(xs, lw['g_pre_ffn'], ms[3], ms[4], ms[5], lw['g_post_ffn'], lw['w_up'], lw['conv_w'],
                          lw['conv_b'], lw['w_down'], tm_s, 1, prev=(prev[:, 0], prev[:, 1]))
        win_all = jnp.concatenate([state_nsa_win[l], z['nsa_win'].reshape(dec, 1, 2, HEAD_DIM)], axis=1)
        wl = min(NSA_WINDOW, win_all.shape[1])
        outs_s.append((z['nsa_kv4'].reshape(dec, 1, 4, HEAD_DIM),
                       z['dsa_kv'].reshape(dec, 1, 2, HEAD_DIM),
                       z['small'][:, SM_IDXK:SM_IDXK + IDX_DIM].reshape(dec, 1, IDX_DIM),
                       mla_st.reshape(dec, 1, KV_LORA + QK_ROPE),
                       win_all[:, win_all.shape[1] - wl:],
                       s_new,
                       jnp.stack([prev[:, 1], jnp.concatenate([ug, uv], axis=-1)], axis=1)))

    sp = [jnp.stack(v) for v in zip(*outs_p)]
    ss = [jnp.stack(v) for v in zip(*outs_s)]
    return (xp.reshape(batch, seq, d), xs.reshape(dec, 1, d),
            sp[0], ss[0], sp[1], ss[1], sp[2], ss[2], sp[3], ss[3], sp[4], ss[4], sp[5], ss[5], sp[6], ss[6])
```

```python
import functools
import math

import jax
import jax.numpy as jnp
from jax import lax
from jax.experimental import pallas as pl
from jax.experimental.pallas import tpu as pltpu

F32, BF16, I32 = jnp.float32, jnp.bfloat16, jnp.int32

N_HEADS = 4
HEAD_DIM = 64
BRANCH_W = N_HEADS * HEAD_DIM
N_BRANCH = 4
NSA_CMP_BLOCK = 32
NSA_SEL_BLOCK = 64
NSA_TOP_N = 16
NSA_WINDOW = 512
FORCE_SCORE = 1e9
IDX_HEADS = 4
IDX_DIM = 64
DSA_TOPK_MAX = 256
HGRN_CHUNK = 16
Q_LORA = 256
KV_LORA = 128
QK_NOPE = 64
QK_ROPE = 32
V_DIM = 64
MLA_HEAD_QK = QK_NOPE + QK_ROPE
ROPE_THETA = 10000.0
REL_BUCKETS = 32
REL_MAX_DIST = 512
CONV_W = 3
EPS = 1e-6
PAGE_SIZE = 128

LANES = 128
VMEM_LIMIT = 56 * 1024 * 1024

OUT_WIDTHS = (256, 256, 128, 256, 128, 256, 1024, 256, 128, 128, 4096)
OUT_NAMES = ('nsa_q', 'nsa_kv4', 'nsa_win', 'dsa_q', 'dsa_kv', 'idx_q', 'hg', 'mla_cq', 'mla_ckv', 'small', 'merge_g')
SM_IDXK, SM_KR, SM_NSAG, SM_IDXW = 0, 64, 96, 108
MLA_QW = 256
INT_MIN = -2 ** 31


def _cp(sem, vmem=VMEM_LIMIT):
    return pltpu.CompilerParams(dimension_semantics=sem, vmem_limit_bytes=vmem)


def _dot(a, b):
    return jnp.dot(a.astype(BF16), b.astype(BF16), preferred_element_type=F32)


def _dot_nt(a, b):
    return lax.dot_general(a.astype(BF16), b.astype(BF16), (((1,), (1,)), ((), ())), preferred_element_type=F32)


def _dot_tn(a, b):
    return lax.dot_general(a.astype(BF16), b.astype(BF16), (((0,), (0,)), ((), ())), preferred_element_type=F32)


def _dot_hilo(a, g):
    hi = a.astype(BF16)
    lo = (a - hi.astype(F32)).astype(BF16)
    return jnp.dot(hi, g, preferred_element_type=F32) + jnp.dot(lo, g, preferred_element_type=F32)


def _rms(x, g):
    return x * lax.rsqrt(jnp.mean(x * x, axis=-1, keepdims=True) + EPS) * g


def _sigmoid(x):
    return 1.0 / (1.0 + jnp.exp(-x))


def _silu(x):
    return x * _sigmoid(x)


def _iota(shape, dim):
    return lax.broadcasted_iota(I32, shape, dim)


def _head_block_ones(n):
    return (_iota((n, n), 0) // HEAD_DIM == _iota((n, n), 1) // HEAD_DIM)


def _rel_bias(d, tab_ref, heads):
    exact = REL_BUCKETS // 2
    n = jnp.maximum(d, 0)
    nf = jnp.maximum(n, 1).astype(F32)
    log_b = exact + (jnp.log(nf / exact) / math.log(REL_MAX_DIST / exact) * (REL_BUCKETS - exact)).astype(I32)
    bucket = jnp.where(n < exact, n, jnp.minimum(log_b, REL_BUCKETS - 1))
    outs = []
    for h in heads:
        o = jnp.full(d.shape, tab_ref[0, h], F32)
        for k in range(1, REL_BUCKETS):
            o = jnp.where(bucket == k, tab_ref[k, h], o)
        outs.append(o)
    return outs


def _ordered_key(x):
    x = jnp.where(x == 0.0, 0.0, x)
    bits = pltpu.bitcast(x, I32)
    return jnp.where(bits < 0, bits ^ 0x7FFFFFFF, bits)


def _kth_largest_key(count_ge, k, shape):
    t = jnp.full(shape, INT_MIN, I32)
    zero = jnp.zeros(shape, I32)
    t = jnp.where(count_ge(zero) >= k, zero, t)
    for bit in range(30, -1, -1):
        cand = t + (1 << bit)
        t = jnp.where(count_ge(cand) >= k, cand, t)
    return t


def _softmax_step(lg, mask, m_ref, l_ref):
    lg = jnp.where(mask, lg, -jnp.inf)
    m_old = m_ref[...]
    m_new = jnp.maximum(m_old, jnp.max(lg, axis=-1, keepdims=True))
    m_safe = jnp.where(m_new == -jnp.inf, 0.0, m_new)
    p = jnp.exp(lg - m_safe)
    alpha = jnp.exp(m_old - m_safe)
    l_ref[...] = alpha * l_ref[...] + jnp.sum(p, axis=-1, keepdims=True)
    m_ref[...] = m_new
    return p, alpha


def _ada_kernel(c_ref, w_ref, b_ref, o_ref):
    o_ref[0] = _dot(_silu(c_ref[...]), w_ref[0]) + b_ref[0]


def _ada(c, w_ada, b_ada):
    depth, d, n = w_ada.shape
    bc = c.shape[0]
    tn = 1536 if n % 1536 == 0 else n
    return pl.pallas_call(
        _ada_kernel,
        out_shape=jax.ShapeDtypeStruct((depth, bc, n), F32),
        grid=(depth, n // tn),
        in_specs=[pl.BlockSpec((bc, d), lambda l, j: (0, 0)),
                  pl.BlockSpec((1, d, tn), lambda l, j: (l, 0, j)),
                  pl.BlockSpec((1, 1, tn), lambda l, j: (l, 0, j))],
        out_specs=pl.BlockSpec((1, bc, tn), lambda l, j: (l, 0, j)),
        compiler_params=_cp(("arbitrary", "arbitrary")),
        name="ada",
    )(c, w_ada, b_ada.reshape(depth, 1, n))


def _bias_kernel(tab_ref, cend_ref, tile_ref, cmp_ref, *, tq, tk, seq):
    j = _iota((tk, tq), 0)
    i = _iota((tk, tq), 1)
    for k in range(4):
        bs = _rel_bias(k * tk + i - j, tab_ref, range(2 * N_HEADS))
        for h in range(2 * N_HEADS):
            tile_ref[k, h] = bs[h]
    t = _iota((cend_ref.shape[0], seq), 1)
    bs = _rel_bias(t - cend_ref[...], tab_ref, range(N_HEADS))
    for h in range(N_HEADS):
        cmp_ref[h] = bs[h]


def _bias_tiles(rel_table, cend, tq, tk, seq):
    nc = cend.shape[0]
    return pl.pallas_call(
        functools.partial(_bias_kernel, tq=tq, tk=tk, seq=seq),
        out_shape=(jax.ShapeDtypeStruct((4, 2 * N_HEADS, tk, tq), F32),
                   jax.ShapeDtypeStruct((N_HEADS, nc, seq), F32)),
        in_specs=[pl.BlockSpec(memory_space=pltpu.SMEM),
                  pl.BlockSpec(memory_space=pltpu.VMEM)],
        out_specs=(pl.BlockSpec(memory_space=pltpu.VMEM), pl.BlockSpec(memory_space=pltpu.VMEM)),
        compiler_params=pltpu.CompilerParams(vmem_limit_bytes=VMEM_LIMIT),
        name="rel_bias_tiles",
    )(rel_table, cend)


def _inproj_kernel(x_ref, g_ref, sh_ref, sc_ref, w_ref, *o_refs):
    h = _rms(x_ref[...], g_ref[...]) * (1.0 + sc_ref[0]) + sh_ref[0]
    hb = h.astype(BF16)
    off = 0
    for o_ref, w in zip(o_refs, OUT_WIDTHS):
        o_ref[...] = jnp.dot(hb, w_ref[:, off:off + w], preferred_element_type=F32)
        off += w


ROW_WIDTHS = (256, 128, 128, 1024, 256, 128, 128, 4096)
ROW_NAMES = ('nsa_kv4', 'nsa_win', 'dsa_kv', 'hg', 'mla_cq', 'mla_ckv', 'small', 'merge_g')
FM_WIDTHS = (256, 256, 256, 256, 16)
FM_NAMES = ('nsa_qT', 'dsa_qT', 'idx_qT', 'vT', 'smallT')


def _inproj_fm_kernel(x_ref, g_ref, sh_ref, sc_ref, w_ref, wt_ref, *o_refs):
    h = _rms(x_ref[...], g_ref[...]) * (1.0 + sc_ref[0]) + sh_ref[0]
    hb = h.astype(BF16)
    off = 0
    for o_ref, w in zip(o_refs[:len(ROW_WIDTHS)], ROW_WIDTHS):
        o_ref[...] = jnp.dot(hb, w_ref[:, off:off + w], preferred_element_type=F32)
        off += w
    off = 0
    for o_ref, w in zip(o_refs[len(ROW_WIDTHS):], FM_WIDTHS):
        o_ref[0] = _dot_nt(wt_ref[off:off + w, :], hb).astype(o_ref.dtype)
        off += w


def _inproj_fm(x, g, shift, scale, w_row, w_fm, tm, tps):
    m, d = x.shape
    fm_dtypes = (BF16, BF16, BF16, BF16, F32)
    return pl.pallas_call(
        _inproj_fm_kernel,
        out_shape=tuple(jax.ShapeDtypeStruct((m, w), F32) for w in ROW_WIDTHS)
                  + tuple(jax.ShapeDtypeStruct((m // tm, w, tm), dt) for w, dt in zip(FM_WIDTHS, fm_dtypes)),
        grid=(m // tm,),
        in_specs=[pl.BlockSpec((tm, d), lambda i: (i, 0)),
                  pl.BlockSpec((1, d), lambda i: (0, 0)),
                  _mod_spec(shift, tm, tps), _mod_spec(scale, tm, tps),
                  pl.BlockSpec(w_row.shape, lambda i: (0, 0), pipeline_mode=pl.Buffered(1)),
                  pl.BlockSpec(w_fm.shape, lambda i: (0, 0), pipeline_mode=pl.Buffered(1))],
        out_specs=tuple(pl.BlockSpec((tm, w), lambda i: (i, 0)) for w in ROW_WIDTHS)
                  + tuple(pl.BlockSpec((1, w, tm), lambda i: (i, 0, 0)) for w in FM_WIDTHS),
        compiler_params=_cp(("arbitrary",)),
        name="inproj_fm",
    )(x, g, shift, scale, w_row, w_fm)


def _mod_spec(arr, tm, tps):
    d = arr.shape[-1]
    if arr.shape[1] == 1:
        return pl.BlockSpec((1, 1, d), lambda i, *_: (i // tps, 0, 0))
    return pl.BlockSpec((1, tm, d), lambda i, *_: (0, i, 0))


def _inproj(x, g, shift, scale, w_p, tm, tps):
    m, d = x.shape
    n = w_p.shape[1]
    return pl.pallas_call(
        _inproj_kernel,
        out_shape=tuple(jax.ShapeDtypeStruct((m, w), F32) for w in OUT_WIDTHS),
        grid=(m // tm,),
        in_specs=[pl.BlockSpec((tm, d), lambda i: (i, 0)),
                  pl.BlockSpec((1, d), lambda i: (0, 0)),
                  _mod_spec(shift, tm, tps), _mod_spec(scale, tm, tps),
                  pl.BlockSpec((d, n), lambda i: (0, 0), pipeline_mode=pl.Buffered(1))],
        out_specs=tuple(pl.BlockSpec((tm, w), lambda i: (i, 0)) for w in OUT_WIDTHS),
        compiler_params=_cp(("arbitrary",)),
        name="inproj",
    )(x, g, shift, scale, w_p)


def _mla_proj_kernel(cq_ref, ckv_ref, sm_ref, inv_ref, gq_ref, gkv_ref, wn_ref, wr1_ref, wr2_ref, wuk_ref,
                     qc_ref, kc_ref, st_ref, *rest, tm, seq, pos0):
    feature_major = len(rest) > 0
    i = pl.program_id(0)
    qn = _rms(cq_ref[...], gq_ref[...]).astype(BF16)
    q_nope = jnp.dot(qn, wn_ref[...], preferred_element_type=F32)
    r1 = jnp.dot(qn, wr1_ref[...], preferred_element_type=F32)
    r2 = jnp.dot(qn, wr2_ref[...], preferred_element_type=F32)
    pos = ((i * tm + _iota((tm, 1), 0)) % seq + pos0).astype(F32)
    ang = pos * inv_ref[...]
    c, s = jnp.cos(ang), jnp.sin(ang)
    rot1 = r1 * c - r2 * s
    rot2 = r1 * s + r2 * c
    hr = QK_ROPE // 2
    zpad = jnp.zeros((tm, MLA_QW - KV_LORA - QK_ROPE), F32)
    for h in range(N_HEADS):
        q_lat = _dot(q_nope[:, h * QK_NOPE:(h + 1) * QK_NOPE], wuk_ref[h])
        qh = jnp.concatenate([q_lat, rot1[:, h * hr:(h + 1) * hr], rot2[:, h * hr:(h + 1) * hr], zpad], axis=-1)
        if feature_major:
            qc_ref[0, h * MLA_QW:(h + 1) * MLA_QW, :] = qh.T.astype(BF16)
        else:
            qc_ref[:, h * MLA_QW:(h + 1) * MLA_QW] = qh.astype(BF16)
    ckv = _rms(ckv_ref[...], gkv_ref[...])
    if feature_major:
        rest[0][0] = ckv.T.astype(BF16)
    sm = sm_ref[...]
    x1 = sm[:, SM_KR:SM_KR + hr]
    x2 = sm[:, SM_KR + hr:SM_KR + 2 * hr]
    ang1 = pos * inv_ref[:, 0:hr]
    c1, s1 = jnp.cos(ang1), jnp.sin(ang1)
    kr = jnp.concatenate([x1 * c1 - x2 * s1, x1 * s1 + x2 * c1], axis=-1)
    st = jnp.concatenate([ckv, kr], axis=-1)
    st_ref[...] = st
    kc_ref[...] = jnp.concatenate([st, zpad], axis=-1).astype(BF16)


def _mla_proj(cq, ckv, small, inv4, gq, gkv, wn, wr1, wr2, wukT, tm, seq, pos0, feature_major=False):
    m = cq.shape[0]
    row = lambda w: pl.BlockSpec((tm, w), lambda i: (i, 0))
    fm = lambda w: pl.BlockSpec((1, w, tm), lambda i: (i, 0, 0))
    full = lambda a: pl.BlockSpec(a.shape, lambda i: (0,) * a.ndim)
    out_shape = (jax.ShapeDtypeStruct((m, N_HEADS * MLA_QW), BF16),
                 jax.ShapeDtypeStruct((m, MLA_QW), BF16),
                 jax.ShapeDtypeStruct((m, KV_LORA + QK_ROPE), F32))
    out_specs = (row(N_HEADS * MLA_QW), row(MLA_QW), row(KV_LORA + QK_ROPE))
    if feature_major:
        out_shape = (jax.ShapeDtypeStruct((m // tm, N_HEADS * MLA_QW, tm), BF16),) + out_shape[1:] \
                    + (jax.ShapeDtypeStruct((m // tm, KV_LORA, tm), BF16),)
        out_specs = (fm(N_HEADS * MLA_QW),) + out_specs[1:] + (fm(KV_LORA),)
    return pl.pallas_call(
        functools.partial(_mla_proj_kernel, tm=tm, seq=seq, pos0=pos0),
        out_shape=out_shape,
        grid=(m // tm,),
        in_specs=[row(Q_LORA), row(KV_LORA), row(LANES), full(inv4), full(gq), full(gkv),
                  full(wn), full(wr1), full(wr2), full(wukT)],
        out_specs=out_specs,
        compiler_params=_cp(("arbitrary",)),
        name="mla_proj",
    )(cq, ckv, small, inv4, gq, gkv, wn, wr1, wr2, wukT)


def _flash_init(m_ref, l_ref, acc_ref):
    m_ref[...] = jnp.full(m_ref.shape, -jnp.inf, F32)
    l_ref[...] = jnp.zeros(l_ref.shape, F32)
    acc_ref[...] = jnp.zeros(acc_ref.shape, F32)


def _units(tq):
    halves = tq // LANES
    return [(h, half) for h in range(N_HEADS) for half in range(halves)]


def _lane(half):
    return slice(half * LANES, (half + 1) * LANES)


def _flash_t(lgs, masks, v_t, m_s, l_s, acc_s):
    m_all, l_all = m_s[...], l_s[...]
    ps, alphas, new_m, new_l = [], [], [], []
    for u, (lg, mask) in enumerate(zip(lgs, masks)):
        lg = jnp.where(mask, lg, -jnp.inf)
        m_old = m_all[u:u + 1, :]
        m_new = jnp.maximum(m_old, jnp.max(lg, axis=0, keepdims=True))
        m_safe = jnp.where(m_new == -jnp.inf, 0.0, m_new)
        p = jnp.exp(lg - m_safe)
        alpha = jnp.exp(m_old - m_safe)
        new_l.append(alpha * l_all[u:u + 1, :] + jnp.sum(p, axis=0, keepdims=True))
        new_m.append(m_new)
        alphas.append(alpha)
        ps.append(p.astype(BF16))
    m_s[...] = jnp.concatenate(new_m, axis=0)
    l_s[...] = jnp.concatenate(new_l, axis=0)
    for u in range(len(lgs)):
        acc_s[u] = alphas[u] * acc_s[u] + jnp.dot(v_t, ps[u], preferred_element_type=F32)


def _flash_t_out(u, l_s, acc_s):
    return acc_s[u] / jnp.maximum(l_s[u:u + 1, :], 1e-30)


def _store_units_transposed(o_ref, outs, tq):
    halves = tq // LANES
    rows = [jnp.concatenate(outs[h * halves:(h + 1) * halves], axis=-1) for h in range(N_HEADS)]
    o_ref[...] = jnp.concatenate(rows, axis=0).T


def _nsa_prompt_kernel(qt_ref, gt_ref, kv_ref, win_ref, vt_ref, bias_ref, bcmp_ref, o_ref,
                       cmp_s, m_s, l_s, acc_s, *, seq, tq, n_top):
    tk = tq
    qb = pl.program_id(1)
    t0 = qb * tq
    nsel = seq // NSA_SEL_BLOCK
    nc = 2 * nsel
    scale = HEAD_DIM ** -0.5
    units = _units(tq)
    halves = tq // LANES

    @pl.when(qb == 0)
    def _():
        for j in range(nsel):
            r0 = j * NSA_SEL_BLOCK
            cmp_s[j:j + 1, :] = jnp.sum(kv_ref[r0:r0 + NSA_CMP_BLOCK, 0:2 * HEAD_DIM], axis=0, keepdims=True) * (1.0 / NSA_CMP_BLOCK)
            cmp_s[nsel + j:nsel + j + 1, :] = jnp.sum(kv_ref[r0 + NSA_CMP_BLOCK:r0 + 2 * NSA_CMP_BLOCK, 0:2 * HEAD_DIM],
                                                      axis=0, keepdims=True) * (1.0 / NSA_CMP_BLOCK)

    def q_unit(h, half):
        return qt_ref[h * HEAD_DIM:(h + 1) * HEAD_DIM, _lane(half)]
    t_rows = [t0 + half * LANES + _iota((1, LANES), 1) for half in range(halves)]

    kc = cmp_s[:, 0:HEAD_DIM].astype(BF16)
    eye = (_iota((HEAD_DIM, HEAD_DIM), 0) == _iota((HEAD_DIM, HEAD_DIM), 1)).astype(BF16)
    vc_t = _dot_nt(eye, cmp_s[:, HEAD_DIM:2 * HEAD_DIM]).astype(BF16)
    jj = _iota((nc, 1), 0)
    c_end = jnp.where(jj < nsel, NSA_SEL_BLOCK * jj + (NSA_CMP_BLOCK - 1), NSA_SEL_BLOCK * (jj - nsel) + (NSA_SEL_BLOCK - 1))
    lgs = [jnp.dot(kc, q_unit(h, half), preferred_element_type=F32) for h, half in units]
    o_cmp, ps_half = [], [None] * halves
    for u, (h, half) in enumerate(units):
        lg = lgs[u] * scale + bcmp_ref[h, :, _lane(half)]
        lg = jnp.where(c_end <= t_rows[half], lg, -jnp.inf)
        mx = jnp.max(lg, axis=0, keepdims=True)
        mx = jnp.where(mx == -jnp.inf, 0.0, mx)
        p = jnp.exp(lg - mx)
        p = p / jnp.maximum(jnp.sum(p, axis=0, keepdims=True), 1e-30)
        o_cmp.append(jnp.dot(vc_t, p.astype(BF16), preferred_element_type=F32))
        ps_half[half] = p if ps_half[half] is None else ps_half[half] + p

    blk = _iota((nsel, 1), 0)
    selm = []
    for half in range(halves):
        imp = ps_half[half][0:nsel] + ps_half[half][nsel:nc]
        cur = t_rows[half] // NSA_SEL_BLOCK
        imp = jnp.where((blk == cur) | (blk == 0), FORCE_SCORE, imp)
        imp = jnp.where(blk <= cur, imp, -jnp.inf)
        chosen = jnp.zeros((nsel, LANES), F32)
        for _ in range(n_top):
            top = jnp.max(imp, axis=0, keepdims=True)
            first = jnp.min(jnp.where(imp == top, blk, nsel), axis=0, keepdims=True)
            pick = blk == first
            chosen = jnp.where(pick, 1.0, chosen)
            imp = jnp.where(pick, -jnp.inf, imp)
        selm.append(chosen.astype(BF16))

    _flash_init(m_s, l_s, acc_s)

    def sel_body(c, carry):
        s0 = pl.multiple_of(c * tk, tk)
        k = kv_ref[pl.ds(s0, tk), 2 * HEAD_DIM:3 * HEAD_DIM].astype(BF16)
        kk = jnp.minimum(qb - c, 3)
        s_pos = s0 + _iota((tk, 1), 0)
        expand = ((s0 + _iota((tk, nsel), 0)) // NSA_SEL_BLOCK == _iota((tk, nsel), 1)).astype(BF16)
        lgs = [jnp.dot(k, q_unit(h, half), preferred_element_type=F32) for h, half in units]
        picked = [jnp.dot(expand, selm[half], preferred_element_type=F32) > 0.5 for half in range(halves)]
        lgs = [lgs[u] * scale + bias_ref[kk, h, :, _lane(half)] for u, (h, half) in enumerate(units)]
        masks = [picked[half] & (s_pos <= t_rows[half]) for h, half in units]
        _flash_t(lgs, masks, vt_ref[c, HEAD_DIM:2 * HEAD_DIM, :], m_s, l_s, acc_s)
        return carry

    lax.fori_loop(0, qb + 1, sel_body, 0)
    o_sel = [_flash_t_out(u, l_s, acc_s) for u in range(len(units))]

    _flash_init(m_s, l_s, acc_s)

    def win_body(c, carry):
        s0 = pl.multiple_of(c * tk, tk)
        k = win_ref[pl.ds(s0, tk), 0:HEAD_DIM].astype(BF16)
        kk = jnp.minimum(qb - c, 3)
        s_pos = s0 + _iota((tk, 1), 0)
        lgs = [jnp.dot(k, q_unit(h, half), preferred_element_type=F32) for h, half in units]
        lgs = [lgs[u] * scale + bias_ref[kk, h, :, _lane(half)] for u, (h, half) in enumerate(units)]
        masks = [(t_rows[half] - s_pos >= 0) & (t_rows[half] - s_pos <= NSA_WINDOW) for h, half in units]
        _flash_t(lgs, masks, vt_ref[c, 2 * HEAD_DIM:3 * HEAD_DIM, :], m_s, l_s, acc_s)
        return carry

    lax.fori_loop(jnp.maximum(qb - (NSA_WINDOW + tk - 1) // tk, 0), qb + 1, win_body, 0)

    g = _sigmoid(gt_ref[0:3 * N_HEADS, :])
    outs = []
    for u, (h, half) in enumerate(units):
        outs.append(g[3 * h:3 * h + 1, _lane(half)] * o_cmp[u] + g[3 * h + 1:3 * h + 2, _lane(half)] * o_sel[u]
                    + g[3 * h + 2:3 * h + 3, _lane(half)] * _flash_t_out(u, l_s, acc_s))
    _store_units_transposed(o_ref, outs, tq)


def _fm_block(w, tq, nq):
    return pl.BlockSpec((None, w, tq), lambda b, i: (b * nq + i, 0, 0))


def _nsa_prompt(nsa_qt, small_t, kv4, win, v_t, bias_t, bias_cmp, batch, seq, tq):
    nq = seq // tq
    nsel = seq // NSA_SEL_BLOCK
    nu = len(_units(tq))
    rows = lambda w: pl.BlockSpec((seq, w), lambda b, i: (b, 0))
    return pl.pallas_call(
        functools.partial(_nsa_prompt_kernel, seq=seq, tq=tq, n_top=min(NSA_TOP_N, nsel)),
        out_shape=jax.ShapeDtypeStruct((batch * seq, BRANCH_W), F32),
        grid=(batch, nq),
        in_specs=[_fm_block(BRANCH_W, tq, nq), _fm_block(FM_WIDTHS[4], tq, nq), rows(4 * HEAD_DIM), rows(2 * HEAD_DIM),
                  pl.BlockSpec((nq, FM_WIDTHS[3], tq), lambda b, i: (b, 0, 0)),
                  pl.BlockSpec((4, N_HEADS, tq, tq), lambda b, i: (0, 0, 0, 0)),
                  pl.BlockSpec((N_HEADS, 2 * nsel, tq), lambda b, i: (0, 0, i))],
        out_specs=pl.BlockSpec((tq, BRANCH_W), lambda b, i: (b * nq + i, 0)),
        scratch_shapes=[pltpu.VMEM((2 * nsel, 2 * HEAD_DIM), F32),
                        pltpu.VMEM((nu, LANES), F32), pltpu.VMEM((nu, LANES), F32),
                        pltpu.VMEM((nu, HEAD_DIM, LANES), F32)],
        compiler_params=_cp(("arbitrary", "arbitrary")),
        name="nsa_prompt",
    )(nsa_qt, small_t, kv4, win, v_t, bias_t, bias_cmp)


def _dsa_prompt_kernel(qt_ref, iqt_ref, gt_ref, smf_ref, kv_ref, vt_ref, bias_ref, o_ref,
                       key_s, m_s, l_s, acc_s, *, tq, topk):
    tk = tq
    qb = pl.program_id(1)
    t0 = qb * tq
    scale = HEAD_DIM ** -0.5
    units = _units(tq)
    halves = tq // LANES
    t_row = t0 + _iota((1, tq), 1)
    cst = IDX_DIM ** -0.5 * IDX_HEADS ** -0.5
    wi = gt_ref[3 * N_HEADS:3 * N_HEADS + IDX_HEADS, :]

    def score_body(c, carry):
        s0 = pl.multiple_of(c * tk, tk)
        ki = smf_ref[pl.ds(s0, tk), SM_IDXK:SM_IDXK + IDX_DIM].astype(BF16)
        s_pos = s0 + _iota((tk, 1), 0)
        dots = [jnp.dot(ki, iqt_ref[h * IDX_DIM:(h + 1) * IDX_DIM, _lane(half)], preferred_element_type=F32)
                for h, half in units]
        for half in range(halves):
            sc = None
            for u, (h, hf) in enumerate(units):
                if hf == half:
                    term = jnp.maximum(dots[u], 0.0) * wi[h:h + 1, _lane(half)]
                    sc = term if sc is None else sc + term
            sc = jnp.where(s_pos <= t_row[:, _lane(half)], sc * cst, -jnp.inf)
            key_s[c, :, _lane(half)] = _ordered_key(sc)
        return carry

    lax.fori_loop(0, qb + 1, score_body, 0)

    def count(pred):
        def body(c, acc):
            return acc + jnp.sum(jnp.where(pred(key_s[c]), 1, 0), axis=0, keepdims=True)
        return lax.fori_loop(0, qb + 1, body, jnp.zeros((1, tq), I32))

    thr = _kth_largest_key(lambda cand: count(lambda key: key >= cand), topk, (1, tq))
    need = (topk - count(lambda key: key > thr)).astype(F32)

    strict_lower = (_iota((tk, tk), 1) < _iota((tk, tk), 0)).astype(BF16)
    _flash_init(m_s, l_s, acc_s)

    def att_body(c, run):
        s0 = pl.multiple_of(c * tk, tk)
        key = key_s[c]
        eq = key == thr
        eqb = jnp.where(eq, 1.0, 0.0).astype(BF16)
        before = jnp.dot(strict_lower, eqb, preferred_element_type=F32) + run
        chosen = ((key > thr) | (eq & (before < need))) & (s0 + _iota((tk, 1), 0) <= t_row)
        k = kv_ref[pl.ds(s0, tk), 0:HEAD_DIM].astype(BF16)
        kk = jnp.minimum(qb - c, 3)
        lgs = [jnp.dot(k, qt_ref[h * HEAD_DIM:(h + 1) * HEAD_DIM, _lane(half)], preferred_element_type=F32)
               for h, half in units]
        lgs = [lgs[u] * scale + bias_ref[kk, h, :, _lane(half)] for u, (h, half) in enumerate(units)]
        masks = [chosen[:, _lane(half)] for h, half in units]
        _flash_t(lgs, masks, vt_ref[c, 3 * HEAD_DIM:4 * HEAD_DIM, :], m_s, l_s, acc_s)
        return run + jnp.sum(eqb.astype(F32), axis=0, keepdims=True)

    lax.fori_loop(0, qb + 1, att_body, jnp.zeros((1, tq), F32))
    _store_units_transposed(o_ref, [_flash_t_out(u, l_s, acc_s) for u in range(len(units))], tq)


def _dsa_prompt(dsa_qt, idx_qt, small_t, small, dsa_kv, v_t, bias_t, batch, seq, tq):
    nq = seq // tq
    nu = len(_units(tq))
    rows = lambda w: pl.BlockSpec((seq, w), lambda b, i: (b, 0))
    return pl.pallas_call(
        functools.partial(_dsa_prompt_kernel, tq=tq, topk=min(DSA_TOPK_MAX, seq // 4)),
        out_shape=jax.ShapeDtypeStruct((batch * seq, BRANCH_W), F32),
        grid=(batch, nq),
        in_specs=[_fm_block(BRANCH_W, tq, nq), _fm_block(IDX_HEADS * IDX_DIM, tq, nq), _fm_block(FM_WIDTHS[4], tq, nq),
                  rows(LANES), rows(2 * HEAD_DIM),
                  pl.BlockSpec((nq, FM_WIDTHS[3], tq), lambda b, i: (b, 0, 0)),
                  pl.BlockSpec((4, N_HEADS, tq, tq), lambda b, i: (0, 1, 0, 0))],
        out_specs=pl.BlockSpec((tq, BRANCH_W), lambda b, i: (b * nq + i, 0)),
        scratch_shapes=[pltpu.VMEM((nq, tq, tq), I32),
                        pltpu.VMEM((nu, LANES), F32), pltpu.VMEM((nu, LANES), F32),
                        pltpu.VMEM((nu, HEAD_DIM, LANES), F32)],
        compiler_params=_cp(("arbitrary", "arbitrary")),
        name="dsa_prompt",
    )(dsa_qt, idx_qt, small_t, small, dsa_kv, v_t, bias_t)


def _mla_prompt_kernel(qt_ref, kc_ref, vt_ref, wuvt_ref, o_ref, m_s, l_s, acc_s, *, tq):
    tk = tq
    qb = pl.program_id(1)
    scale = MLA_HEAD_QK ** -0.5
    units = _units(tq)
    t_rows = [qb * tq + half * LANES + _iota((1, LANES), 1) for half in range(tq // LANES)]
    _flash_init(m_s, l_s, acc_s)

    def body(c, carry):
        s0 = pl.multiple_of(c * tk, tk)
        kc = kc_ref[pl.ds(s0, tk), :]
        s_pos = s0 + _iota((tk, 1), 0)
        lgs = [jnp.dot(kc, qt_ref[h * MLA_QW:(h + 1) * MLA_QW, _lane(half)], preferred_element_type=F32) * scale
               for h, half in units]
        masks = [s_pos <= t_rows[half] for h, half in units]
        _flash_t(lgs, masks, vt_ref[c], m_s, l_s, acc_s)
        return carry

    lax.fori_loop(0, qb + 1, body, 0)
    outs = [jnp.dot(wuvt_ref[h], _flash_t_out(u, l_s, acc_s).astype(BF16), preferred_element_type=F32)
            for u, (h, half) in enumerate(units)]
    _store_units_transposed(o_ref, outs, tq)


def _mla_prompt(qc_t, kc, v_t, wuv_t, batch, seq, tq):
    nq = seq // tq
    nu = len(_units(tq))
    return pl.pallas_call(
        functools.partial(_mla_prompt_kernel, tq=tq),
        out_shape=jax.ShapeDtypeStruct((batch * seq, BRANCH_W), F32),
        grid=(batch, nq),
        in_specs=[_fm_block(N_HEADS * MLA_QW, tq, nq),
                  pl.BlockSpec((seq, MLA_QW), lambda b, i: (b, 0)),
                  pl.BlockSpec((nq, KV_LORA, tq), lambda b, i: (b, 0, 0)),
                  pl.BlockSpec(wuv_t.shape, lambda b, i: (0, 0, 0))],
        out_specs=pl.BlockSpec((tq, BRANCH_W), lambda b, i: (b * nq + i, 0)),
        scratch_shapes=[pltpu.VMEM((nu, LANES), F32), pltpu.VMEM((nu, LANES), F32),
                        pltpu.VMEM((nu, KV_LORA, LANES), F32)],
        compiler_params=_cp(("arbitrary", "arbitrary")),
        name="mla_prompt",
    )(qc_t, kc, v_t, wuv_t)


def _hgrn_gates(hg, lb):
    w = BRANCH_W
    q, fl, iv, gg = hg[:, 0:w], hg[:, w:2 * w], hg[:, 2 * w:3 * w], hg[:, 3 * w:4 * w]
    f = lb + (1.0 - lb) * _sigmoid(fl)
    return _silu(q), f, 1.0 - f, iv, gg


def _hgrn_finish(o, gg, hgn, ones_bf):
    ms = _dot_hilo(o * o, ones_bf) * (1.0 / HEAD_DIM)
    return o * lax.rsqrt(ms + EPS) * hgn * _silu(gg)


def _hgrn_prompt_kernel(hg_ref, lb_ref, hgn_ref, o_ref, s_ref, st_s, q_s, b_s, k_s, v_s, o_s, w_s, *, tc):
    C = HGRN_CHUNK
    w = BRANCH_W
    i = pl.program_id(1)

    @pl.when(i == 0)
    def _():
        st_s[...] = jnp.zeros(st_s.shape, F32)

    qf, f, k, iv, gg = _hgrn_gates(hg_ref[...], lb_ref[...])
    b = jnp.log(jnp.maximum(f, 1e-20))
    row = _iota((tc, 1), 0) % C
    for s in (1, 2, 4, 8):
        b = b + jnp.where(row >= s, pltpu.roll(b, s, 0), 0.0)
    q_s[...] = qf
    b_s[...] = b
    k_s[...] = k
    v_s[...] = iv
    same_head = _head_block_ones(w)
    ones_bf = same_head.astype(BF16)
    group = (_iota((C, C * C), 0) == _iota((C, C * C), 1) // C).astype(BF16)
    s_idx = _iota((C, 1), 0)

    def chunk(ci, carry):
        r0 = pl.multiple_of(ci * C, C)
        qc, bc, kc, vc = q_s[pl.ds(r0, C), :], b_s[pl.ds(r0, C), :], k_s[pl.ds(r0, C), :], v_s[pl.ds(r0, C), :]
        for tt in range(C):
            dec = jnp.exp(jnp.where(s_idx <= tt, bc[tt:tt + 1, :] - bc, -jnp.inf))
            w_s[tt * C:(tt + 1) * C, :] = dec * qc[tt:tt + 1, :] * kc
        a_rep = _dot_hilo(w_s[...], ones_bf)
        prod = (a_rep.reshape(C, C, w) * vc[None]).reshape(C * C, w)
        hi = prod.astype(BF16)
        lo = (prod - hi.astype(F32)).astype(BF16)
        o_intra = jnp.dot(group, hi, preferred_element_type=F32) + jnp.dot(group, lo, preferred_element_type=F32)
        st = st_s[...]
        o_s[pl.ds(r0, C), :] = o_intra + _dot_nt(qc * jnp.exp(bc), st)
        bl = bc[C - 1:C, :]
        upd = _dot_tn(vc, kc * jnp.exp(bl - bc))
        st_s[...] = st * jnp.exp(bl) + jnp.where(same_head, upd, 0.0)
        return carry

    lax.fori_loop(0, tc // C, chunk, 0)
    o_ref[...] = _hgrn_finish(o_s[...], gg, hgn_ref[...], ones_bf)

    @pl.when(i == pl.num_programs(1) - 1)
    def _():
        for h in range(N_HEADS):
            s_ref[0, h] = st_s[h * HEAD_DIM:(h + 1) * HEAD_DIM, h * HEAD_DIM:(h + 1) * HEAD_DIM].T


def _hgrn_prompt(hg, lower, hgn, batch, seq, tc):
    nt = seq // tc
    w = BRANCH_W
    return pl.pallas_call(
        functools.partial(_hgrn_prompt_kernel, tc=tc),
        out_shape=(jax.ShapeDtypeStruct((batch * seq, w), F32),
                   jax.ShapeDtypeStruct((batch, N_HEADS, HEAD_DIM, HEAD_DIM), F32)),
        grid=(batch, nt),
        in_specs=[pl.BlockSpec((tc, 4 * w), lambda b, i: (b * nt + i, 0)),
                  pl.BlockSpec((1, w), lambda b, i: (0, 0)),
                  pl.BlockSpec((1, w), lambda b, i: (0, 0))],
        out_specs=(pl.BlockSpec((tc, w), lambda b, i: (b * nt + i, 0)),
                   pl.BlockSpec((1, N_HEADS, HEAD_DIM, HEAD_DIM), lambda b, i: (b, 0, 0, 0))),
        scratch_shapes=[pltpu.VMEM((w, w), F32)] + [pltpu.VMEM((tc, w), F32)] * 5
                       + [pltpu.VMEM((HGRN_CHUNK * HGRN_CHUNK, w), F32)],
        compiler_params=_cp(("arbitrary", "arbitrary")),
        name="hgrn_prompt",
    )(hg, lower, hgn)


def _hgrn_step_kernel(hg_ref, lb_ref, hgn_ref, s0_ref, o_ref, s_ref, o_s, *, bt):
    qf, f, k, iv, gg = _hgrn_gates(hg_ref[...], lb_ref[...])
    fT = jnp.maximum(f, 1e-20).T
    kT = k.T
    qT = qf.T
    for bi in range(bt):
        for h in range(N_HEADS):
            hs = slice(h * HEAD_DIM, (h + 1) * HEAD_DIM)
            s_new = fT[hs, bi:bi + 1] * s0_ref[bi, h] + kT[hs, bi:bi + 1] * iv[bi:bi + 1, hs]
            s_ref[bi, h] = s_new
            o_s[bi:bi + 1, hs] = jnp.sum(qT[hs, bi:bi + 1] * s_new, axis=0, keepdims=True)
    o_ref[...] = _hgrn_finish(o_s[...], gg, hgn_ref[...], _head_block_ones(BRANCH_W).astype(BF16))


def _hgrn_step(hg, lower, hgn, s0, bt):
    m = hg.shape[0]
    w = BRANCH_W
    sblk = pl.BlockSpec((bt, N_HEADS, HEAD_DIM, HEAD_DIM), lambda i: (i, 0, 0, 0))
    return pl.pallas_call(
        functools.partial(_hgrn_step_kernel, bt=bt),
        out_shape=(jax.ShapeDtypeStruct((m, w), F32), jax.ShapeDtypeStruct(s0.shape, F32)),
        grid=(m // bt,),
        in_specs=[pl.BlockSpec((bt, 4 * w), lambda i: (i, 0)),
                  pl.BlockSpec((1, w), lambda i: (0, 0)), pl.BlockSpec((1, w), lambda i: (0, 0)), sblk],
        out_specs=(pl.BlockSpec((bt, w), lambda i: (i, 0)), sblk),
        scratch_shapes=[pltpu.VMEM((bt, w), F32)],
        compiler_params=_cp(("arbitrary",)),
        name="hgrn_step",
    )(hg, lower, hgn, s0)


def _merge_kernel(x_ref, o0_ref, o1_ref, o2_ref, o3_ref, mg_ref, gate_ref, gpost_ref, wb_ref, wo_ref, y_ref):
    d = x_ref.shape[1]
    mixed = None
    for n, o_ref in enumerate((o0_ref, o1_ref, o2_ref, o3_ref)):
        term = _sigmoid(mg_ref[:, n * d:(n + 1) * d]) * _dot(o_ref[...], wb_ref[n])
        mixed = term if mixed is None else mixed + term
    y = _dot(mixed, wo_ref[...])
    y_ref[...] = x_ref[...] + gate_ref[0] * _rms(y, gpost_ref[...])


def _merge(x, branches, merge_g, gate, gpost, wb, wo, tm, tps):
    m, d = x.shape
    row = lambda w: pl.BlockSpec((tm, w), lambda i: (i, 0))
    return pl.pallas_call(
        _merge_kernel,
        out_shape=jax.ShapeDtypeStruct((m, d), F32),
        grid=(m // tm,),
        in_specs=[row(d)] + [row(BRANCH_W)] * 4 + [row(N_BRANCH * d), _mod_spec(gate, tm, tps),
                  pl.BlockSpec((1, d), lambda i: (0, 0)),
                  pl.BlockSpec(wb.shape, lambda i: (0, 0, 0), pipeline_mode=pl.Buffered(1)),
                  pl.BlockSpec(wo.shape, lambda i: (0, 0), pipeline_mode=pl.Buffered(1))],
        out_specs=row(d),
        compiler_params=_cp(("arbitrary",)),
        name="merge",
    )(x, *branches, merge_g, gate, gpost, wb, wo)


def _gelu_tanh(x):
    return 0.5 * x * (1.0 + jnp.tanh(math.sqrt(2.0 / math.pi) * (x + 0.044715 * (x * x * x))))


def _ffn_kernel(*refs, tm, tps, nff, stepwise):
    if stepwise:
        (x_ref, gpre_ref, sh_ref, sc_ref, gate_ref, gpost_ref, wg_ref, wv_ref, cwg_ref, cwv_ref, cbg_ref, cbv_ref,
         wd_ref, p0g_ref, p0v_ref, p1g_ref, p1v_ref, y_ref, ug_ref, uv_ref, h_s, acc_s) = refs
    else:
        (x_ref, gpre_ref, sh_ref, sc_ref, gate_ref, gpost_ref, wg_ref, wv_ref, cwg_ref, cwv_ref, cbg_ref, cbv_ref,
         wd_ref, y_ref, csg_ref, csv_ref, h_s, acc_s, carry_g, carry_v) = refs
    i = pl.program_id(0)
    j = pl.program_id(1)

    @pl.when(j == 0)
    def _():
        h = _rms(x_ref[...], gpre_ref[...]) * (1.0 + sc_ref[0]) + sh_ref[0]
        h_s[...] = h.astype(BF16)
        acc_s[...] = jnp.zeros(acc_s.shape, F32)

    hb = h_s[...]
    ug = jnp.dot(hb, wg_ref[...], preferred_element_type=F32)
    uv = jnp.dot(hb, wv_ref[...], preferred_element_type=F32)

    if stepwise:
        def conv(u, cw_ref, cb_ref, p0_ref, p1_ref):
            return cb_ref[...] + p0_ref[...] * cw_ref[0:1, :] + p1_ref[...] * cw_ref[1:2, :] + u * cw_ref[2:3, :]
        cg = conv(ug, cwg_ref, cbg_ref, p0g_ref, p1g_ref)
        cv = conv(uv, cwv_ref, cbv_ref, p0v_ref, p1v_ref)
        ug_ref[...] = ug
        uv_ref[...] = uv
    else:
        first = (i % tps) == 0
        row = _iota((tm, 1), 0)

        @pl.when(i == 0)
        def _():
            carry_g[j] = jnp.zeros(carry_g.shape[1:], F32)
            carry_v[j] = jnp.zeros(carry_v.shape[1:], F32)

        def conv(u, cw_ref, cb_ref, carry):
            prev = jnp.where(first, 0.0, carry[j])
            um1 = jnp.where(row == 0, prev[1:2, :], pltpu.roll(u, 1, 0))
            um2 = jnp.where(row == 0, prev[0:1, :], jnp.where(row == 1, prev[1:2, :], pltpu.roll(u, 2, 0)))
            return cb_ref[...] + um2 * cw_ref[0:1, :] + um1 * cw_ref[1:2, :] + u * cw_ref[2:3, :]
        cg = conv(ug, cwg_ref, cbg_ref, carry_g)
        cv = conv(uv, cwv_ref, cbv_ref, carry_v)
        carry_g[j] = ug[tm - 2:tm, :]
        carry_v[j] = uv[tm - 2:tm, :]
        csg_ref[0, j] = ug[tm - 2:tm, :]
        csv_ref[0, j] = uv[tm - 2:tm, :]

    acc_s[...] += _dot(_gelu_tanh(cg) * cv, wd_ref[...])

    @pl.when(j == nff - 1)
    def _():
        y_ref[...] = x_ref[...] + gate_ref[0] * _rms(acc_s[...], gpost_ref[...])


def _ffn(x, gpre, shift, scale, gate, gpost, w_up, conv_w, conv_b, w_down, tm, tps, prev=None):
    m, d = x.shape
    dff = w_down.shape[0]
    nff = 2 if dff % (2 * LANES) == 0 else 1
    fc = dff // nff
    stepwise = prev is not None
    vec = lambda: pl.BlockSpec((1, d), lambda i, j: (0, 0))
    colg = lambda r: pl.BlockSpec((r, fc), lambda i, j: (0, j))
    colv = lambda r: pl.BlockSpec((r, fc), lambda i, j: (0, nff + j))
    in_specs = [pl.BlockSpec((tm, d), lambda i, j: (i, 0)), vec(),
                _mod_spec(shift, tm, tps), _mod_spec(scale, tm, tps), _mod_spec(gate, tm, tps), vec(),
                colg(d), colv(d), colg(CONV_W), colv(CONV_W), colg(1), colv(1),
                pl.BlockSpec((fc, d), lambda i, j: (j, 0))]
    args = [x, gpre, shift, scale, gate, gpost, w_up, w_up, conv_w, conv_w, conv_b, conv_b, w_down]
    scratch = [pltpu.VMEM((tm, d), BF16), pltpu.VMEM((tm, d), F32)]
    if stepwise:
        p0, p1 = prev
        in_specs += [pl.BlockSpec((tm, fc), lambda i, j: (i, j)), pl.BlockSpec((tm, fc), lambda i, j: (i, nff + j))] * 2
        args += [p0, p0, p1, p1]
        out_shape = (jax.ShapeDtypeStruct((m, d), F32), jax.ShapeDtypeStruct((m, dff), F32), jax.ShapeDtypeStruct((m, dff), F32))
        out_specs = (pl.BlockSpec((tm, d), lambda i, j: (i, 0)),
                     pl.BlockSpec((tm, fc), lambda i, j: (i, j)), pl.BlockSpec((tm, fc), lambda i, j: (i, j)))
    else:
        nseq = m // (tm * tps)
        out_shape = (jax.ShapeDtypeStruct((m, d), F32),
                     jax.ShapeDtypeStruct((nseq, nff, CONV_W - 1, fc), F32),
                     jax.ShapeDtypeStruct((nseq, nff, CONV_W - 1, fc), F32))
        cs = pl.BlockSpec((1, nff, CONV_W - 1, fc), lambda i, j: (i // tps, 0, 0, 0))
        out_specs = (pl.BlockSpec((tm, d), lambda i, j: (i, 0)), cs, cs)
        scratch += [pltpu.VMEM((nff, CONV_W - 1, fc), F32)] * 2
    return pl.pallas_call(
        functools.partial(_ffn_kernel, tm=tm, tps=tps, nff=nff, stepwise=stepwise),
        out_shape=out_shape,
        grid=(m // tm, nff),
        in_specs=in_specs,
        out_specs=out_specs,
        scratch_shapes=scratch,
        compiler_params=_cp(("arbitrary", "arbitrary")),
        name="ffn_step" if stepwise else "ffn_seq",
    )(*args)


def _fetch_pages(pt_ref, pool_ref, buf, sem, layer, pg, rows, rowblk):
    b, j = pl.program_id(0), pl.program_id(1)
    nb, nj = pl.num_programs(0), pl.num_programs(1)
    step = b * nj + j
    slot = step % 2

    def copies(bb, jj, sl):
        return [pltpu.make_async_copy(pool_ref.at[layer, pt_ref[bb, jj * pg + k], pl.ds(rowblk * rows, rows), :],
                                      buf.at[sl, k], sem.at[sl]) for k in range(pg)]

    @pl.when(step == 0)
    def _():
        for c in copies(b, j, slot):
            c.start()

    @pl.when(step + 1 < nb * nj)
    def _():
        wrap = j + 1 == nj
        for c in copies(jnp.where(wrap, b + 1, b), jnp.where(wrap, 0, j + 1), 1 - slot):
            c.start()

    for c in copies(b, j, slot):
        c.wait()
    return slot


def _page_scratch(pg, rows):
    return [pltpu.VMEM((2, pg, rows, PAGE_SIZE), F32), pltpu.SemaphoreType.DMA((2,))]


def _softmax_with_self(lg, valid, lg_self):
    lg = jnp.where(valid, lg, -jnp.inf)
    m = jnp.maximum(jnp.max(lg, axis=-1, keepdims=True), lg_self)
    p = jnp.exp(lg - m)
    p_self = jnp.exp(lg_self - m)
    den = jnp.sum(p, axis=-1, keepdims=True) + p_self
    return p / den, p_self / den


def _self_logit(q8, k_row):
    qf = q8.astype(BF16).astype(F32)
    kf = k_row.astype(BF16).astype(F32)
    return jnp.sum(qf * kf, axis=-1, keepdims=True)


def _rows8(x, w):
    return jnp.concatenate([x[:, h * w:(h + 1) * w] for h in range(N_HEADS)] + [jnp.zeros((8 - N_HEADS, w), x.dtype)], axis=0)


def _col8(x):
    r = _iota((8, 1), 0)
    out = jnp.zeros((8, 1), F32)
    for h in range(N_HEADS):
        out = jnp.where(r == h, x[:, h:h + 1], out)
    return out


def _bias8(d, tab_ref, head0):
    bs = _rel_bias(d, tab_ref, range(head0, head0 + N_HEADS))
    r = _iota((8, d.shape[1]), 0)
    out = jnp.zeros((8, d.shape[1]), F32)
    for h in range(N_HEADS):
        out = jnp.where(r == h, bs[h], out)
    return out


def _softmax_rows(lg, valid):
    lg = jnp.where(valid, lg, -jnp.inf)
    mx = jnp.max(lg, axis=-1, keepdims=True)
    mx = jnp.where(mx == -jnp.inf, 0.0, mx)
    p = jnp.exp(lg - mx)
    return p / jnp.maximum(jnp.sum(p, axis=-1, keepdims=True), 1e-30)


def _write_heads(o_ref, o8, w):
    for h in range(N_HEADS):
        o_ref[0, :, h * w:(h + 1) * w] = o8[h:h + 1, :]


def _nsa_cmp_step_kernel(pt_ref, tab_ref, q_ref, pool_ref, o_ref, idx_ref, cmp_s, buf, sem, *, layer, pg, past):
    slot = _fetch_pages(pt_ref, pool_ref, buf, sem, layer, pg, 2 * HEAD_DIM, 0)
    pages = [buf.at[slot, k] for k in range(pg)]
    j = pl.program_id(1)
    nsel = past // NSA_SEL_BLOCK
    per_page = PAGE_SIZE // NSA_CMP_BLOCK
    nb = per_page * pg
    half = nb // 2
    nsteps = cmp_s.shape[0]
    key = _iota((PAGE_SIZE, nb), 0)
    col = _iota((PAGE_SIZE, nb), 1)
    acc = jnp.zeros((2 * HEAD_DIM, nb), F32)
    for k in range(pg):
        g = per_page * k + key // NSA_CMP_BLOCK
        pool = jnp.where(col == (g % 2) * half + g // 2, 1.0 / NSA_CMP_BLOCK, 0.0).astype(BF16)
        acc = acc + jnp.dot(pages[k][...].astype(BF16), pool, preferred_element_type=F32)
    cmp_s[j] = acc

    @pl.when(j == pl.num_programs(1) - 1)
    def _():
        q8 = _rows8(q_ref[0], HEAD_DIM)
        cc = _iota((1, nb), 1)
        ps_all, o_cmp = [], None
        lgs = []
        for s in range(nsteps):
            sel = s * half + jnp.where(cc < half, cc, cc - half)
            c_end = NSA_SEL_BLOCK * sel + jnp.where(cc < half, NSA_CMP_BLOCK - 1, NSA_SEL_BLOCK - 1)
            lgs.append(_dot(q8, cmp_s[s, 0:HEAD_DIM, :]) * HEAD_DIM ** -0.5 + _bias8(past - c_end, tab_ref, 0))
        lg = jnp.concatenate(lgs, axis=-1)
        p = _softmax_rows(lg, True)
        for s in range(nsteps):
            term = _dot_nt(p[:, s * nb:(s + 1) * nb], cmp_s[s, HEAD_DIM:2 * HEAD_DIM, :])
            o_cmp = term if o_cmp is None else o_cmp + term
            ps = p[0:1, s * nb:(s + 1) * nb] + p[1:2, s * nb:(s + 1) * nb] + p[2:3, s * nb:(s + 1) * nb] + p[3:4, s * nb:(s + 1) * nb]
            ps_all.append(ps[:, 0:half] + ps[:, half:nb])
        _write_heads(o_ref, o_cmp, HEAD_DIM)
        idx_ref[0] = jnp.concatenate(ps_all, axis=-1)


def _nsa_pick_kernel(imp_ref, idx_ref, *, n_pick):
    imp = imp_ref[...]
    bsz, nsel = imp.shape
    blk = _iota((1, nsel), 1)
    imp = jnp.where(blk == 0, -jnp.inf, imp)
    lane = _iota((1, LANES), 1)
    idx = jnp.where(lane == n_pick + 1, nsel, jnp.zeros((bsz, LANES), I32))
    for s in range(n_pick):
        top = jnp.max(imp, axis=-1, keepdims=True)
        first = jnp.min(jnp.where(imp == top, blk, nsel), axis=-1, keepdims=True)
        idx = jnp.where(lane == s + 1, first, idx)
        imp = jnp.where(blk == first, -jnp.inf, imp)
    idx_ref[...] = idx


def _nsa_pick(imp, n_pick):
    bsz = imp.shape[0]
    return pl.pallas_call(
        functools.partial(_nsa_pick_kernel, n_pick=n_pick),
        out_shape=jax.ShapeDtypeStruct((bsz, LANES), I32),
        in_specs=[pl.BlockSpec(memory_space=pltpu.VMEM)],
        out_specs=pl.BlockSpec(memory_space=pltpu.VMEM),
        name="nsa_pick",
    )(imp)


def _nsa_cmp_step(page_table, rel_table, q3, cache_t, layer, pg, past):
    bsz, n_pages = page_table.shape
    nsel = past // NSA_SEL_BLOCK
    n_pick = min(NSA_TOP_N, nsel + 1) - 2
    nb = pg * (PAGE_SIZE // NSA_CMP_BLOCK)
    grid_spec = pltpu.PrefetchScalarGridSpec(
        num_scalar_prefetch=1,
        grid=(bsz, n_pages // pg),
        in_specs=[pl.BlockSpec(memory_space=pltpu.SMEM),
                  pl.BlockSpec((1, 1, BRANCH_W), lambda b, j, pt: (b, 0, 0)),
                  pl.BlockSpec(memory_space=pl.ANY)],
        out_specs=(pl.BlockSpec((1, 1, BRANCH_W), lambda b, j, pt: (b, 0, 0)),
                   pl.BlockSpec((1, 1, nsel), lambda b, j, pt: (b, 0, 0))),
        scratch_shapes=[pltpu.VMEM((n_pages // pg, 2 * HEAD_DIM, nb), F32)] + _page_scratch(pg, 2 * HEAD_DIM))
    o_cmp, imp = pl.pallas_call(
        functools.partial(_nsa_cmp_step_kernel, layer=layer, pg=pg, past=past),
        out_shape=(jax.ShapeDtypeStruct((bsz, 1, BRANCH_W), F32), jax.ShapeDtypeStruct((bsz, 1, nsel), F32)),
        grid_spec=grid_spec,
        compiler_params=_cp(("arbitrary", "arbitrary")),
        name="nsa_cmp_step",
    )(page_table, rel_table, q3, cache_t)
    return o_cmp, _nsa_pick(imp.reshape(bsz, nsel), n_pick)


def _nsa_sel_step_kernel(idx_ref, pt_ref, tab_ref, q_ref, sm_ref, kv_ref, nw_ref, ocmp_ref, win_ref, *rest,
                         n_past, past):
    blocks, (o_ref,) = rest[:n_past], rest[n_past:]
    b = pl.program_id(0)
    sb = NSA_SEL_BLOCK
    per_page = PAGE_SIZE // sb
    scale = HEAD_DIM ** -0.5
    q8 = _rows8(q_ref[0], HEAD_DIM)
    zero_d = jnp.zeros((1, 1), I32)

    row = _iota((1, PAGE_SIZE), 1)
    lgs, valids = [], []
    for k in range(n_past):
        blk = idx_ref[b, k]
        pos = (blk // per_page) * PAGE_SIZE + row
        lgs.append(_dot(q8, blocks[k][0:HEAD_DIM, :]) * scale + _bias8(past - pos, tab_ref, 0))
        valids.append(row // sb == blk % per_page)
    new_kv = kv_ref[0]
    lg_self = _self_logit(q8, new_kv[:, 2 * HEAD_DIM:3 * HEAD_DIM]) * scale + _bias8(zero_d, tab_ref, 0)
    p, p_self = _softmax_with_self(jnp.concatenate(lgs, axis=-1), jnp.concatenate(valids, axis=-1), lg_self)
    o_sel = p_self * new_kv[:, 3 * HEAD_DIM:4 * HEAD_DIM]
    for k in range(n_past):
        o_sel = o_sel + _dot_nt(p[:, k * PAGE_SIZE:(k + 1) * PAGE_SIZE], blocks[k][HEAD_DIM:2 * HEAD_DIM, :])

    wb = win_ref.shape[1]
    d = wb - _iota((1, wb), 1)
    lg = _dot(q8, win_ref[0:HEAD_DIM, :]) * scale + _bias8(d, tab_ref, 0)
    new_win = nw_ref[0]
    lg_self = _self_logit(q8, new_win[:, 0:HEAD_DIM]) * scale + _bias8(zero_d, tab_ref, 0)
    p, p_self = _softmax_with_self(lg, (d <= NSA_WINDOW) & (past - d >= 0), lg_self)
    o_win = _dot_nt(p, win_ref[HEAD_DIM:2 * HEAD_DIM, :]) + p_self * new_win[:, HEAD_DIM:2 * HEAD_DIM]

    g = _sigmoid(sm_ref[0][:, SM_NSAG:SM_NSAG + 3 * N_HEADS])
    for h in range(N_HEADS):
        hs = slice(h * HEAD_DIM, (h + 1) * HEAD_DIM)
        o_ref[0, :, hs] = (g[:, 3 * h:3 * h + 1] * ocmp_ref[0][:, hs] + g[:, 3 * h + 1:3 * h + 2] * o_sel[h:h + 1, :]
                           + g[:, 3 * h + 2:3 * h + 3] * o_win[h:h + 1, :])


def _nsa_sel_step(idx, page_table, rel_table, q3, sm3, kv3, nw3, ocmp3, win_t, cache_t, layer, past):
    bsz = page_table.shape[0]
    nsel = past // NSA_SEL_BLOCK
    n_past = min(NSA_TOP_N, nsel + 1) - 1
    per_page = PAGE_SIZE // NSA_SEL_BLOCK
    wb = win_t.shape[3]
    tok = lambda w: pl.BlockSpec((1, 1, w), lambda b, idx, pt: (b, 0, 0))

    def blk_spec(k):
        return pl.BlockSpec((None, None, 2 * HEAD_DIM, PAGE_SIZE),
                            lambda b, idx, pt: (layer, pt[b, idx[b, k] // per_page], 1, 0))
    grid_spec = pltpu.PrefetchScalarGridSpec(
        num_scalar_prefetch=2,
        grid=(bsz,),
        in_specs=[pl.BlockSpec(memory_space=pltpu.SMEM), tok(BRANCH_W), tok(LANES), tok(4 * HEAD_DIM), tok(2 * HEAD_DIM),
                  tok(BRANCH_W),
                  pl.BlockSpec((None, None, 2 * HEAD_DIM, wb), lambda b, idx, pt: (layer, b, 0, 0))]
                 + [blk_spec(k) for k in range(n_past)],
        out_specs=tok(BRANCH_W))
    return pl.pallas_call(
        functools.partial(_nsa_sel_step_kernel, n_past=n_past, past=past),
        out_shape=jax.ShapeDtypeStruct((bsz, 1, BRANCH_W), F32),
        grid_spec=grid_spec,
        compiler_params=_cp(("arbitrary",)),
        name="nsa_sel_step",
    )(idx, page_table, rel_table, q3, sm3, kv3, nw3, ocmp3, win_t, *([cache_t] * n_past))


def _index_weights(sm):
    return _col8(sm[:, SM_IDXW:SM_IDXW + IDX_HEADS])


def _dsa_score_step_kernel(pt_ref, iq_ref, sm_ref, pool_ref, sc_ref, new_ref, buf, sem, *, layer, pg):
    slot = _fetch_pages(pt_ref, pool_ref, buf, sem, layer, pg, IDX_DIM, 0)
    pages = [buf.at[slot, k] for k in range(pg)]
    qi8 = _rows8(iq_ref[0], IDX_DIM)
    sm = sm_ref[0]
    wcol = _index_weights(sm)
    cst = IDX_DIM ** -0.5 * IDX_HEADS ** -0.5
    rows = []
    for k in range(pg):
        s = jnp.maximum(_dot(qi8, pages[k][...]), 0.0)
        rows.append(jnp.sum(s * wcol, axis=0, keepdims=True) * cst)
    sc_ref[0] = jnp.concatenate(rows, axis=0)
    s_new = jnp.maximum(_self_logit(qi8, sm[:, SM_IDXK:SM_IDXK + IDX_DIM]), 0.0)
    sc_new = jnp.sum(s_new * wcol, axis=0, keepdims=True) * cst
    new_ref[0] = jnp.where(_iota((1, LANES), 1) == 0, sc_new, -jnp.inf)


def _dsa_score_step(page_table, iq3, sm3, cache_idx_t, layer, pg):
    bsz, n_pages = page_table.shape
    tok = lambda w: pl.BlockSpec((1, 1, w), lambda b, j, pt: (b, 0, 0))
    grid_spec = pltpu.PrefetchScalarGridSpec(
        num_scalar_prefetch=1,
        grid=(bsz, n_pages // pg),
        in_specs=[tok(IDX_HEADS * IDX_DIM), tok(LANES), pl.BlockSpec(memory_space=pl.ANY)],
        out_specs=(pl.BlockSpec((1, pg, PAGE_SIZE), lambda b, j, pt: (b, j, 0)), tok(LANES)),
        scratch_shapes=_page_scratch(pg, IDX_DIM))
    return pl.pallas_call(
        functools.partial(_dsa_score_step_kernel, layer=layer, pg=pg),
        out_shape=(jax.ShapeDtypeStruct((bsz, n_pages, PAGE_SIZE), F32), jax.ShapeDtypeStruct((bsz, 1, LANES), F32)),
        grid_spec=grid_spec,
        compiler_params=_cp(("arbitrary", "arbitrary")),
        name="dsa_score_step",
    )(page_table, iq3, sm3, cache_idx_t)


def _dsa_thr_step_kernel(sc_ref, new_ref, thr_ref, need_ref, tie_ref, *, topk):
    key = _ordered_key(sc_ref[...])
    key_new = _ordered_key(new_ref[...])
    bt = key.shape[0]

    def count(pred):
        return (jnp.sum(jnp.where(pred(key), 1, 0), axis=-1, keepdims=True)
                + jnp.sum(jnp.where(pred(key_new), 1, 0), axis=-1, keepdims=True))

    thr = _kth_largest_key(lambda cand: count(lambda x: x >= cand), topk, (bt, 1))
    need = topk - count(lambda x: x > thr)
    thr_ref[...] = jnp.broadcast_to(thr, thr_ref.shape)
    need_ref[...] = jnp.broadcast_to(need, need_ref.shape)
    tie_ref[...] = jnp.broadcast_to(jnp.where(count(lambda x: x == thr) > need, 1, 0), tie_ref.shape)


def _dsa_thr_step(scores, new, topk, bt):
    bsz, p = scores.shape
    out = pl.BlockSpec((bt, LANES), lambda i: (i, 0))
    return pl.pallas_call(
        functools.partial(_dsa_thr_step_kernel, topk=topk),
        out_shape=(jax.ShapeDtypeStruct((bsz, LANES), I32),) * 3,
        grid=(bsz // bt,),
        in_specs=[pl.BlockSpec((bt, p), lambda i: (i, 0)), pl.BlockSpec((bt, LANES), lambda i: (i, 0))],
        out_specs=(out, out, out),
        compiler_params=_cp(("arbitrary",)),
        name="dsa_thr_step",
    )(scores, new)


def _online_self(lg_self, v_row, m_s, l_s, acc_s):
    m_old = m_s[...]
    m_new = jnp.maximum(m_old, lg_self)
    m_safe = jnp.where(m_new == -jnp.inf, 0.0, m_new)
    alpha = jnp.exp(m_old - m_safe)
    p_self = jnp.exp(lg_self - m_safe)
    den = alpha * l_s[...] + p_self
    return (alpha * acc_s[...] + p_self * v_row) / jnp.maximum(den, 1e-30)


def _dsa_att_step_kernel(pt_ref, tie_ref, tab_ref, q_ref, kv_ref, sc_ref, new_ref, thr_ref, need_ref, pool_ref,
                         o_ref, m_s, l_s, acc_s, run_s, buf, sem, *, layer, pg, past):
    slot = _fetch_pages(pt_ref, pool_ref, buf, sem, layer, pg, 2 * HEAD_DIM, 0)
    pages = [buf.at[slot, k] for k in range(pg)]
    b = pl.program_id(0)
    j = pl.program_id(1)
    scale = HEAD_DIM ** -0.5
    nk = pg * PAGE_SIZE

    @pl.when(j == 0)
    def _():
        _flash_init(m_s, l_s, acc_s)
        run_s[...] = jnp.zeros(run_s.shape, F32)

    q8 = _rows8(q_ref[0], HEAD_DIM)
    thr = thr_ref[0][:, 0:1]
    need = need_ref[0][:, 0:1].astype(F32)
    has_ties = tie_ref[b] > 0
    key = _ordered_key(sc_ref[0])

    def pick_plain(run):
        return jnp.where(key >= thr, 1.0, 0.0), run

    def pick_ties(run):
        eq = key == thr
        eqf = jnp.where(eq, 1.0, 0.0)
        strict_upper = (_iota((PAGE_SIZE, PAGE_SIZE), 0) < _iota((PAGE_SIZE, PAGE_SIZE), 1)).astype(BF16)
        inside = jnp.dot(eqf.astype(BF16), strict_upper, preferred_element_type=F32)
        cnt = jnp.sum(eqf, axis=-1, keepdims=True)
        rows = []
        for k in range(pg):
            rows.append(inside[k:k + 1] + run)
            run = run + cnt[k:k + 1]
        before = jnp.concatenate(rows, axis=0)
        return jnp.where((key > thr) | (eq & (before < need)), 1.0, 0.0), run

    chosen, run = lax.cond(has_ties, pick_ties, pick_plain, run_s[...])
    run_s[...] = run

    lg = jnp.concatenate([_dot(q8, pages[k][0:HEAD_DIM, :]) for k in range(pg)], axis=-1) * scale
    mask = jnp.concatenate([chosen[k:k + 1] for k in range(pg)], axis=-1) > 0.5
    r8 = _iota((8, 1), 0)
    far = jnp.zeros((8, 1), F32)
    for h in range(N_HEADS):
        far = jnp.where(r8 == h, tab_ref[REL_BUCKETS - 1, N_HEADS + h], far)
    d0 = past - j * nk
    bias = lax.cond(d0 - (nk - 1) >= REL_MAX_DIST,
                    lambda: jnp.broadcast_to(far, (8, nk)),
                    lambda: _bias8(d0 - _iota((1, nk), 1), tab_ref, N_HEADS))
    p, alpha = _softmax_step(lg + bias, mask, m_s, l_s)
    acc = alpha * acc_s[...]
    for k in range(pg):
        acc = acc + _dot_nt(p[:, k * PAGE_SIZE:(k + 1) * PAGE_SIZE], pages[k][HEAD_DIM:2 * HEAD_DIM, :])
    acc_s[...] = acc

    @pl.when(j == pl.num_programs(1) - 1)
    def _():
        key_new = _ordered_key(new_ref[0][:, 0:1])
        limit = jnp.where(has_ties, need, jnp.float32(3.0e38))
        take = (key_new > thr) | ((key_new == thr) & (run < limit))
        new_kv = kv_ref[0]
        lg_self = _self_logit(q8, new_kv[:, 0:HEAD_DIM]) * scale + _bias8(jnp.zeros((1, 1), I32), tab_ref, N_HEADS)
        lg_self = jnp.where(take, lg_self, -jnp.inf)
        _write_heads(o_ref, _online_self(lg_self, new_kv[:, HEAD_DIM:2 * HEAD_DIM], m_s, l_s, acc_s), HEAD_DIM)


def _dsa_att_step(page_table, tie, rel_table, q3, kv3, scores, new, thr, need, cache_t, layer, pg, past):
    bsz, n_pages = page_table.shape
    tok = lambda w: pl.BlockSpec((1, 1, w), lambda b, j, pt, tie: (b, 0, 0))
    grid_spec = pltpu.PrefetchScalarGridSpec(
        num_scalar_prefetch=2,
        grid=(bsz, n_pages // pg),
        in_specs=[pl.BlockSpec(memory_space=pltpu.SMEM), tok(BRANCH_W), tok(2 * HEAD_DIM),
                  pl.BlockSpec((1, pg, PAGE_SIZE), lambda b, j, pt, tie: (b, j, 0)), tok(LANES), tok(LANES), tok(LANES),
                  pl.BlockSpec(memory_space=pl.ANY)],
        out_specs=tok(BRANCH_W),
        scratch_shapes=[pltpu.VMEM((8, 1), F32), pltpu.VMEM((8, 1), F32), pltpu.VMEM((8, HEAD_DIM), F32),
                        pltpu.VMEM((1, 1), F32)] + _page_scratch(pg, 2 * HEAD_DIM))
    return pl.pallas_call(
        functools.partial(_dsa_att_step_kernel, layer=layer, pg=pg, past=past),
        out_shape=jax.ShapeDtypeStruct((bsz, 1, BRANCH_W), F32),
        grid_spec=grid_spec,
        compiler_params=_cp(("arbitrary", "arbitrary")),
        name="dsa_att_step",
    )(page_table, tie, rel_table, q3, kv3, scores, new, thr, need, cache_t)


def _mla_step_kernel(pt_ref, qc_ref, kc_ref, wuv_ref, pool_ref, o_ref, m_s, l_s, acc_s, buf, sem, *, layer, pg):
    slot = _fetch_pages(pt_ref, pool_ref, buf, sem, layer, pg, KV_LORA + QK_ROPE, 0)
    pages = [buf.at[slot, k] for k in range(pg)]
    j = pl.program_id(1)
    scale = MLA_HEAD_QK ** -0.5
    kw = KV_LORA + QK_ROPE

    @pl.when(j == 0)
    def _():
        _flash_init(m_s, l_s, acc_s)

    q8 = _rows8(qc_ref[0], MLA_QW)[:, 0:kw]
    lg = jnp.concatenate([_dot(q8, pages[k][...]) for k in range(pg)], axis=-1) * scale
    p, alpha = _softmax_step(lg, True, m_s, l_s)
    acc = alpha * acc_s[...]
    for k in range(pg):
        acc = acc + _dot_nt(p[:, k * PAGE_SIZE:(k + 1) * PAGE_SIZE], pages[k][0:KV_LORA, :])
    acc_s[...] = acc

    @pl.when(j == pl.num_programs(1) - 1)
    def _():
        new_k = kc_ref[0].astype(F32)
        lg_self = _self_logit(q8, new_k[:, 0:kw]) * scale
        o_lat = _online_self(lg_self, new_k[:, 0:KV_LORA], m_s, l_s, acc_s)
        for h in range(N_HEADS):
            o_ref[0, :, h * V_DIM:(h + 1) * V_DIM] = _dot(o_lat, wuv_ref[h])[h:h + 1, :]


def _mla_step(page_table, qc3, kc3, wuv, cache_mla, layer, pg):
    bsz, n_pages = page_table.shape
    tok = lambda w: pl.BlockSpec((1, 1, w), lambda b, j, pt: (b, 0, 0))
    grid_spec = pltpu.PrefetchScalarGridSpec(
        num_scalar_prefetch=1,
        grid=(bsz, n_pages // pg),
        in_specs=[tok(N_HEADS * MLA_QW), tok(MLA_QW), pl.BlockSpec(wuv.shape, lambda b, j, pt: (0, 0, 0)),
                  pl.BlockSpec(memory_space=pl.ANY)],
        out_specs=tok(BRANCH_W),
        scratch_shapes=[pltpu.VMEM((8, 1), F32), pltpu.VMEM((8, 1), F32), pltpu.VMEM((8, KV_LORA), F32)]
                       + _page_scratch(pg, KV_LORA + QK_ROPE))
    return pl.pallas_call(
        functools.partial(_mla_step_kernel, layer=layer, pg=pg),
        out_shape=jax.ShapeDtypeStruct((bsz, 1, BRANCH_W), F32),
        grid_spec=grid_spec,
        compiler_params=_cp(("arbitrary", "arbitrary")),
        name="mla_step",
    )(page_table, qc3, kc3, wuv, cache_mla)


def _permute_w_in(w_in):
    d = w_in.shape[0]
    o = {}
    off = 0
    for name, w in (('nsa_q', 256), ('nsa_kv', 384), ('nsa_g', 12), ('dsa_q', 256), ('dsa_kv', 128), ('idx_q', 256),
                    ('idx_k', 64), ('idx_w', 4), ('hg', 1024), ('mla_cq', 256), ('mla_ckv', 128), ('mla_kr', 32),
                    ('merge_g', w_in.shape[1] - 2800)):
        o[name] = w_in[:, off:off + w]
        off += w
    pad = jnp.zeros((d, LANES - 112), w_in.dtype)
    small = [o['idx_k'], o['mla_kr'], o['nsa_g'], o['idx_w'], pad]
    cols = [o['nsa_q'], o['nsa_kv'], o['dsa_q'], o['dsa_kv'], o['idx_q'], o['hg'], o['mla_cq'], o['mla_ckv']] + small \
           + [o['merge_g']]
    w_all = jnp.concatenate(cols, axis=1).astype(BF16)
    w_row = jnp.concatenate([o['nsa_kv'], o['dsa_kv'], o['hg'], o['mla_cq'], o['mla_ckv']] + small + [o['merge_g']],
                            axis=1).astype(BF16)
    hd = HEAD_DIM
    v_cols = [o['nsa_kv'][:, hd:2 * hd], o['nsa_kv'][:, 3 * hd:4 * hd], o['nsa_kv'][:, 5 * hd:6 * hd], o['dsa_kv'][:, hd:2 * hd]]
    w_fm = jnp.concatenate([o['nsa_q'], o['dsa_q'], o['idx_q']] + v_cols + [o['nsa_g'], o['idx_w']], axis=1).T.astype(BF16)
    return w_all, w_row, w_fm


def _layer_weights(l, w):
    hq = MLA_HEAD_QK
    wuq = w['w_uq'][l].reshape(Q_LORA, N_HEADS, hq)
    hr = QK_ROPE // 2
    w_all, w_row, w_fm = _permute_w_in(w['w_in'][l])
    return dict(
        w_in=w_all, w_row=w_row, w_fm=w_fm,
        wuvT=jnp.transpose(w['w_uv'][l], (1, 2, 0)).astype(BF16),
        wn=wuq[:, :, :QK_NOPE].reshape(Q_LORA, N_HEADS * QK_NOPE).astype(BF16),
        wr1=wuq[:, :, QK_NOPE:QK_NOPE + hr].reshape(Q_LORA, N_HEADS * hr).astype(BF16),
        wr2=wuq[:, :, QK_NOPE + hr:].reshape(Q_LORA, N_HEADS * hr).astype(BF16),
        wukT=jnp.transpose(w['w_uk'][l], (1, 2, 0)).astype(BF16),
        wuv=jnp.transpose(w['w_uv'][l], (1, 0, 2)).astype(BF16),
        wb=w['w_branch'][l].astype(BF16),
        wo=w['w_out'][l].astype(BF16),
        w_up=w['w_up'][l].astype(BF16),
        w_down=w['w_down'][l].astype(BF16),
        conv_w=w['conv_w'][l],
        conv_b=w['conv_b'][l][None, :],
        g_pre_mix=w['g_pre_mix'][l][None, :], g_post_mix=w['g_post_mix'][l][None, :],
        g_pre_ffn=w['g_pre_ffn'][l][None, :], g_post_ffn=w['g_post_ffn'][l][None, :],
        gq=w['mla_q_norm'][l][None, :], gkv=w['mla_kv_norm'][l][None, :],
        hgn=jnp.tile(w['hg_norm'][l], N_HEADS)[None, :],
    )


def _largest_divisor(n, cap):
    for c in range(min(n, cap), 0, -1):
        if n % c == 0:
            return c
    return 1


def kernel(x_prompt, x_sample, cache_nsa_kv, cache_dsa_kv, cache_dsa_idx, cache_mla, state_nsa_win, state_hgrn, state_ffn_conv, page_table, c_prompt, c_sample, rel_table, w_ada, b_ada, g_pre_mix, g_post_mix, g_pre_ffn, g_post_ffn, w_in, hg_lb, hg_norm, mla_q_norm, mla_kv_norm, w_uq, w_uk, w_uv, w_branch, w_out, w_up, conv_w, conv_b, w_down):
    weights = dict(w_in=w_in, w_uq=w_uq, w_uk=w_uk, w_uv=w_uv, w_branch=w_branch, w_out=w_out, w_up=w_up,
                   w_down=w_down, conv_w=conv_w, conv_b=conv_b, g_pre_mix=g_pre_mix, g_post_mix=g_post_mix,
                   g_pre_ffn=g_pre_ffn, g_post_ffn=g_post_ffn, mla_q_norm=mla_q_norm, mla_kv_norm=mla_kv_norm,
                   hg_norm=hg_norm)
    depth = w_in.shape[0]
    batch, seq, d = x_prompt.shape
    dec, dec_seq, _ = x_sample.shape
    assert dec_seq == 1 and seq % 256 == 0 and dec % 8 == 0
    n_pool = cache_nsa_kv.shape[1]
    n_pages = page_table.shape[1]
    past = n_pages * PAGE_SIZE
    dff = w_down.shape[1]

    tq = 256
    tm_p = 256
    tps_p = seq // tm_p
    tm_s = _largest_divisor(dec, 128)
    pg = _largest_divisor(n_pages, 16)
    pg_cmp = _largest_divisor(n_pages, 32)
    tm_ffn = 512 if seq % 512 == 0 else tm_p

    gam = jax.nn.softmax(hg_lb.astype(F32), axis=0)
    cum = jnp.cumsum(gam, axis=0)
    lower = cum - cum[0]

    inv = ROPE_THETA ** (-jnp.arange(0, QK_ROPE, 2, dtype=F32) / QK_ROPE)
    inv4 = jnp.tile(inv, N_HEADS)[None, :]

    nsel = seq // NSA_SEL_BLOCK
    jj = jnp.arange(2 * nsel, dtype=I32)
    cend = jnp.where(jj < nsel, NSA_SEL_BLOCK * jj + (NSA_CMP_BLOCK - 1),
                     NSA_SEL_BLOCK * (jj - nsel) + (NSA_SEL_BLOCK - 1))[:, None]
    bias_t, bias_cmp = _bias_tiles(rel_table, cend, tq, tq, seq)

    mod = _ada(jnp.concatenate([c_prompt, c_sample], axis=0), w_ada, b_ada)

    nsa_t = jnp.transpose(cache_nsa_kv, (0, 1, 3, 4, 2)).reshape(depth, n_pool, 4 * HEAD_DIM, PAGE_SIZE)
    dsa_t = jnp.transpose(cache_dsa_kv, (0, 1, 3, 4, 2)).reshape(depth, n_pool, 2 * HEAD_DIM, PAGE_SIZE)
    idx_t = jnp.transpose(cache_dsa_idx, (0, 1, 3, 2))
    mla_t = jnp.transpose(cache_mla, (0, 1, 3, 2))
    win_t = jnp.transpose(state_nsa_win, (0, 1, 3, 4, 2)).reshape(depth, dec, 2 * HEAD_DIM, state_nsa_win.shape[2])

    xp = x_prompt.reshape(batch * seq, d)
    xs = x_sample.reshape(dec, d)
    outs_p, outs_s = [], []
    for l in range(depth):
        lw = _layer_weights(l, weights)
        lower_l = lower[l][None, :]
        mp = [mod[l, :batch, k * d:(k + 1) * d][:, None, :] for k in range(6)]
        ms = [mod[l, batch:, k * d:(k + 1) * d][None, :, :] for k in range(6)]

        z = dict(zip(ROW_NAMES + FM_NAMES,
                     _inproj_fm(xp, lw['g_pre_mix'], mp[0], mp[1], lw['w_row'], lw['w_fm'], tm_p, tps_p)))
        qc_t, kc, mla_st, ckv_t = _mla_proj(z['mla_cq'], z['mla_ckv'], z['small'], inv4, lw['gq'], lw['gkv'],
                                            lw['wn'], lw['wr1'], lw['wr2'], lw['wukT'], tm_p, seq, 0, feature_major=True)
        o_nsa = _nsa_prompt(z['nsa_qT'], z['smallT'], z['nsa_kv4'], z['nsa_win'], z['vT'], bias_t, bias_cmp,
                            batch, seq, tq)
        o_dsa = _dsa_prompt(z['dsa_qT'], z['idx_qT'], z['smallT'], z['small'], z['dsa_kv'], z['vT'], bias_t,
                            batch, seq, tq)
        o_mla = _mla_prompt(qc_t, kc, ckv_t, lw['wuvT'], batch, seq, tq)
        o_hg, s_new = _hgrn_prompt(z['hg'], lower_l, lw['hgn'], batch, seq, 256)
        xp = _merge(xp, (o_nsa, o_dsa, o_hg, o_mla), z['merge_g'], mp[2], lw['g_post_mix'], lw['wb'], lw['wo'], tm_p, tps_p)
        xp, csg, csv = _ffn(xp, lw['g_pre_ffn'], mp[3], mp[4], mp[5], lw['g_post_ffn'], lw['w_up'], lw['conv_w'],
                            lw['conv_b'], lw['w_down'], tm_ffn, seq // tm_ffn)
        wl = min(NSA_WINDOW, seq)
        outs_p.append((z['nsa_kv4'].reshape(batch, seq, 4, HEAD_DIM),
                       z['dsa_kv'].reshape(batch, seq, 2, HEAD_DIM),
                       z['small'][:, SM_IDXK:SM_IDXK + IDX_DIM].reshape(batch, seq, IDX_DIM),
                       mla_st.reshape(batch, seq, KV_LORA + QK_ROPE),
                       z['nsa_win'].reshape(batch, seq, 2, HEAD_DIM)[:, seq - wl:],
                       s_new,
                       jnp.concatenate([jnp.swapaxes(csg, 1, 2).reshape(batch, CONV_W - 1, dff),
                                        jnp.swapaxes(csv, 1, 2).reshape(batch, CONV_W - 1, dff)], axis=-1)))

        z = dict(zip(OUT_NAMES, _inproj(xs, lw['g_pre_mix'], ms[0], ms[1], lw['w_in'], tm_s, 1)))
        qc, kc, mla_st = _mla_proj(z['mla_cq'], z['mla_ckv'], z['small'], inv4, lw['gq'], lw['gkv'],
                                   lw['wn'], lw['wr1'], lw['wr2'], lw['wukT'], tm_s, 1, past)
        r3 = lambda a: a.reshape(dec, 1, a.shape[-1])
        q3, sm3, kv3, nw3 = r3(z['nsa_q']), r3(z['small']), r3(z['nsa_kv4']), r3(z['nsa_win'])
        o_cmp, sel_idx = _nsa_cmp_step(page_table, rel_table, q3, nsa_t, l, pg_cmp, past)
        o_nsa = _nsa_sel_step(sel_idx, page_table, rel_table, q3, sm3, kv3, nw3, o_cmp, win_t,
                              nsa_t, l, past)
        scores, sc_new = _dsa_score_step(page_table, r3(z['idx_q']), sm3, idx_t, l, pg)
        thr, need, tie = _dsa_thr_step(scores.reshape(dec, past), sc_new.reshape(dec, LANES),
                                       min(DSA_TOPK_MAX, (past + 1) // 4), 8)
        o_dsa = _dsa_att_step(page_table, tie[:, 0], rel_table, r3(z['dsa_q']), r3(z['dsa_kv']), scores, sc_new,
                              r3(thr), r3(need), dsa_t, l, pg, past)
        o_mla = _mla_step(page_table, r3(qc), r3(kc), lw['wuv'], mla_t, l, pg)
        o_hg, s_new = _hgrn_step(z['hg'], lower_l, lw['hgn'], state_hgrn[l], 8)
        xs = _merge(xs, (o_nsa.reshape(dec, BRANCH_W), o_dsa.reshape(dec, BRANCH_W), o_hg, o_mla.reshape(dec, BRANCH_W)),
                    z['merge_g'], ms[2], lw['g_post_mix'], lw['wb'], lw['wo'], tm_s, 1)
        prev = state_ffn_conv[l]
        xs, ug, uv = _ffn(xs, lw['g_pre_ffn'], ms[3], ms[4], ms[5], lw['g_post_ffn'], lw['w_up'], lw['conv_w'],
                          lw['conv_b'], lw['w_down'], tm_s, 1, prev=(prev[:, 0], prev[:, 1]))
        win_all = jnp.concatenate([state_nsa_win[l], z['nsa_win'].reshape(dec, 1, 2, HEAD_DIM)], axis=1)
        wl = min(NSA_WINDOW, win_all.shape[1])
        outs_s.append((z['nsa_kv4'].reshape(dec, 1, 4, HEAD_DIM),
                       z['dsa_kv'].reshape(dec, 1, 2, HEAD_DIM),
                       z['small'][:, SM_IDXK:SM_IDXK + IDX_DIM].reshape(dec, 1, IDX_DIM),
                       mla_st.reshape(dec, 1, KV_LORA + QK_ROPE),
                       win_all[:, win_all.shape[1] - wl:],
                       s_new,
                       jnp.stack([prev[:, 1], jnp.concatenate([ug, uv], axis=-1)], axis=1)))

    sp = [jnp.stack(v) for v in zip(*outs_p)]
    ss = [jnp.stack(v) for v in zip(*outs_s)]
    return (xp.reshape(batch, seq, d), xs.reshape(dec, 1, d),
            sp[0], ss[0], sp[1], ss[1], sp[2], ss[2], sp[3], ss[3], sp[4], ss[4], sp[5], ss[5], sp[6], ss[6])
```

```python
import functools
import math

import jax
import jax.numpy as jnp
from jax import lax
from jax.experimental import pallas as pl
from jax.experimental.pallas import tpu as pltpu

F32, BF16, I32 = jnp.float32, jnp.bfloat16, jnp.int32

N_HEADS = 4
HEAD_DIM = 64
BRANCH_W = N_HEADS * HEAD_DIM
N_BRANCH = 4
NSA_CMP_BLOCK = 32
NSA_SEL_BLOCK = 64
NSA_TOP_N = 16
NSA_WINDOW = 512
FORCE_SCORE = 1e9
IDX_HEADS = 4
IDX_DIM = 64
DSA_TOPK_MAX = 256
HGRN_CHUNK = 16
Q_LORA = 256
KV_LORA = 128
QK_NOPE = 64
QK_ROPE = 32
V_DIM = 64
MLA_HEAD_QK = QK_NOPE + QK_ROPE
ROPE_THETA = 10000.0
REL_BUCKETS = 32
REL_MAX_DIST = 512
CONV_W = 3
EPS = 1e-6
PAGE_SIZE = 128

LANES = 128
VMEM_LIMIT = 56 * 1024 * 1024

OUT_WIDTHS = (256, 256, 128, 256, 128, 256, 1024, 256, 128, 128, 4096)
OUT_NAMES = ('nsa_q', 'nsa_kv4', 'nsa_win', 'dsa_q', 'dsa_kv', 'idx_q', 'hg', 'mla_cq', 'mla_ckv', 'small', 'merge_g')
SM_IDXK, SM_KR, SM_NSAG, SM_IDXW = 0, 64, 96, 108
MLA_QW = 256
INT_MIN = -2 ** 31
PAGE_SLOTS = 3


def _cp(sem, vmem=VMEM_LIMIT):
    return pltpu.CompilerParams(dimension_semantics=sem, vmem_limit_bytes=vmem)


def _dot(a, b):
    return jnp.dot(a.astype(BF16), b.astype(BF16), preferred_element_type=F32)


def _dot_nt(a, b):
    return lax.dot_general(a.astype(BF16), b.astype(BF16), (((1,), (1,)), ((), ())), preferred_element_type=F32)


def _dot_tn(a, b):
    return lax.dot_general(a.astype(BF16), b.astype(BF16), (((0,), (0,)), ((), ())), preferred_element_type=F32)


def _dot_hilo(a, g):
    hi = a.astype(BF16)
    lo = (a - hi.astype(F32)).astype(BF16)
    return jnp.dot(hi, g, preferred_element_type=F32) + jnp.dot(lo, g, preferred_element_type=F32)


def _rms(x, g):
    return x * lax.rsqrt(jnp.mean(x * x, axis=-1, keepdims=True) + EPS) * g


def _sigmoid(x):
    return 1.0 / (1.0 + jnp.exp(-x))


def _silu(x):
    return x * _sigmoid(x)


def _iota(shape, dim):
    return lax.broadcasted_iota(I32, shape, dim)


def _head_block_ones(n):
    return (_iota((n, n), 0) // HEAD_DIM == _iota((n, n), 1) // HEAD_DIM)


def _rel_bias(d, tab_ref, heads):
    exact = REL_BUCKETS // 2
    n = jnp.maximum(d, 0)
    nf = jnp.maximum(n, 1).astype(F32)
    log_b = exact + (jnp.log(nf / exact) / math.log(REL_MAX_DIST / exact) * (REL_BUCKETS - exact)).astype(I32)
    bucket = jnp.where(n < exact, n, jnp.minimum(log_b, REL_BUCKETS - 1))
    outs = []
    for h in heads:
        o = jnp.full(d.shape, tab_ref[0, h], F32)
        for k in range(1, REL_BUCKETS):
            o = jnp.where(bucket == k, tab_ref[k, h], o)
        outs.append(o)
    return outs


def _ordered_key(x):
    x = jnp.where(x == 0.0, 0.0, x)
    bits = pltpu.bitcast(x, I32)
    return jnp.where(bits < 0, bits ^ 0x7FFFFFFF, bits)


def _kth_largest_key(count_ge, k, shape):
    t = jnp.full(shape, INT_MIN, I32)
    zero = jnp.zeros(shape, I32)
    t = jnp.where(count_ge(zero) >= k, zero, t)
    for bit in range(30, -1, -1):
        cand = t + (1 << bit)
        t = jnp.where(count_ge(cand) >= k, cand, t)
    return t


def _softmax_step(lg, mask, m_ref, l_ref):
    lg = jnp.where(mask, lg, -jnp.inf)
    m_old = m_ref[...]
    m_new = jnp.maximum(m_old, jnp.max(lg, axis=-1, keepdims=True))
    m_safe = jnp.where(m_new == -jnp.inf, 0.0, m_new)
    p = jnp.exp(lg - m_safe)
    alpha = jnp.exp(m_old - m_safe)
    l_ref[...] = alpha * l_ref[...] + jnp.sum(p, axis=-1, keepdims=True)
    m_ref[...] = m_new
    return p, alpha


def _ada_kernel(c_ref, w_ref, b_ref, o_ref):
    o_ref[0] = _dot(_silu(c_ref[...]), w_ref[0]) + b_ref[0]


def _ada(c, w_ada, b_ada):
    depth, d, n = w_ada.shape
    bc = c.shape[0]
    tn = 1536 if n % 1536 == 0 else n
    return pl.pallas_call(
        _ada_kernel,
        out_shape=jax.ShapeDtypeStruct((depth, bc, n), F32),
        grid=(depth, n // tn),
        in_specs=[pl.BlockSpec((bc, d), lambda l, j: (0, 0)),
                  pl.BlockSpec((1, d, tn), lambda l, j: (l, 0, j)),
                  pl.BlockSpec((1, 1, tn), lambda l, j: (l, 0, j))],
        out_specs=pl.BlockSpec((1, bc, tn), lambda l, j: (l, 0, j)),
        compiler_params=_cp(("arbitrary", "arbitrary")),
        name="ada",
    )(c, w_ada, b_ada.reshape(depth, 1, n))


def _bias_kernel(tab_ref, cend_ref, tile_ref, cmp_ref, *, tq, tk, seq):
    j = _iota((tk, tq), 0)
    i = _iota((tk, tq), 1)
    for k in range(4):
        bs = _rel_bias(k * tk + i - j, tab_ref, range(2 * N_HEADS))
        for h in range(2 * N_HEADS):
            tile_ref[k, h] = bs[h]
    t = _iota((cend_ref.shape[0], seq), 1)
    bs = _rel_bias(t - cend_ref[...], tab_ref, range(N_HEADS))
    for h in range(N_HEADS):
        cmp_ref[h] = bs[h]


def _bias_tiles(rel_table, cend, tq, tk, seq):
    nc = cend.shape[0]
    return pl.pallas_call(
        functools.partial(_bias_kernel, tq=tq, tk=tk, seq=seq),
        out_shape=(jax.ShapeDtypeStruct((4, 2 * N_HEADS, tk, tq), F32),
                   jax.ShapeDtypeStruct((N_HEADS, nc, seq), F32)),
        in_specs=[pl.BlockSpec(memory_space=pltpu.SMEM),
                  pl.BlockSpec(memory_space=pltpu.VMEM)],
        out_specs=(pl.BlockSpec(memory_space=pltpu.VMEM), pl.BlockSpec(memory_space=pltpu.VMEM)),
        compiler_params=pltpu.CompilerParams(vmem_limit_bytes=VMEM_LIMIT),
        name="rel_bias_tiles",
    )(rel_table, cend)


def _inproj_kernel(x_ref, g_ref, sh_ref, sc_ref, w_ref, *o_refs):
    h = _rms(x_ref[...], g_ref[...]) * (1.0 + sc_ref[0]) + sh_ref[0]
    hb = h.astype(BF16)
    off = 0
    for o_ref, w in zip(o_refs, OUT_WIDTHS):
        o_ref[...] = jnp.dot(hb, w_ref[:, off:off + w], preferred_element_type=F32)
        off += w


ROW_WIDTHS = (256, 128, 128, 1024, 256, 128, 128, 4096)
ROW_NAMES = ('nsa_kv4', 'nsa_win', 'dsa_kv', 'hg', 'mla_cq', 'mla_ckv', 'small', 'merge_g')
FM_WIDTHS = (256, 256, 256, 256, 16)
FM_NAMES = ('nsa_qT', 'dsa_qT', 'idx_qT', 'vT', 'smallT')


def _inproj_fm_kernel(x_ref, g_ref, sh_ref, sc_ref, w_ref, wt_ref, *o_refs):
    h = _rms(x_ref[...], g_ref[...]) * (1.0 + sc_ref[0]) + sh_ref[0]
    hb = h.astype(BF16)
    off = 0
    for o_ref, w in zip(o_refs[:len(ROW_WIDTHS)], ROW_WIDTHS):
        o_ref[...] = jnp.dot(hb, w_ref[:, off:off + w], preferred_element_type=F32)
        off += w
    off = 0
    for o_ref, w in zip(o_refs[len(ROW_WIDTHS):], FM_WIDTHS):
        o_ref[0] = _dot_nt(wt_ref[off:off + w, :], hb).astype(o_ref.dtype)
        off += w


def _inproj_fm(x, g, shift, scale, w_row, w_fm, tm, tps):
    m, d = x.shape
    fm_dtypes = (BF16, BF16, BF16, BF16, F32)
    return pl.pallas_call(
        _inproj_fm_kernel,
        out_shape=tuple(jax.ShapeDtypeStruct((m, w), F32) for w in ROW_WIDTHS)
                  + tuple(jax.ShapeDtypeStruct((m // tm, w, tm), dt) for w, dt in zip(FM_WIDTHS, fm_dtypes)),
        grid=(m // tm,),
        in_specs=[pl.BlockSpec((tm, d), lambda i: (i, 0)),
                  pl.BlockSpec((1, d), lambda i: (0, 0)),
                  _mod_spec(shift, tm, tps), _mod_spec(scale, tm, tps),
                  pl.BlockSpec(w_row.shape, lambda i: (0, 0), pipeline_mode=pl.Buffered(1)),
                  pl.BlockSpec(w_fm.shape, lambda i: (0, 0), pipeline_mode=pl.Buffered(1))],
        out_specs=tuple(pl.BlockSpec((tm, w), lambda i: (i, 0)) for w in ROW_WIDTHS)
                  + tuple(pl.BlockSpec((1, w, tm), lambda i: (i, 0, 0)) for w in FM_WIDTHS),
        compiler_params=_cp(("arbitrary",)),
        name="inproj_fm",
    )(x, g, shift, scale, w_row, w_fm)


def _mod_spec(arr, tm, tps):
    d = arr.shape[-1]
    if arr.shape[1] == 1:
        return pl.BlockSpec((1, 1, d), lambda i, *_: (i // tps, 0, 0))
    return pl.BlockSpec((1, tm, d), lambda i, *_: (0, i, 0))


def _inproj(x, g, shift, scale, w_p, tm, tps):
    m, d = x.shape
    n = w_p.shape[1]
    return pl.pallas_call(
        _inproj_kernel,
        out_shape=tuple(jax.ShapeDtypeStruct((m, w), F32) for w in OUT_WIDTHS),
        grid=(m // tm,),
        in_specs=[pl.BlockSpec((tm, d), lambda i: (i, 0)),
                  pl.BlockSpec((1, d), lambda i: (0, 0)),
                  _mod_spec(shift, tm, tps), _mod_spec(scale, tm, tps),
                  pl.BlockSpec((d, n), lambda i: (0, 0), pipeline_mode=pl.Buffered(1))],
        out_specs=tuple(pl.BlockSpec((tm, w), lambda i: (i, 0)) for w in OUT_WIDTHS),
        compiler_params=_cp(("arbitrary",)),
        name="inproj",
    )(x, g, shift, scale, w_p)


def _mla_proj_kernel(cq_ref, ckv_ref, sm_ref, inv_ref, gq_ref, gkv_ref, wn_ref, wr1_ref, wr2_ref, wuk_ref,
                     qc_ref, kc_ref, st_ref, *rest, tm, seq, pos0):
    feature_major = len(rest) > 0
    i = pl.program_id(0)
    qn = _rms(cq_ref[...], gq_ref[...]).astype(BF16)
    q_nope = jnp.dot(qn, wn_ref[...], preferred_element_type=F32)
    r1 = jnp.dot(qn, wr1_ref[...], preferred_element_type=F32)
    r2 = jnp.dot(qn, wr2_ref[...], preferred_element_type=F32)
    pos = ((i * tm + _iota((tm, 1), 0)) % seq + pos0).astype(F32)
    ang = pos * inv_ref[...]
    c, s = jnp.cos(ang), jnp.sin(ang)
    rot1 = r1 * c - r2 * s
    rot2 = r1 * s + r2 * c
    hr = QK_ROPE // 2
    zpad = jnp.zeros((tm, MLA_QW - KV_LORA - QK_ROPE), F32)
    for h in range(N_HEADS):
        q_lat = _dot(q_nope[:, h * QK_NOPE:(h + 1) * QK_NOPE], wuk_ref[h])
        qh = jnp.concatenate([q_lat, rot1[:, h * hr:(h + 1) * hr], rot2[:, h * hr:(h + 1) * hr], zpad], axis=-1)
        if feature_major:
            qc_ref[0, h * MLA_QW:(h + 1) * MLA_QW, :] = qh.T.astype(BF16)
        else:
            qc_ref[:, h * MLA_QW:(h + 1) * MLA_QW] = qh.astype(BF16)
    ckv = _rms(ckv_ref[...], gkv_ref[...])
    if feature_major:
        rest[0][0] = ckv.T.astype(BF16)
    sm = sm_ref[...]
    x1 = sm[:, SM_KR:SM_KR + hr]
    x2 = sm[:, SM_KR + hr:SM_KR + 2 * hr]
    ang1 = pos * inv_ref[:, 0:hr]
    c1, s1 = jnp.cos(ang1), jnp.sin(ang1)
    kr = jnp.concatenate([x1 * c1 - x2 * s1, x1 * s1 + x2 * c1], axis=-1)
    st = jnp.concatenate([ckv, kr], axis=-1)
    st_ref[...] = st
    kc_ref[...] = jnp.concatenate([st, zpad], axis=-1).astype(BF16)


def _mla_proj(cq, ckv, small, inv4, gq, gkv, wn, wr1, wr2, wukT, tm, seq, pos0, feature_major=False):
    m = cq.shape[0]
    row = lambda w: pl.BlockSpec((tm, w), lambda i: (i, 0))
    fm = lambda w: pl.BlockSpec((1, w, tm), lambda i: (i, 0, 0))
    full = lambda a: pl.BlockSpec(a.shape, lambda i: (0,) * a.ndim)
    out_shape = (jax.ShapeDtypeStruct((m, N_HEADS * MLA_QW), BF16),
                 jax.ShapeDtypeStruct((m, MLA_QW), BF16),
                 jax.ShapeDtypeStruct((m, KV_LORA + QK_ROPE), F32))
    out_specs = (row(N_HEADS * MLA_QW), row(MLA_QW), row(KV_LORA + QK_ROPE))
    if feature_major:
        out_shape = (jax.ShapeDtypeStruct((m // tm, N_HEADS * MLA_QW, tm), BF16),) + out_shape[1:] \
                    + (jax.ShapeDtypeStruct((m // tm, KV_LORA, tm), BF16),)
        out_specs = (fm(N_HEADS * MLA_QW),) + out_specs[1:] + (fm(KV_LORA),)
    return pl.pallas_call(
        functools.partial(_mla_proj_kernel, tm=tm, seq=seq, pos0=pos0),
        out_shape=out_shape,
        grid=(m // tm,),
        in_specs=[row(Q_LORA), row(KV_LORA), row(LANES), full(inv4), full(gq), full(gkv),
                  full(wn), full(wr1), full(wr2), full(wukT)],
        out_specs=out_specs,
        compiler_params=_cp(("arbitrary",)),
        name="mla_proj",
    )(cq, ckv, small, inv4, gq, gkv, wn, wr1, wr2, wukT)


def _flash_init(m_ref, l_ref, acc_ref):
    m_ref[...] = jnp.full(m_ref.shape, -jnp.inf, F32)
    l_ref[...] = jnp.zeros(l_ref.shape, F32)
    acc_ref[...] = jnp.zeros(acc_ref.shape, F32)


def _units(tq):
    halves = tq // LANES
    return [(h, half) for h in range(N_HEADS) for half in range(halves)]


def _lane(half):
    return slice(half * LANES, (half + 1) * LANES)


def _flash_t(lgs, masks, v_t, m_s, l_s, acc_s):
    m_all, l_all = m_s[...], l_s[...]
    ps, alphas, new_m, new_l = [], [], [], []
    for u, (lg, mask) in enumerate(zip(lgs, masks)):
        lg = jnp.where(mask, lg, -jnp.inf)
        m_old = m_all[u:u + 1, :]
        m_new = jnp.maximum(m_old, jnp.max(lg, axis=0, keepdims=True))
        m_safe = jnp.where(m_new == -jnp.inf, 0.0, m_new)
        p = jnp.exp(lg - m_safe)
        alpha = jnp.exp(m_old - m_safe)
        new_l.append(alpha * l_all[u:u + 1, :] + jnp.sum(p, axis=0, keepdims=True))
        new_m.append(m_new)
        alphas.append(alpha)
        ps.append(p.astype(BF16))
    m_s[...] = jnp.concatenate(new_m, axis=0)
    l_s[...] = jnp.concatenate(new_l, axis=0)
    for u in range(len(lgs)):
        acc_s[u] = alphas[u] * acc_s[u] + jnp.dot(v_t, ps[u], preferred_element_type=F32)


def _flash_t_out(u, l_s, acc_s):
    return acc_s[u] / jnp.maximum(l_s[u:u + 1, :], 1e-30)


def _store_units_transposed(o_ref, outs, tq):
    halves = tq // LANES
    rows = [jnp.concatenate(outs[h * halves:(h + 1) * halves], axis=-1) for h in range(N_HEADS)]
    o_ref[...] = jnp.concatenate(rows, axis=0).T


def _nsa_prompt_kernel(qt_ref, gt_ref, kv_ref, win_ref, vt_ref, bias_ref, bcmp_ref, o_ref,
                       cmp_s, m_s, l_s, acc_s, *, seq, tq, n_top):
    tk = tq
    qb = pl.program_id(1)
    t0 = qb * tq
    nsel = seq // NSA_SEL_BLOCK
    nc = 2 * nsel
    scale = HEAD_DIM ** -0.5
    units = _units(tq)
    halves = tq // LANES

    @pl.when(qb == 0)
    def _():
        for j in range(nsel):
            r0 = j * NSA_SEL_BLOCK
            cmp_s[j:j + 1, :] = jnp.sum(kv_ref[r0:r0 + NSA_CMP_BLOCK, 0:2 * HEAD_DIM], axis=0, keepdims=True) * (1.0 / NSA_CMP_BLOCK)
            cmp_s[nsel + j:nsel + j + 1, :] = jnp.sum(kv_ref[r0 + NSA_CMP_BLOCK:r0 + 2 * NSA_CMP_BLOCK, 0:2 * HEAD_DIM],
                                                      axis=0, keepdims=True) * (1.0 / NSA_CMP_BLOCK)

    def q_unit(h, half):
        return qt_ref[h * HEAD_DIM:(h + 1) * HEAD_DIM, _lane(half)]
    t_rows = [t0 + half * LANES + _iota((1, LANES), 1) for half in range(halves)]

    kc = cmp_s[:, 0:HEAD_DIM].astype(BF16)
    eye = (_iota((HEAD_DIM, HEAD_DIM), 0) == _iota((HEAD_DIM, HEAD_DIM), 1)).astype(BF16)
    vc_t = _dot_nt(eye, cmp_s[:, HEAD_DIM:2 * HEAD_DIM]).astype(BF16)
    jj = _iota((nc, 1), 0)
    c_end = jnp.where(jj < nsel, NSA_SEL_BLOCK * jj + (NSA_CMP_BLOCK - 1), NSA_SEL_BLOCK * (jj - nsel) + (NSA_SEL_BLOCK - 1))
    lgs = [jnp.dot(kc, q_unit(h, half), preferred_element_type=F32) for h, half in units]
    o_cmp, ps_half = [], [None] * halves
    for u, (h, half) in enumerate(units):
        lg = lgs[u] * scale + bcmp_ref[h, :, _lane(half)]
        lg = jnp.where(c_end <= t_rows[half], lg, -jnp.inf)
        mx = jnp.max(lg, axis=0, keepdims=True)
        mx = jnp.where(mx == -jnp.inf, 0.0, mx)
        p = jnp.exp(lg - mx)
        p = p / jnp.maximum(jnp.sum(p, axis=0, keepdims=True), 1e-30)
        o_cmp.append(jnp.dot(vc_t, p.astype(BF16), preferred_element_type=F32))
        ps_half[half] = p if ps_half[half] is None else ps_half[half] + p

    blk = _iota((nsel, 1), 0)
    selm = []
    for half in range(halves):
        imp = ps_half[half][0:nsel] + ps_half[half][nsel:nc]
        cur = t_rows[half] // NSA_SEL_BLOCK
        imp = jnp.where((blk == cur) | (blk == 0), FORCE_SCORE, imp)
        imp = jnp.where(blk <= cur, imp, -jnp.inf)
        chosen = jnp.zeros((nsel, LANES), F32)
        for _ in range(n_top):
            top = jnp.max(imp, axis=0, keepdims=True)
            first = jnp.min(jnp.where(imp == top, blk, nsel), axis=0, keepdims=True)
            pick = blk == first
            chosen = jnp.where(pick, 1.0, chosen)
            imp = jnp.where(pick, -jnp.inf, imp)
        selm.append(chosen.astype(BF16))

    _flash_init(m_s, l_s, acc_s)

    def sel_body(c, carry):
        s0 = pl.multiple_of(c * tk, tk)
        k = kv_ref[pl.ds(s0, tk), 2 * HEAD_DIM:3 * HEAD_DIM].astype(BF16)
        kk = jnp.minimum(qb - c, 3)
        s_pos = s0 + _iota((tk, 1), 0)
        expand = ((s0 + _iota((tk, nsel), 0)) // NSA_SEL_BLOCK == _iota((tk, nsel), 1)).astype(BF16)
        lgs = [jnp.dot(k, q_unit(h, half), preferred_element_type=F32) for h, half in units]
        picked = [jnp.dot(expand, selm[half], preferred_element_type=F32) > 0.5 for half in range(halves)]
        lgs = [lgs[u] * scale + bias_ref[kk, h, :, _lane(half)] for u, (h, half) in enumerate(units)]
        masks = [picked[half] & (s_pos <= t_rows[half]) for h, half in units]
        _flash_t(lgs, masks, vt_ref[c, HEAD_DIM:2 * HEAD_DIM, :], m_s, l_s, acc_s)
        return carry

    lax.fori_loop(0, qb + 1, sel_body, 0)
    o_sel = [_flash_t_out(u, l_s, acc_s) for u in range(len(units))]

    _flash_init(m_s, l_s, acc_s)

    def win_body(c, carry):
        s0 = pl.multiple_of(c * tk, tk)
        k = win_ref[pl.ds(s0, tk), 0:HEAD_DIM].astype(BF16)
        kk = jnp.minimum(qb - c, 3)
        s_pos = s0 + _iota((tk, 1), 0)
        lgs = [jnp.dot(k, q_unit(h, half), preferred_element_type=F32) for h, half in units]
        lgs = [lgs[u] * scale + bias_ref[kk, h, :, _lane(half)] for u, (h, half) in enumerate(units)]
        masks = [(t_rows[half] - s_pos >= 0) & (t_rows[half] - s_pos <= NSA_WINDOW) for h, half in units]
        _flash_t(lgs, masks, vt_ref[c, 2 * HEAD_DIM:3 * HEAD_DIM, :], m_s, l_s, acc_s)
        return carry

    lax.fori_loop(jnp.maximum(qb - (NSA_WINDOW + tk - 1) // tk, 0), qb + 1, win_body, 0)

    g = _sigmoid(gt_ref[0:3 * N_HEADS, :])
    outs = []
    for u, (h, half) in enumerate(units):
        outs.append(g[3 * h:3 * h + 1, _lane(half)] * o_cmp[u] + g[3 * h + 1:3 * h + 2, _lane(half)] * o_sel[u]
                    + g[3 * h + 2:3 * h + 3, _lane(half)] * _flash_t_out(u, l_s, acc_s))
    _store_units_transposed(o_ref, outs, tq)


def _fm_block(w, tq, nq):
    return pl.BlockSpec((None, w, tq), lambda b, i: (b * nq + i, 0, 0))


def _nsa_prompt(nsa_qt, small_t, kv4, win, v_t, bias_t, bias_cmp, batch, seq, tq):
    nq = seq // tq
    nsel = seq // NSA_SEL_BLOCK
    nu = len(_units(tq))
    rows = lambda w: pl.BlockSpec((seq, w), lambda b, i: (b, 0))
    return pl.pallas_call(
        functools.partial(_nsa_prompt_kernel, seq=seq, tq=tq, n_top=min(NSA_TOP_N, nsel)),
        out_shape=jax.ShapeDtypeStruct((batch * seq, BRANCH_W), F32),
        grid=(batch, nq),
        in_specs=[_fm_block(BRANCH_W, tq, nq), _fm_block(FM_WIDTHS[4], tq, nq), rows(4 * HEAD_DIM), rows(2 * HEAD_DIM),
                  pl.BlockSpec((nq, FM_WIDTHS[3], tq), lambda b, i: (b, 0, 0)),
                  pl.BlockSpec((4, N_HEADS, tq, tq), lambda b, i: (0, 0, 0, 0)),
                  pl.BlockSpec((N_HEADS, 2 * nsel, tq), lambda b, i: (0, 0, i))],
        out_specs=pl.BlockSpec((tq, BRANCH_W), lambda b, i: (b * nq + i, 0)),
        scratch_shapes=[pltpu.VMEM((2 * nsel, 2 * HEAD_DIM), F32),
                        pltpu.VMEM((nu, LANES), F32), pltpu.VMEM((nu, LANES), F32),
                        pltpu.VMEM((nu, HEAD_DIM, LANES), F32)],
        compiler_params=_cp(("arbitrary", "arbitrary")),
        name="nsa_prompt",
    )(nsa_qt, small_t, kv4, win, v_t, bias_t, bias_cmp)


def _dsa_prompt_kernel(qt_ref, iqt_ref, gt_ref, smf_ref, kv_ref, vt_ref, bias_ref, o_ref,
                       key_s, m_s, l_s, acc_s, *, tq, topk):
    tk = tq
    qb = pl.program_id(1)
    t0 = qb * tq
    scale = HEAD_DIM ** -0.5
    units = _units(tq)
    halves = tq // LANES
    t_row = t0 + _iota((1, tq), 1)
    cst = IDX_DIM ** -0.5 * IDX_HEADS ** -0.5
    wi = gt_ref[3 * N_HEADS:3 * N_HEADS + IDX_HEADS, :]

    def score_body(c, carry):
        s0 = pl.multiple_of(c * tk, tk)
        ki = smf_ref[pl.ds(s0, tk), SM_IDXK:SM_IDXK + IDX_DIM].astype(BF16)
        s_pos = s0 + _iota((tk, 1), 0)
        dots = [jnp.dot(ki, iqt_ref[h * IDX_DIM:(h + 1) * IDX_DIM, _lane(half)], preferred_element_type=F32)
                for h, half in units]
        for half in range(halves):
            sc = None
            for u, (h, hf) in enumerate(units):
                if hf == half:
                    term = jnp.maximum(dots[u], 0.0) * wi[h:h + 1, _lane(half)]
                    sc = term if sc is None else sc + term
            sc = jnp.where(s_pos <= t_row[:, _lane(half)], sc * cst, -jnp.inf)
            key_s[c, :, _lane(half)] = _ordered_key(sc)
        return carry

    lax.fori_loop(0, qb + 1, score_body, 0)

    def count(pred):
        def body(c, acc):
            return acc + jnp.sum(jnp.where(pred(key_s[c]), 1, 0), axis=0, keepdims=True)
        return lax.fori_loop(0, qb + 1, body, jnp.zeros((1, tq), I32))

    thr = _kth_largest_key(lambda cand: count(lambda key: key >= cand), topk, (1, tq))
    need = (topk - count(lambda key: key > thr)).astype(F32)

    strict_lower = (_iota((tk, tk), 1) < _iota((tk, tk), 0)).astype(BF16)
    _flash_init(m_s, l_s, acc_s)

    def att_body(c, run):
        s0 = pl.multiple_of(c * tk, tk)
        key = key_s[c]
        eq = key == thr
        eqb = jnp.where(eq, 1.0, 0.0).astype(BF16)
        before = jnp.dot(strict_lower, eqb, preferred_element_type=F32) + run
        chosen = ((key > thr) | (eq & (before < need))) & (s0 + _iota((tk, 1), 0) <= t_row)
        k = kv_ref[pl.ds(s0, tk), 0:HEAD_DIM].astype(BF16)
        kk = jnp.minimum(qb - c, 3)
        lgs = [jnp.dot(k, qt_ref[h * HEAD_DIM:(h + 1) * HEAD_DIM, _lane(half)], preferred_element_type=F32)
               for h, half in units]
        lgs = [lgs[u] * scale + bias_ref[kk, h, :, _lane(half)] for u, (h, half) in enumerate(units)]
        masks = [chosen[:, _lane(half)] for h, half in units]
        _flash_t(lgs, masks, vt_ref[c, 3 * HEAD_DIM:4 * HEAD_DIM, :], m_s, l_s, acc_s)
        return run + jnp.sum(eqb.astype(F32), axis=0, keepdims=True)

    lax.fori_loop(0, qb + 1, att_body, jnp.zeros((1, tq), F32))
    _store_units_transposed(o_ref, [_flash_t_out(u, l_s, acc_s) for u in range(len(units))], tq)


def _dsa_prompt(dsa_qt, idx_qt, small_t, small, dsa_kv, v_t, bias_t, batch, seq, tq):
    nq = seq // tq
    nu = len(_units(tq))
    rows = lambda w: pl.BlockSpec((seq, w), lambda b, i: (b, 0))
    return pl.pallas_call(
        functools.partial(_dsa_prompt_kernel, tq=tq, topk=min(DSA_TOPK_MAX, seq // 4)),
        out_shape=jax.ShapeDtypeStruct((batch * seq, BRANCH_W), F32),
        grid=(batch, nq),
        in_specs=[_fm_block(BRANCH_W, tq, nq), _fm_block(IDX_HEADS * IDX_DIM, tq, nq), _fm_block(FM_WIDTHS[4], tq, nq),
                  rows(LANES), rows(2 * HEAD_DIM),
                  pl.BlockSpec((nq, FM_WIDTHS[3], tq), lambda b, i: (b, 0, 0)),
                  pl.BlockSpec((4, N_HEADS, tq, tq), lambda b, i: (0, 1, 0, 0))],
        out_specs=pl.BlockSpec((tq, BRANCH_W), lambda b, i: (b * nq + i, 0)),
        scratch_shapes=[pltpu.VMEM((nq, tq, tq), I32),
                        pltpu.VMEM((nu, LANES), F32), pltpu.VMEM((nu, LANES), F32),
                        pltpu.VMEM((nu, HEAD_DIM, LANES), F32)],
        compiler_params=_cp(("arbitrary", "arbitrary")),
        name="dsa_prompt",
    )(dsa_qt, idx_qt, small_t, small, dsa_kv, v_t, bias_t)


def _mla_prompt_kernel(qt_ref, kc_ref, vt_ref, wuvt_ref, o_ref, m_s, l_s, acc_s, *, tq):
    tk = tq
    qb = pl.program_id(1)
    scale = MLA_HEAD_QK ** -0.5
    units = _units(tq)
    t_rows = [qb * tq + half * LANES + _iota((1, LANES), 1) for half in range(tq // LANES)]
    _flash_init(m_s, l_s, acc_s)

    def body(c, carry):
        s0 = pl.multiple_of(c * tk, tk)
        kc = kc_ref[pl.ds(s0, tk), :]
        s_pos = s0 + _iota((tk, 1), 0)
        lgs = [jnp.dot(kc, qt_ref[h * MLA_QW:(h + 1) * MLA_QW, _lane(half)], preferred_element_type=F32) * scale
               for h, half in units]
        masks = [s_pos <= t_rows[half] for h, half in units]
        _flash_t(lgs, masks, vt_ref[c], m_s, l_s, acc_s)
        return carry

    lax.fori_loop(0, qb + 1, body, 0)
    outs = [jnp.dot(wuvt_ref[h], _flash_t_out(u, l_s, acc_s).astype(BF16), preferred_element_type=F32)
            for u, (h, half) in enumerate(units)]
    _store_units_transposed(o_ref, outs, tq)


def _mla_prompt(qc_t, kc, v_t, wuv_t, batch, seq, tq):
    nq = seq // tq
    nu = len(_units(tq))
    return pl.pallas_call(
        functools.partial(_mla_prompt_kernel, tq=tq),
        out_shape=jax.ShapeDtypeStruct((batch * seq, BRANCH_W), F32),
        grid=(batch, nq),
        in_specs=[_fm_block(N_HEADS * MLA_QW, tq, nq),
                  pl.BlockSpec((seq, MLA_QW), lambda b, i: (b, 0)),
                  pl.BlockSpec((nq, KV_LORA, tq), lambda b, i: (b, 0, 0)),
                  pl.BlockSpec(wuv_t.shape, lambda b, i: (0, 0, 0))],
        out_specs=pl.BlockSpec((tq, BRANCH_W), lambda b, i: (b * nq + i, 0)),
        scratch_shapes=[pltpu.VMEM((nu, LANES), F32), pltpu.VMEM((nu, LANES), F32),
                        pltpu.VMEM((nu, KV_LORA, LANES), F32)],
        compiler_params=_cp(("arbitrary", "arbitrary")),
        name="mla_prompt",
    )(qc_t, kc, v_t, wuv_t)


def _hgrn_gates(hg, lb):
    w = BRANCH_W
    q, fl, iv, gg = hg[:, 0:w], hg[:, w:2 * w], hg[:, 2 * w:3 * w], hg[:, 3 * w:4 * w]
    f = lb + (1.0 - lb) * _sigmoid(fl)
    return _silu(q), f, 1.0 - f, iv, gg


def _hgrn_finish(o, gg, hgn, ones_bf):
    ms = _dot_hilo(o * o, ones_bf) * (1.0 / HEAD_DIM)
    return o * lax.rsqrt(ms + EPS) * hgn * _silu(gg)


def _hgrn_prompt_kernel(hg_ref, lb_ref, hgn_ref, o_ref, s_ref, st_s, q_s, b_s, k_s, v_s, o_s, w_s, *, tc):
    C = HGRN_CHUNK
    w = BRANCH_W
    i = pl.program_id(1)

    @pl.when(i == 0)
    def _():
        st_s[...] = jnp.zeros(st_s.shape, F32)

    qf, f, k, iv, gg = _hgrn_gates(hg_ref[...], lb_ref[...])
    b = jnp.log(jnp.maximum(f, 1e-20))
    row = _iota((tc, 1), 0) % C
    for s in (1, 2, 4, 8):
        b = b + jnp.where(row >= s, pltpu.roll(b, s, 0), 0.0)
    q_s[...] = qf
    b_s[...] = b
    k_s[...] = k
    v_s[...] = iv
    same_head = _head_block_ones(w)
    ones_bf = same_head.astype(BF16)
    group = (_iota((C, C * C), 0) == _iota((C, C * C), 1) // C).astype(BF16)
    s_idx = _iota((C, 1), 0)

    def chunk(ci, carry):
        r0 = pl.multiple_of(ci * C, C)
        qc, bc, kc, vc = q_s[pl.ds(r0, C), :], b_s[pl.ds(r0, C), :], k_s[pl.ds(r0, C), :], v_s[pl.ds(r0, C), :]
        for tt in range(C):
            dec = jnp.exp(jnp.where(s_idx <= tt, bc[tt:tt + 1, :] - bc, -jnp.inf))
            w_s[tt * C:(tt + 1) * C, :] = dec * qc[tt:tt + 1, :] * kc
        a_rep = _dot_hilo(w_s[...], ones_bf)
        prod = (a_rep.reshape(C, C, w) * vc[None]).reshape(C * C, w)
        hi = prod.astype(BF16)
        lo = (prod - hi.astype(F32)).astype(BF16)
        o_intra = jnp.dot(group, hi, preferred_element_type=F32) + jnp.dot(group, lo, preferred_element_type=F32)
        st = st_s[...]
        o_s[pl.ds(r0, C), :] = o_intra + _dot_nt(qc * jnp.exp(bc), st)
        bl = bc[C - 1:C, :]
        upd = _dot_tn(vc, kc * jnp.exp(bl - bc))
        st_s[...] = st * jnp.exp(bl) + jnp.where(same_head, upd, 0.0)
        return carry

    lax.fori_loop(0, tc // C, chunk, 0)
    o_ref[...] = _hgrn_finish(o_s[...], gg, hgn_ref[...], ones_bf)

    @pl.when(i == pl.num_programs(1) - 1)
    def _():
        for h in range(N_HEADS):
            s_ref[0, h] = st_s[h * HEAD_DIM:(h + 1) * HEAD_DIM, h * HEAD_DIM:(h + 1) * HEAD_DIM].T


def _hgrn_prompt(hg, lower, hgn, batch, seq, tc):
    nt = seq // tc
    w = BRANCH_W
    return pl.pallas_call(
        functools.partial(_hgrn_prompt_kernel, tc=tc),
        out_shape=(jax.ShapeDtypeStruct((batch * seq, w), F32),
                   jax.ShapeDtypeStruct((batch, N_HEADS, HEAD_DIM, HEAD_DIM), F32)),
        grid=(batch, nt),
        in_specs=[pl.BlockSpec((tc, 4 * w), lambda b, i: (b * nt + i, 0)),
                  pl.BlockSpec((1, w), lambda b, i: (0, 0)),
                  pl.BlockSpec((1, w), lambda b, i: (0, 0))],
        out_specs=(pl.BlockSpec((tc, w), lambda b, i: (b * nt + i, 0)),
                   pl.BlockSpec((1, N_HEADS, HEAD_DIM, HEAD_DIM), lambda b, i: (b, 0, 0, 0))),
        scratch_shapes=[pltpu.VMEM((w, w), F32)] + [pltpu.VMEM((tc, w), F32)] * 5
                       + [pltpu.VMEM((HGRN_CHUNK * HGRN_CHUNK, w), F32)],
        compiler_params=_cp(("arbitrary", "arbitrary")),
        name="hgrn_prompt",
    )(hg, lower, hgn)


def _hgrn_step_kernel(hg_ref, lb_ref, hgn_ref, s0_ref, o_ref, s_ref, o_s, *, bt):
    qf, f, k, iv, gg = _hgrn_gates(hg_ref[...], lb_ref[...])
    fT = jnp.maximum(f, 1e-20).T
    kT = k.T
    qT = qf.T
    for bi in range(bt):
        for h in range(N_HEADS):
            hs = slice(h * HEAD_DIM, (h + 1) * HEAD_DIM)
            s_new = fT[hs, bi:bi + 1] * s0_ref[bi, h] + kT[hs, bi:bi + 1] * iv[bi:bi + 1, hs]
            s_ref[bi, h] = s_new
            o_s[bi:bi + 1, hs] = jnp.sum(qT[hs, bi:bi + 1] * s_new, axis=0, keepdims=True)
    o_ref[...] = _hgrn_finish(o_s[...], gg, hgn_ref[...], _head_block_ones(BRANCH_W).astype(BF16))


def _hgrn_step(hg, lower, hgn, s0, bt):
    m = hg.shape[0]
    w = BRANCH_W
    sblk = pl.BlockSpec((bt, N_HEADS, HEAD_DIM, HEAD_DIM), lambda i: (i, 0, 0, 0))
    return pl.pallas_call(
        functools.partial(_hgrn_step_kernel, bt=bt),
        out_shape=(jax.ShapeDtypeStruct((m, w), F32), jax.ShapeDtypeStruct(s0.shape, F32)),
        grid=(m // bt,),
        in_specs=[pl.BlockSpec((bt, 4 * w), lambda i: (i, 0)),
                  pl.BlockSpec((1, w), lambda i: (0, 0)), pl.BlockSpec((1, w), lambda i: (0, 0)), sblk],
        out_specs=(pl.BlockSpec((bt, w), lambda i: (i, 0)), sblk),
        scratch_shapes=[pltpu.VMEM((bt, w), F32)],
        compiler_params=_cp(("arbitrary",)),
        name="hgrn_step",
    )(hg, lower, hgn, s0)


def _merge_kernel(x_ref, o0_ref, o1_ref, o2_ref, o3_ref, mg_ref, gate_ref, gpost_ref, wb_ref, wo_ref, y_ref):
    d = x_ref.shape[1]
    mixed = None
    for n, o_ref in enumerate((o0_ref, o1_ref, o2_ref, o3_ref)):
        term = _sigmoid(mg_ref[:, n * d:(n + 1) * d]) * _dot(o_ref[...], wb_ref[n])
        mixed = term if mixed is None else mixed + term
    y = _dot(mixed, wo_ref[...])
    y_ref[...] = x_ref[...] + gate_ref[0] * _rms(y, gpost_ref[...])


def _merge(x, branches, merge_g, gate, gpost, wb, wo, tm, tps):
    m, d = x.shape
    row = lambda w: pl.BlockSpec((tm, w), lambda i: (i, 0))
    return pl.pallas_call(
        _merge_kernel,
        out_shape=jax.ShapeDtypeStruct((m, d), F32),
        grid=(m // tm,),
        in_specs=[row(d)] + [row(BRANCH_W)] * 4 + [row(N_BRANCH * d), _mod_spec(gate, tm, tps),
                  pl.BlockSpec((1, d), lambda i: (0, 0)),
                  pl.BlockSpec(wb.shape, lambda i: (0, 0, 0), pipeline_mode=pl.Buffered(1)),
                  pl.BlockSpec(wo.shape, lambda i: (0, 0), pipeline_mode=pl.Buffered(1))],
        out_specs=row(d),
        compiler_params=_cp(("arbitrary",)),
        name="merge",
    )(x, *branches, merge_g, gate, gpost, wb, wo)


def _gelu_tanh(x):
    return 0.5 * x * (1.0 + jnp.tanh(math.sqrt(2.0 / math.pi) * (x + 0.044715 * (x * x * x))))


def _ffn_kernel(*refs, tm, tps, nff, stepwise):
    if stepwise:
        (x_ref, gpre_ref, sh_ref, sc_ref, gate_ref, gpost_ref, wg_ref, wv_ref, cwg_ref, cwv_ref, cbg_ref, cbv_ref,
         wd_ref, p0g_ref, p0v_ref, p1g_ref, p1v_ref, y_ref, ug_ref, uv_ref, h_s, acc_s) = refs
    else:
        (x_ref, gpre_ref, sh_ref, sc_ref, gate_ref, gpost_ref, wg_ref, wv_ref, cwg_ref, cwv_ref, cbg_ref, cbv_ref,
         wd_ref, y_ref, csg_ref, csv_ref, h_s, acc_s, carry_g, carry_v) = refs
    i = pl.program_id(0)
    j = pl.program_id(1)

    @pl.when(j == 0)
    def _():
        h = _rms(x_ref[...], gpre_ref[...]) * (1.0 + sc_ref[0]) + sh_ref[0]
        h_s[...] = h.astype(BF16)
        acc_s[...] = jnp.zeros(acc_s.shape, F32)

    hb = h_s[...]
    ug = jnp.dot(hb, wg_ref[...], preferred_element_type=F32)
    uv = jnp.dot(hb, wv_ref[...], preferred_element_type=F32)

    if stepwise:
        def conv(u, cw_ref, cb_ref, p0_ref, p1_ref):
            return cb_ref[...] + p0_ref[...] * cw_ref[0:1, :] + p1_ref[...] * cw_ref[1:2, :] + u * cw_ref[2:3, :]
        cg = conv(ug, cwg_ref, cbg_ref, p0g_ref, p1g_ref)
        cv = conv(uv, cwv_ref, cbv_ref, p0v_ref, p1v_ref)
        ug_ref[...] = ug
        uv_ref[...] = uv
    else:
        first = (i % tps) == 0
        row = _iota((tm, 1), 0)

        @pl.when(i == 0)
        def _():
            carry_g[j] = jnp.zeros(carry_g.shape[1:], F32)
            carry_v[j] = jnp.zeros(carry_v.shape[1:], F32)

        def conv(u, cw_ref, cb_ref, carry):
            prev = jnp.where(first, 0.0, carry[j])
            um1 = jnp.where(row == 0, prev[1:2, :], pltpu.roll(u, 1, 0))
            um2 = jnp.where(row == 0, prev[0:1, :], jnp.where(row == 1, prev[1:2, :], pltpu.roll(u, 2, 0)))
            return cb_ref[...] + um2 * cw_ref[0:1, :] + um1 * cw_ref[1:2, :] + u * cw_ref[2:3, :]
        cg = conv(ug, cwg_ref, cbg_ref, carry_g)
        cv = conv(uv, cwv_ref, cbv_ref, carry_v)
        carry_g[j] = ug[tm - 2:tm, :]
        carry_v[j] = uv[tm - 2:tm, :]
        csg_ref[0, j] = ug[tm - 2:tm, :]
        csv_ref[0, j] = uv[tm - 2:tm, :]

    acc_s[...] += _dot(_gelu_tanh(cg) * cv, wd_ref[...])

    @pl.when(j == nff - 1)
    def _():
        y_ref[...] = x_ref[...] + gate_ref[0] * _rms(acc_s[...], gpost_ref[...])


def _ffn(x, gpre, shift, scale, gate, gpost, w_up, conv_w, conv_b, w_down, tm, tps, prev=None):
    m, d = x.shape
    dff = w_down.shape[0]
    nff = 2 if dff % (2 * LANES) == 0 else 1
    fc = dff // nff
    stepwise = prev is not None
    vec = lambda: pl.BlockSpec((1, d), lambda i, j: (0, 0))
    colg = lambda r: pl.BlockSpec((r, fc), lambda i, j: (0, j))
    colv = lambda r: pl.BlockSpec((r, fc), lambda i, j: (0, nff + j))
    in_specs = [pl.BlockSpec((tm, d), lambda i, j: (i, 0)), vec(),
                _mod_spec(shift, tm, tps), _mod_spec(scale, tm, tps), _mod_spec(gate, tm, tps), vec(),
                colg(d), colv(d), colg(CONV_W), colv(CONV_W), colg(1), colv(1),
                pl.BlockSpec((fc, d), lambda i, j: (j, 0))]
    args = [x, gpre, shift, scale, gate, gpost, w_up, w_up, conv_w, conv_w, conv_b, conv_b, w_down]
    scratch = [pltpu.VMEM((tm, d), BF16), pltpu.VMEM((tm, d), F32)]
    if stepwise:
        p0, p1 = prev
        in_specs += [pl.BlockSpec((tm, fc), lambda i, j: (i, j)), pl.BlockSpec((tm, fc), lambda i, j: (i, nff + j))] * 2
        args += [p0, p0, p1, p1]
        out_shape = (jax.ShapeDtypeStruct((m, d), F32), jax.ShapeDtypeStruct((m, dff), F32), jax.ShapeDtypeStruct((m, dff), F32))
        out_specs = (pl.BlockSpec((tm, d), lambda i, j: (i, 0)),
                     pl.BlockSpec((tm, fc), lambda i, j: (i, j)), pl.BlockSpec((tm, fc), lambda i, j: (i, j)))
    else:
        nseq = m // (tm * tps)
        out_shape = (jax.ShapeDtypeStruct((m, d), F32),
                     jax.ShapeDtypeStruct((nseq, nff, CONV_W - 1, fc), F32),
                     jax.ShapeDtypeStruct((nseq, nff, CONV_W - 1, fc), F32))
        cs = pl.BlockSpec((1, nff, CONV_W - 1, fc), lambda i, j: (i // tps, 0, 0, 0))
        out_specs = (pl.BlockSpec((tm, d), lambda i, j: (i, 0)), cs, cs)
        scratch += [pltpu.VMEM((nff, CONV_W - 1, fc), F32)] * 2
    return pl.pallas_call(
        functools.partial(_ffn_kernel, tm=tm, tps=tps, nff=nff, stepwise=stepwise),
        out_shape=out_shape,
        grid=(m // tm, nff),
        in_specs=in_specs,
        out_specs=out_specs,
        scratch_shapes=scratch,
        compiler_params=_cp(("arbitrary", "arbitrary")),
        name="ffn_step" if stepwise else "ffn_seq",
    )(*args)


def _fetch_pages(pt_ref, pool_ref, buf, sem, layer, pg, rows, rowblk):
    nb, nj = pt_ref.shape[0], pt_ref.shape[1] // pg
    total = nb * nj
    step = pl.program_id(0) * nj + pl.program_id(1)
    ahead = PAGE_SLOTS - 1

    def copies(s, sl):
        bb, jj = s // nj, s % nj
        return [pltpu.make_async_copy(pool_ref.at[layer, pt_ref[bb, jj * pg + k], pl.ds(rowblk * rows, rows), :],
                                      buf.at[sl, k], sem.at[sl]) for k in range(pg)]

    @pl.when(step == 0)
    def _():
        for s in range(min(ahead, total)):
            for c in copies(s, s % PAGE_SLOTS):
                c.start()

    @pl.when(step + ahead < total)
    def _():
        for c in copies(step + ahead, (step + ahead) % PAGE_SLOTS):
            c.start()

    slot = step % PAGE_SLOTS
    for c in copies(step, slot):
        c.wait()
    return slot


def _page_scratch(pg, rows):
    return [pltpu.VMEM((PAGE_SLOTS, pg, rows, PAGE_SIZE), F32), pltpu.SemaphoreType.DMA((PAGE_SLOTS,))]


def _softmax_with_self(lg, valid, lg_self):
    lg = jnp.where(valid, lg, -jnp.inf)
    m = jnp.maximum(jnp.max(lg, axis=-1, keepdims=True), lg_self)
    p = jnp.exp(lg - m)
    p_self = jnp.exp(lg_self - m)
    den = jnp.sum(p, axis=-1, keepdims=True) + p_self
    return p / den, p_self / den


def _self_logit(q8, k_row):
    qf = q8.astype(BF16).astype(F32)
    kf = k_row.astype(BF16).astype(F32)
    return jnp.sum(qf * kf, axis=-1, keepdims=True)


def _rows8(x, w):
    return jnp.concatenate([x[:, h * w:(h + 1) * w] for h in range(N_HEADS)] + [jnp.zeros((8 - N_HEADS, w), x.dtype)], axis=0)


def _col8(x):
    r = _iota((8, 1), 0)
    out = jnp.zeros((8, 1), F32)
    for h in range(N_HEADS):
        out = jnp.where(r == h, x[:, h:h + 1], out)
    return out


def _bias8(d, tab_ref, head0):
    bs = _rel_bias(d, tab_ref, range(head0, head0 + N_HEADS))
    r = _iota((8, d.shape[1]), 0)
    out = jnp.zeros((8, d.shape[1]), F32)
    for h in range(N_HEADS):
        out = jnp.where(r == h, bs[h], out)
    return out


def _softmax_rows(lg, valid):
    lg = jnp.where(valid, lg, -jnp.inf)
    mx = jnp.max(lg, axis=-1, keepdims=True)
    mx = jnp.where(mx == -jnp.inf, 0.0, mx)
    p = jnp.exp(lg - mx)
    return p / jnp.maximum(jnp.sum(p, axis=-1, keepdims=True), 1e-30)


def _write_heads(o_ref, o8, w):
    for h in range(N_HEADS):
        o_ref[0, :, h * w:(h + 1) * w] = o8[h:h + 1, :]


def _nsa_cmp_step_kernel(pt_ref, tab_ref, q_ref, pool_ref, o_ref, idx_ref, cmp_s, buf, sem, *, layer, pg, past):
    slot = _fetch_pages(pt_ref, pool_ref, buf, sem, layer, pg, 2 * HEAD_DIM, 0)
    pages = [buf.at[slot, k] for k in range(pg)]
    j = pl.program_id(1)
    nsel = past // NSA_SEL_BLOCK
    per_page = PAGE_SIZE // NSA_CMP_BLOCK
    nb = per_page * pg
    half = nb // 2
    nsteps = cmp_s.shape[0]
    key = _iota((PAGE_SIZE, nb), 0)
    col = _iota((PAGE_SIZE, nb), 1)
    acc = jnp.zeros((2 * HEAD_DIM, nb), F32)
    for k in range(pg):
        g = per_page * k + key // NSA_CMP_BLOCK
        pool = jnp.where(col == (g % 2) * half + g // 2, 1.0 / NSA_CMP_BLOCK, 0.0).astype(BF16)
        acc = acc + jnp.dot(pages[k][...].astype(BF16), pool, preferred_element_type=F32)
    cmp_s[j] = acc

    @pl.when(j == pl.num_programs(1) - 1)
    def _():
        q8 = _rows8(q_ref[0], HEAD_DIM)
        cc = _iota((1, nb), 1)
        ps_all, o_cmp = [], None
        lgs = []
        for s in range(nsteps):
            sel = s * half + jnp.where(cc < half, cc, cc - half)
            c_end = NSA_SEL_BLOCK * sel + jnp.where(cc < half, NSA_CMP_BLOCK - 1, NSA_SEL_BLOCK - 1)
            lgs.append(_dot(q8, cmp_s[s, 0:HEAD_DIM, :]) * HEAD_DIM ** -0.5 + _bias8(past - c_end, tab_ref, 0))
        lg = jnp.concatenate(lgs, axis=-1)
        p = _softmax_rows(lg, True)
        for s in range(nsteps):
            term = _dot_nt(p[:, s * nb:(s + 1) * nb], cmp_s[s, HEAD_DIM:2 * HEAD_DIM, :])
            o_cmp = term if o_cmp is None else o_cmp + term
            ps = p[0:1, s * nb:(s + 1) * nb] + p[1:2, s * nb:(s + 1) * nb] + p[2:3, s * nb:(s + 1) * nb] + p[3:4, s * nb:(s + 1) * nb]
            ps_all.append(ps[:, 0:half] + ps[:, half:nb])
        _write_heads(o_ref, o_cmp, HEAD_DIM)
        idx_ref[0] = jnp.concatenate(ps_all, axis=-1)


def _nsa_pick_kernel(imp_ref, idx_ref, *, n_pick):
    imp = imp_ref[...]
    bsz, nsel = imp.shape
    blk = _iota((1, nsel), 1)
    imp = jnp.where(blk == 0, -jnp.inf, imp)
    lane = _iota((1, LANES), 1)
    idx = jnp.where(lane == n_pick + 1, nsel, jnp.zeros((bsz, LANES), I32))
    for s in range(n_pick):
        top = jnp.max(imp, axis=-1, keepdims=True)
        first = jnp.min(jnp.where(imp == top, blk, nsel), axis=-1, keepdims=True)
        idx = jnp.where(lane == s + 1, first, idx)
        imp = jnp.where(blk == first, -jnp.inf, imp)
    idx_ref[...] = idx


def _nsa_pick(imp, n_pick):
    bsz = imp.shape[0]
    return pl.pallas_call(
        functools.partial(_nsa_pick_kernel, n_pick=n_pick),
        out_shape=jax.ShapeDtypeStruct((bsz, LANES), I32),
        in_specs=[pl.BlockSpec(memory_space=pltpu.VMEM)],
        out_specs=pl.BlockSpec(memory_space=pltpu.VMEM),
        name="nsa_pick",
    )(imp)


def _nsa_cmp_step(page_table, rel_table, q3, cache_t, layer, pg, past):
    bsz, n_pages = page_table.shape
    nsel = past // NSA_SEL_BLOCK
    n_pick = min(NSA_TOP_N, nsel + 1) - 2
    nb = pg * (PAGE_SIZE // NSA_CMP_BLOCK)
    grid_spec = pltpu.PrefetchScalarGridSpec(
        num_scalar_prefetch=1,
        grid=(bsz, n_pages // pg),
        in_specs=[pl.BlockSpec(memory_space=pltpu.SMEM),
                  pl.BlockSpec((1, 1, BRANCH_W), lambda b, j, pt: (b, 0, 0)),
                  pl.BlockSpec(memory_space=pl.ANY)],
        out_specs=(pl.BlockSpec((1, 1, BRANCH_W), lambda b, j, pt: (b, 0, 0)),
                   pl.BlockSpec((1, 1, nsel), lambda b, j, pt: (b, 0, 0))),
        scratch_shapes=[pltpu.VMEM((n_pages // pg, 2 * HEAD_DIM, nb), F32)] + _page_scratch(pg, 2 * HEAD_DIM))
    o_cmp, imp = pl.pallas_call(
        functools.partial(_nsa_cmp_step_kernel, layer=layer, pg=pg, past=past),
        out_shape=(jax.ShapeDtypeStruct((bsz, 1, BRANCH_W), F32), jax.ShapeDtypeStruct((bsz, 1, nsel), F32)),
        grid_spec=grid_spec,
        compiler_params=_cp(("arbitrary", "arbitrary")),
        name="nsa_cmp_step",
    )(page_table, rel_table, q3, cache_t)
    return o_cmp, _nsa_pick(imp.reshape(bsz, nsel), n_pick)


def _nsa_sel_step_kernel(idx_ref, pt_ref, tab_ref, q_ref, sm_ref, kv_ref, nw_ref, ocmp_ref, win_ref, *rest,
                         n_past, past):
    blocks, (o_ref,) = rest[:n_past], rest[n_past:]
    b = pl.program_id(0)
    sb = NSA_SEL_BLOCK
    per_page = PAGE_SIZE // sb
    scale = HEAD_DIM ** -0.5
    q8 = _rows8(q_ref[0], HEAD_DIM)
    zero_d = jnp.zeros((1, 1), I32)

    row = _iota((1, PAGE_SIZE), 1)
    lgs, valids = [], []
    for k in range(n_past):
        blk = idx_ref[b, k]
        pos = (blk // per_page) * PAGE_SIZE + row
        lgs.append(_dot(q8, blocks[k][0:HEAD_DIM, :]) * scale + _bias8(past - pos, tab_ref, 0))
        valids.append(row // sb == blk % per_page)
    new_kv = kv_ref[0]
    lg_self = _self_logit(q8, new_kv[:, 2 * HEAD_DIM:3 * HEAD_DIM]) * scale + _bias8(zero_d, tab_ref, 0)
    p, p_self = _softmax_with_self(jnp.concatenate(lgs, axis=-1), jnp.concatenate(valids, axis=-1), lg_self)
    o_sel = p_self * new_kv[:, 3 * HEAD_DIM:4 * HEAD_DIM]
    for k in range(n_past):
        o_sel = o_sel + _dot_nt(p[:, k * PAGE_SIZE:(k + 1) * PAGE_SIZE], blocks[k][HEAD_DIM:2 * HEAD_DIM, :])

    wb = win_ref.shape[1]
    d = wb - _iota((1, wb), 1)
    lg = _dot(q8, win_ref[0:HEAD_DIM, :]) * scale + _bias8(d, tab_ref, 0)
    new_win = nw_ref[0]
    lg_self = _self_logit(q8, new_win[:, 0:HEAD_DIM]) * scale + _bias8(zero_d, tab_ref, 0)
    p, p_self = _softmax_with_self(lg, (d <= NSA_WINDOW) & (past - d >= 0), lg_self)
    o_win = _dot_nt(p, win_ref[HEAD_DIM:2 * HEAD_DIM, :]) + p_self * new_win[:, HEAD_DIM:2 * HEAD_DIM]

    g = _sigmoid(sm_ref[0][:, SM_NSAG:SM_NSAG + 3 * N_HEADS])
    for h in range(N_HEADS):
        hs = slice(h * HEAD_DIM, (h + 1) * HEAD_DIM)
        o_ref[0, :, hs] = (g[:, 3 * h:3 * h + 1] * ocmp_ref[0][:, hs] + g[:, 3 * h + 1:3 * h + 2] * o_sel[h:h + 1, :]
                           + g[:, 3 * h + 2:3 * h + 3] * o_win[h:h + 1, :])


def _nsa_sel_step(idx, page_table, rel_table, q3, sm3, kv3, nw3, ocmp3, win_t, cache_t, layer, past):
    bsz = page_table.shape[0]
    nsel = past // NSA_SEL_BLOCK
    n_past = min(NSA_TOP_N, nsel + 1) - 1
    per_page = PAGE_SIZE // NSA_SEL_BLOCK
    wb = win_t.shape[3]
    tok = lambda w: pl.BlockSpec((1, 1, w), lambda b, idx, pt: (b, 0, 0))

    def blk_spec(k):
        return pl.BlockSpec((None, None, 2 * HEAD_DIM, PAGE_SIZE),
                            lambda b, idx, pt: (layer, pt[b, idx[b, k] // per_page], 1, 0))
    grid_spec = pltpu.PrefetchScalarGridSpec(
        num_scalar_prefetch=2,
        grid=(bsz,),
        in_specs=[pl.BlockSpec(memory_space=pltpu.SMEM), tok(BRANCH_W), tok(LANES), tok(4 * HEAD_DIM), tok(2 * HEAD_DIM),
                  tok(BRANCH_W),
                  pl.BlockSpec((None, None, 2 * HEAD_DIM, wb), lambda b, idx, pt: (layer, b, 0, 0))]
                 + [blk_spec(k) for k in range(n_past)],
        out_specs=tok(BRANCH_W))
    return pl.pallas_call(
        functools.partial(_nsa_sel_step_kernel, n_past=n_past, past=past),
        out_shape=jax.ShapeDtypeStruct((bsz, 1, BRANCH_W), F32),
        grid_spec=grid_spec,
        compiler_params=_cp(("arbitrary",)),
        name="nsa_sel_step",
    )(idx, page_table, rel_table, q3, sm3, kv3, nw3, ocmp3, win_t, *([cache_t] * n_past))


def _index_weights(sm):
    return _col8(sm[:, SM_IDXW:SM_IDXW + IDX_HEADS])


def _dsa_score_step_kernel(pt_ref, iq_ref, sm_ref, pool_ref, sc_ref, new_ref, buf, sem, *, layer, pg):
    slot = _fetch_pages(pt_ref, pool_ref, buf, sem, layer, pg, IDX_DIM, 0)
    pages = [buf.at[slot, k] for k in range(pg)]
    qi8 = _rows8(iq_ref[0], IDX_DIM)
    sm = sm_ref[0]
    wcol = _index_weights(sm)
    cst = IDX_DIM ** -0.5 * IDX_HEADS ** -0.5
    rows = []
    for k in range(pg):
        s = jnp.maximum(_dot(qi8, pages[k][...]), 0.0)
        rows.append(jnp.sum(s * wcol, axis=0, keepdims=True) * cst)
    sc_ref[0] = jnp.concatenate(rows, axis=0)
    s_new = jnp.maximum(_self_logit(qi8, sm[:, SM_IDXK:SM_IDXK + IDX_DIM]), 0.0)
    sc_new = jnp.sum(s_new * wcol, axis=0, keepdims=True) * cst
    new_ref[0] = jnp.where(_iota((1, LANES), 1) == 0, sc_new, -jnp.inf)


def _dsa_score_step(page_table, iq3, sm3, cache_idx_t, layer, pg):
    bsz, n_pages = page_table.shape
    tok = lambda w: pl.BlockSpec((1, 1, w), lambda b, j, pt: (b, 0, 0))
    grid_spec = pltpu.PrefetchScalarGridSpec(
        num_scalar_prefetch=1,
        grid=(bsz, n_pages // pg),
        in_specs=[tok(IDX_HEADS * IDX_DIM), tok(LANES), pl.BlockSpec(memory_space=pl.ANY)],
        out_specs=(pl.BlockSpec((1, pg, PAGE_SIZE), lambda b, j, pt: (b, j, 0)), tok(LANES)),
        scratch_shapes=_page_scratch(pg, IDX_DIM))
    return pl.pallas_call(
        functools.partial(_dsa_score_step_kernel, layer=layer, pg=pg),
        out_shape=(jax.ShapeDtypeStruct((bsz, n_pages, PAGE_SIZE), F32), jax.ShapeDtypeStruct((bsz, 1, LANES), F32)),
        grid_spec=grid_spec,
        compiler_params=_cp(("arbitrary", "arbitrary")),
        name="dsa_score_step",
    )(page_table, iq3, sm3, cache_idx_t)


def _dsa_thr_step_kernel(sc_ref, new_ref, thr_ref, need_ref, tie_ref, *, topk):
    key = _ordered_key(sc_ref[...])
    key_new = _ordered_key(new_ref[...])
    bt = key.shape[0]

    def count(pred):
        return (jnp.sum(jnp.where(pred(key), 1, 0), axis=-1, keepdims=True)
                + jnp.sum(jnp.where(pred(key_new), 1, 0), axis=-1, keepdims=True))

    thr = _kth_largest_key(lambda cand: count(lambda x: x >= cand), topk, (bt, 1))
    need = topk - count(lambda x: x > thr)
    thr_ref[...] = jnp.broadcast_to(thr, thr_ref.shape)
    need_ref[...] = jnp.broadcast_to(need, need_ref.shape)
    tie_ref[...] = jnp.broadcast_to(jnp.where(count(lambda x: x == thr) > need, 1, 0), tie_ref.shape)


def _dsa_thr_step(scores, new, topk, bt):
    bsz, p = scores.shape
    out = pl.BlockSpec((bt, LANES), lambda i: (i, 0))
    return pl.pallas_call(
        functools.partial(_dsa_thr_step_kernel, topk=topk),
        out_shape=(jax.ShapeDtypeStruct((bsz, LANES), I32),) * 3,
        grid=(bsz // bt,),
        in_specs=[pl.BlockSpec((bt, p), lambda i: (i, 0)), pl.BlockSpec((bt, LANES), lambda i: (i, 0))],
        out_specs=(out, out, out),
        compiler_params=_cp(("arbitrary",)),
        name="dsa_thr_step",
    )(scores, new)


def _online_self(lg_self, v_row, m_s, l_s, acc_s):
    m_old = m_s[...]
    m_new = jnp.maximum(m_old, lg_self)
    m_safe = jnp.where(m_new == -jnp.inf, 0.0, m_new)
    alpha = jnp.exp(m_old - m_safe)
    p_self = jnp.exp(lg_self - m_safe)
    den = alpha * l_s[...] + p_self
    return (alpha * acc_s[...] + p_self * v_row) / jnp.maximum(den, 1e-30)


def _dsa_att_step_kernel(pt_ref, tie_ref, tab_ref, q_ref, kv_ref, sc_ref, new_ref, thr_ref, need_ref, pool_ref,
                         o_ref, m_s, l_s, acc_s, run_s, buf, sem, *, layer, pg, past):
    slot = _fetch_pages(pt_ref, pool_ref, buf, sem, layer, pg, 2 * HEAD_DIM, 0)
    pages = [buf.at[slot, k] for k in range(pg)]
    b = pl.program_id(0)
    j = pl.program_id(1)
    scale = HEAD_DIM ** -0.5
    nk = pg * PAGE_SIZE

    @pl.when(j == 0)
    def _():
        _flash_init(m_s, l_s, acc_s)
        run_s[...] = jnp.zeros(run_s.shape, F32)

    q8 = _rows8(q_ref[0], HEAD_DIM)
    thr = thr_ref[0][:, 0:1]
    need = need_ref[0][:, 0:1].astype(F32)
    has_ties = tie_ref[b] > 0
    key = _ordered_key(sc_ref[0])

    def pick_plain(run):
        return jnp.where(key >= thr, 1.0, 0.0), run

    def pick_ties(run):
        eq = key == thr
        eqf = jnp.where(eq, 1.0, 0.0)
        strict_upper = (_iota((PAGE_SIZE, PAGE_SIZE), 0) < _iota((PAGE_SIZE, PAGE_SIZE), 1)).astype(BF16)
        inside = jnp.dot(eqf.astype(BF16), strict_upper, preferred_element_type=F32)
        cnt = jnp.sum(eqf, axis=-1, keepdims=True)
        rows = []
        for k in range(pg):
            rows.append(inside[k:k + 1] + run)
            run = run + cnt[k:k + 1]
        before = jnp.concatenate(rows, axis=0)
        return jnp.where((key > thr) | (eq & (before < need)), 1.0, 0.0), run

    chosen, run = lax.cond(has_ties, pick_ties, pick_plain, run_s[...])
    run_s[...] = run

    lg = jnp.concatenate([_dot(q8, pages[k][0:HEAD_DIM, :]) for k in range(pg)], axis=-1) * scale
    mask = jnp.concatenate([chosen[k:k + 1] for k in range(pg)], axis=-1) > 0.5
    r8 = _iota((8, 1), 0)
    far = jnp.zeros((8, 1), F32)
    for h in range(N_HEADS):
        far = jnp.where(r8 == h, tab_ref[REL_BUCKETS - 1, N_HEADS + h], far)
    d0 = past - j * nk
    bias = lax.cond(d0 - (nk - 1) >= REL_MAX_DIST,
                    lambda: jnp.broadcast_to(far, (8, nk)),
                    lambda: _bias8(d0 - _iota((1, nk), 1), tab_ref, N_HEADS))
    p, alpha = _softmax_step(lg + bias, mask, m_s, l_s)
    acc = alpha * acc_s[...]
    for k in range(pg):
        acc = acc + _dot_nt(p[:, k * PAGE_SIZE:(k + 1) * PAGE_SIZE], pages[k][HEAD_DIM:2 * HEAD_DIM, :])
    acc_s[...] = acc

    @pl.when(j == pl.num_programs(1) - 1)
    def _():
        key_new = _ordered_key(new_ref[0][:, 0:1])
        limit = jnp.where(has_ties, need, jnp.float32(3.0e38))
        take = (key_new > thr) | ((key_new == thr) & (run < limit))
        new_kv = kv_ref[0]
        lg_self = _self_logit(q8, new_kv[:, 0:HEAD_DIM]) * scale + _bias8(jnp.zeros((1, 1), I32), tab_ref, N_HEADS)
        lg_self = jnp.where(take, lg_self, -jnp.inf)
        _write_heads(o_ref, _online_self(lg_self, new_kv[:, HEAD_DIM:2 * HEAD_DIM], m_s, l_s, acc_s), HEAD_DIM)


def _dsa_att_step(page_table, tie, rel_table, q3, kv3, scores, new, thr, need, cache_t, layer, pg, past):
    bsz, n_pages = page_table.shape
    tok = lambda w: pl.BlockSpec((1, 1, w), lambda b, j, pt, tie: (b, 0, 0))
    grid_spec = pltpu.PrefetchScalarGridSpec(
        num_scalar_prefetch=2,
        grid=(bsz, n_pages // pg),
        in_specs=[pl.BlockSpec(memory_space=pltpu.SMEM), tok(BRANCH_W), tok(2 * HEAD_DIM),
                  pl.BlockSpec((1, pg, PAGE_SIZE), lambda b, j, pt, tie: (b, j, 0)), tok(LANES), tok(LANES), tok(LANES),
                  pl.BlockSpec(memory_space=pl.ANY)],
        out_specs=tok(BRANCH_W),
        scratch_shapes=[pltpu.VMEM((8, 1), F32), pltpu.VMEM((8, 1), F32), pltpu.VMEM((8, HEAD_DIM), F32),
                        pltpu.VMEM((1, 1), F32)] + _page_scratch(pg, 2 * HEAD_DIM))
    return pl.pallas_call(
        functools.partial(_dsa_att_step_kernel, layer=layer, pg=pg, past=past),
        out_shape=jax.ShapeDtypeStruct((bsz, 1, BRANCH_W), F32),
        grid_spec=grid_spec,
        compiler_params=_cp(("arbitrary", "arbitrary")),
        name="dsa_att_step",
    )(page_table, tie, rel_table, q3, kv3, scores, new, thr, need, cache_t)


def _mla_step_kernel(pt_ref, qc_ref, kc_ref, wuv_ref, pool_ref, o_ref, m_s, l_s, acc_s, buf, sem, *, layer, pg):
    slot = _fetch_pages(pt_ref, pool_ref, buf, sem, layer, pg, KV_LORA + QK_ROPE, 0)
    pages = [buf.at[slot, k] for k in range(pg)]
    j = pl.program_id(1)
    scale = MLA_HEAD_QK ** -0.5
    kw = KV_LORA + QK_ROPE

    @pl.when(j == 0)
    def _():
        _flash_init(m_s, l_s, acc_s)

    q8 = _rows8(qc_ref[0], MLA_QW)[:, 0:kw]
    lg = jnp.concatenate([_dot(q8, pages[k][...]) for k in range(pg)], axis=-1) * scale
    p, alpha = _softmax_step(lg, True, m_s, l_s)
    acc = alpha * acc_s[...]
    for k in range(pg):
        acc = acc + _dot_nt(p[:, k * PAGE_SIZE:(k + 1) * PAGE_SIZE], pages[k][0:KV_LORA, :])
    acc_s[...] = acc

    @pl.when(j == pl.num_programs(1) - 1)
    def _():
        new_k = kc_ref[0].astype(F32)
        lg_self = _self_logit(q8, new_k[:, 0:kw]) * scale
        o_lat = _online_self(lg_self, new_k[:, 0:KV_LORA], m_s, l_s, acc_s)
        for h in range(N_HEADS):
            o_ref[0, :, h * V_DIM:(h + 1) * V_DIM] = _dot(o_lat, wuv_ref[h])[h:h + 1, :]


def _mla_step(page_table, qc3, kc3, wuv, cache_mla, layer, pg):
    bsz, n_pages = page_table.shape
    tok = lambda w: pl.BlockSpec((1, 1, w), lambda b, j, pt: (b, 0, 0))
    grid_spec = pltpu.PrefetchScalarGridSpec(
        num_scalar_prefetch=1,
        grid=(bsz, n_pages // pg),
        in_specs=[tok(N_HEADS * MLA_QW), tok(MLA_QW), pl.BlockSpec(wuv.shape, lambda b, j, pt: (0, 0, 0)),
                  pl.BlockSpec(memory_space=pl.ANY)],
        out_specs=tok(BRANCH_W),
        scratch_shapes=[pltpu.VMEM((8, 1), F32), pltpu.VMEM((8, 1), F32), pltpu.VMEM((8, KV_LORA), F32)]
                       + _page_scratch(pg, KV_LORA + QK_ROPE))
    return pl.pallas_call(
        functools.partial(_mla_step_kernel, layer=layer, pg=pg),
        out_shape=jax.ShapeDtypeStruct((bsz, 1, BRANCH_W), F32),
        grid_spec=grid_spec,
        compiler_params=_cp(("arbitrary", "arbitrary")),
        name="mla_step",
    )(page_table, qc3, kc3, wuv, cache_mla)


def _permute_w_in(w_in):
    d = w_in.shape[0]
    o = {}
    off = 0
    for name, w in (('nsa_q', 256), ('nsa_kv', 384), ('nsa_g', 12), ('dsa_q', 256), ('dsa_kv', 128), ('idx_q', 256),
                    ('idx_k', 64), ('idx_w', 4), ('hg', 1024), ('mla_cq', 256), ('mla_ckv', 128), ('mla_kr', 32),
                    ('merge_g', w_in.shape[1] - 2800)):
        o[name] = w_in[:, off:off + w]
        off += w
    pad = jnp.zeros((d, LANES - 112), w_in.dtype)
    small = [o['idx_k'], o['mla_kr'], o['nsa_g'], o['idx_w'], pad]
    cols = [o['nsa_q'], o['nsa_kv'], o['dsa_q'], o['dsa_kv'], o['idx_q'], o['hg'], o['mla_cq'], o['mla_ckv']] + small \
           + [o['merge_g']]
    w_all = jnp.concatenate(cols, axis=1).astype(BF16)
    w_row = jnp.concatenate([o['nsa_kv'], o['dsa_kv'], o['hg'], o['mla_cq'], o['mla_ckv']] + small + [o['merge_g']],
                            axis=1).astype(BF16)
    hd = HEAD_DIM
    v_cols = [o['nsa_kv'][:, hd:2 * hd], o['nsa_kv'][:, 3 * hd:4 * hd], o['nsa_kv'][:, 5 * hd:6 * hd], o['dsa_kv'][:, hd:2 * hd]]
    w_fm = jnp.concatenate([o['nsa_q'], o['dsa_q'], o['idx_q']] + v_cols + [o['nsa_g'], o['idx_w']], axis=1).T.astype(BF16)
    return w_all, w_row, w_fm


def _layer_weights(l, w):
    hq = MLA_HEAD_QK
    wuq = w['w_uq'][l].reshape(Q_LORA, N_HEADS, hq)
    hr = QK_ROPE // 2
    w_all, w_row, w_fm = _permute_w_in(w['w_in'][l])
    return dict(
        w_in=w_all, w_row=w_row, w_fm=w_fm,
        wuvT=jnp.transpose(w['w_uv'][l], (1, 2, 0)).astype(BF16),
        wn=wuq[:, :, :QK_NOPE].reshape(Q_LORA, N_HEADS * QK_NOPE).astype(BF16),
        wr1=wuq[:, :, QK_NOPE:QK_NOPE + hr].reshape(Q_LORA, N_HEADS * hr).astype(BF16),
        wr2=wuq[:, :, QK_NOPE + hr:].reshape(Q_LORA, N_HEADS * hr).astype(BF16),
        wukT=jnp.transpose(w['w_uk'][l], (1, 2, 0)).astype(BF16),
        wuv=jnp.transpose(w['w_uv'][l], (1, 0, 2)).astype(BF16),
        wb=w['w_branch'][l].astype(BF16),
        wo=w['w_out'][l].astype(BF16),
        w_up=w['w_up'][l].astype(BF16),
        w_down=w['w_down'][l].astype(BF16),
        conv_w=w['conv_w'][l],
        conv_b=w['conv_b'][l][None, :],
        g_pre_mix=w['g_pre_mix'][l][None, :], g_post_mix=w['g_post_mix'][l][None, :],
        g_pre_ffn=w['g_pre_ffn'][l][None, :], g_post_ffn=w['g_post_ffn'][l][None, :],
        gq=w['mla_q_norm'][l][None, :], gkv=w['mla_kv_norm'][l][None, :],
        hgn=jnp.tile(w['hg_norm'][l], N_HEADS)[None, :],
    )


def _largest_divisor(n, cap):
    for c in range(min(n, cap), 0, -1):
        if n % c == 0:
            return c
    return 1


def kernel(x_prompt, x_sample, cache_nsa_kv, cache_dsa_kv, cache_dsa_idx, cache_mla, state_nsa_win, state_hgrn, state_ffn_conv, page_table, c_prompt, c_sample, rel_table, w_ada, b_ada, g_pre_mix, g_post_mix, g_pre_ffn, g_post_ffn, w_in, hg_lb, hg_norm, mla_q_norm, mla_kv_norm, w_uq, w_uk, w_uv, w_branch, w_out, w_up, conv_w, conv_b, w_down):
    weights = dict(w_in=w_in, w_uq=w_uq, w_uk=w_uk, w_uv=w_uv, w_branch=w_branch, w_out=w_out, w_up=w_up,
                   w_down=w_down, conv_w=conv_w, conv_b=conv_b, g_pre_mix=g_pre_mix, g_post_mix=g_post_mix,
                   g_pre_ffn=g_pre_ffn, g_post_ffn=g_post_ffn, mla_q_norm=mla_q_norm, mla_kv_norm=mla_kv_norm,
                   hg_norm=hg_norm)
    depth = w_in.shape[0]
    batch, seq, d = x_prompt.shape
    dec, dec_seq, _ = x_sample.shape
    assert dec_seq == 1 and seq % 256 == 0 and dec % 8 == 0
    n_pool = cache_nsa_kv.shape[1]
    n_pages = page_table.shape[1]
    past = n_pages * PAGE_SIZE
    dff = w_down.shape[1]

    tq = 256
    tm_p = 256
    tps_p = seq // tm_p
    tm_s = _largest_divisor(dec, 128)
    pg = _largest_divisor(n_pages, 16)
    pg_cmp = _largest_divisor(n_pages, 32)
    tm_ffn = 512 if seq % 512 == 0 else tm_p

    gam = jax.nn.softmax(hg_lb.astype(F32), axis=0)
    cum = jnp.cumsum(gam, axis=0)
    lower = cum - cum[0]

    inv = ROPE_THETA ** (-jnp.arange(0, QK_ROPE, 2, dtype=F32) / QK_ROPE)
    inv4 = jnp.tile(inv, N_HEADS)[None, :]

    nsel = seq // NSA_SEL_BLOCK
    jj = jnp.arange(2 * nsel, dtype=I32)
    cend = jnp.where(jj < nsel, NSA_SEL_BLOCK * jj + (NSA_CMP_BLOCK - 1),
                     NSA_SEL_BLOCK * (jj - nsel) + (NSA_SEL_BLOCK - 1))[:, None]
    bias_t, bias_cmp = _bias_tiles(rel_table, cend, tq, tq, seq)

    mod = _ada(jnp.concatenate([c_prompt, c_sample], axis=0), w_ada, b_ada)

    nsa_t = jnp.transpose(cache_nsa_kv, (0, 1, 3, 4, 2)).reshape(depth, n_pool, 4 * HEAD_DIM, PAGE_SIZE)
    dsa_t = jnp.transpose(cache_dsa_kv, (0, 1, 3, 4, 2)).reshape(depth, n_pool, 2 * HEAD_DIM, PAGE_SIZE)
    idx_t = jnp.transpose(cache_dsa_idx, (0, 1, 3, 2))
    mla_t = jnp.transpose(cache_mla, (0, 1, 3, 2))
    win_t = jnp.transpose(state_nsa_win, (0, 1, 3, 4, 2)).reshape(depth, dec, 2 * HEAD_DIM, state_nsa_win.shape[2])

    xp = x_prompt.reshape(batch * seq, d)
    xs = x_sample.reshape(dec, d)
    outs_p, outs_s = [], []
    for l in range(depth):
        lw = _layer_weights(l, weights)
        lower_l = lower[l][None, :]
        mp = [mod[l, :batch, k * d:(k + 1) * d][:, None, :] for k in range(6)]
        ms = [mod[l, batch:, k * d:(k + 1) * d][None, :, :] for k in range(6)]

        z = dict(zip(ROW_NAMES + FM_NAMES,
                     _inproj_fm(xp, lw['g_pre_mix'], mp[0], mp[1], lw['w_row'], lw['w_fm'], tm_p, tps_p)))
        qc_t, kc, mla_st, ckv_t = _mla_proj(z['mla_cq'], z['mla_ckv'], z['small'], inv4, lw['gq'], lw['gkv'],
                                            lw['wn'], lw['wr1'], lw['wr2'], lw['wukT'], tm_p, seq, 0, feature_major=True)
        o_nsa = _nsa_prompt(z['nsa_qT'], z['smallT'], z['nsa_kv4'], z['nsa_win'], z['vT'], bias_t, bias_cmp,
                            batch, seq, tq)
        o_dsa = _dsa_prompt(z['dsa_qT'], z['idx_qT'], z['smallT'], z['small'], z['dsa_kv'], z['vT'], bias_t,
                            batch, seq, tq)
        o_mla = _mla_prompt(qc_t, kc, ckv_t, lw['wuvT'], batch, seq, tq)
        o_hg, s_new = _hgrn_prompt(z['hg'], lower_l, lw['hgn'], batch, seq, 256)
        xp = _merge(xp, (o_nsa, o_dsa, o_hg, o_mla), z['merge_g'], mp[2], lw['g_post_mix'], lw['wb'], lw['wo'], tm_p, tps_p)
        xp, csg, csv = _ffn(xp, lw['g_pre_ffn'], mp[3], mp[4], mp[5], lw['g_post_ffn'], lw['w_up'], lw['conv_w'],
                            lw['conv_b'], lw['w_down'], tm_ffn, seq // tm_ffn)
        wl = min(NSA_WINDOW, seq)
        outs_p.append((z['nsa_kv4'].reshape(batch, seq, 4, HEAD_DIM),
                       z['dsa_kv'].reshape(batch, seq, 2, HEAD_DIM),
                       z['small'][:, SM_IDXK:SM_IDXK + IDX_DIM].reshape(batch, seq, IDX_DIM),
                       mla_st.reshape(batch, seq, KV_LORA + QK_ROPE),
                       z['nsa_win'].reshape(batch, seq, 2, HEAD_DIM)[:, seq - wl:],
                       s_new,
                       jnp.concatenate([jnp.swapaxes(csg, 1, 2).reshape(batch, CONV_W - 1, dff),
                                        jnp.swapaxes(csv, 1, 2).reshape(batch, CONV_W - 1, dff)], axis=-1)))

        z = dict(zip(OUT_NAMES, _inproj(xs, lw['g_pre_mix'], ms[0], ms[1], lw['w_in'], tm_s, 1)))
        qc, kc, mla_st = _mla_proj(z['mla_cq'], z['mla_ckv'], z['small'], inv4, lw['gq'], lw['gkv'],
                                   lw['wn'], lw['wr1'], lw['wr2'], lw['wukT'], tm_s, 1, past)
        r3 = lambda a: a.reshape(dec, 1, a.shape[-1])
        q3, sm3, kv3, nw3 = r3(z['nsa_q']), r3(z['small']), r3(z['nsa_kv4']), r3(z['nsa_win'])
        o_cmp, sel_idx = _nsa_cmp_step(page_table, rel_table, q3, nsa_t, l, pg_cmp, past)
        o_nsa = _nsa_sel_step(sel_idx, page_table, rel_table, q3, sm3, kv3, nw3, o_cmp, win_t,
                              nsa_t, l, past)
        scores, sc_new = _dsa_score_step(page_table, r3(z['idx_q']), sm3, idx_t, l, pg)
        thr, need, tie = _dsa_thr_step(scores.reshape(dec, past), sc_new.reshape(dec, LANES),
                                       min(DSA_TOPK_MAX, (past + 1) // 4), 8)
        o_dsa = _dsa_att_step(page_table, tie[:, 0], rel_table, r3(z['dsa_q']), r3(z['dsa_kv']), scores, sc_new,
                              r3(thr), r3(need), dsa_t, l, pg, past)
        o_mla = _mla_step(page_table, r3(qc), r3(kc), lw['wuv'], mla_t, l, pg)
        o_hg, s_new = _hgrn_step(z['hg'], lower_l, lw['hgn'], state_hgrn[l], 8)
        xs = _merge(xs, (o_nsa.reshape(dec, BRANCH_W), o_dsa.reshape(dec, BRANCH_W), o_hg, o_mla.reshape(dec, BRANCH_W)),
                    z['merge_g'], ms[2], lw['g_post_mix'], lw['wb'], lw['wo'], tm_s, 1)
        prev = state_ffn_conv[l]
        xs, ug, uv = _ffn(xs, lw['g_pre_ffn'], ms[3], ms[4], ms[5], lw['g_post_ffn'], lw['w_up'], lw['conv_w'],
                          lw['conv_b'], lw['w_down'], tm_s, 1, prev=(prev[:, 0], prev[:, 1]))
        win_all = jnp.concatenate([state_nsa_win[l], z['nsa_win'].reshape(dec, 1, 2, HEAD_DIM)], axis=1)
        wl = min(NSA_WINDOW, win_all.shape[1])
        outs_s.append((z['nsa_kv4'].reshape(dec, 1, 4, HEAD_DIM),
                       z['dsa_kv'].reshape(dec, 1, 2, HEAD_DIM),
                       z['small'][:, SM_IDXK:SM_IDXK + IDX_DIM].reshape(dec, 1, IDX_DIM),
                       mla_st.reshape(dec, 1, KV_LORA + QK_ROPE),
                       win_all[:, win_all.shape[1] - wl:],
                       s_new,
                       jnp.stack([prev[:, 1], jnp.concatenate([ug, uv], axis=-1)], axis=1)))

    sp = [jnp.stack(v) for v in zip(*outs_p)]
    ss = [jnp.stack(v) for v in zip(*outs_s)]
    return (xp.reshape(batch, seq, d), xs.reshape(dec, 1, d),
            sp[0], ss[0], sp[1], ss[1], sp[2], ss[2], sp[3], ss[3], sp[4], ss[4], sp[5], ss[5], sp[6], ss[6])
```
